```python
import jax
import jax.numpy as jnp
from jax import lax
import numpy as np

D_MODEL = 4096
BATCH = 4
SEQ = 4096
DEPTH = 4

N_A_LAYERS = DEPTH // 2
N_B_LAYERS = DEPTH - N_A_LAYERS
NORM_EPS = 1e-5

GLA_HEADS = 8
GLA_DK = D_MODEL // (2 * GLA_HEADS)
GLA_DV = D_MODEL // GLA_HEADS
GLA_GATE_RANK = 16
GLA_GATE_TAU = 16.0
GLA_CHUNK = 64
GLA_IN = 2 * GLA_HEADS * GLA_DK + 2 * GLA_HEADS * GLA_DV + GLA_GATE_RANK

NSA_HEADS = 32
NSA_GROUPS = 2
NSA_HEAD_DIM = D_MODEL // NSA_HEADS
NSA_IN = NSA_HEADS * NSA_HEAD_DIM + 3 * NSA_HEADS
CMP_BLOCK = 32
CMP_STRIDE = 16
SLC_BLOCK = 64
SLC_TOPK = 8
WINDOW = 512
Q_BLOCK = 128

N_EXPERTS = 32
TOP_K = 4
EXPERT_FF = 384
SWIGLU_ALPHA = 1.702
SWIGLU_LIMIT = 7.0
MOE_ROW_BLOCK = 512

kernel_name = 'yoco_gla_nsa_moe_trunk'

F32 = jnp.float32


def rms_norm(x, gain):
    xf = x.astype(F32)
    y = xf * lax.rsqrt(jnp.mean(xf * xf, axis=-1, keepdims=True) + NORM_EPS)
    return (y * gain.astype(F32)).astype(x.dtype)


def masked_softmax(s, mask):
    s = jnp.where(mask, s, -jnp.inf)
    m = jnp.max(s, axis=-1, keepdims=True)
    m = jnp.where(jnp.isfinite(m), m, 0.0)
    e = jnp.exp(s - m)
    return e / jnp.maximum(jnp.sum(e, axis=-1, keepdims=True), 1e-30)


def alibi_slopes(n_heads):
    return jnp.asarray(2.0 ** (-8.0 * np.arange(1, n_heads + 1, dtype=np.float32) / n_heads), dtype=F32)


def gla_mixer(h, w_in, w_gate2, b_gate2, out_gain, w_out):
    B, T, _ = h.shape
    H, dk, dv, C = GLA_HEADS, GLA_DK, GLA_DV, GLA_CHUNK
    proj = h @ w_in
    q, k, v, g, a = jnp.split(proj, [H * dk, 2 * H * dk, 2 * H * dk + H * dv, 2 * H * dk + 2 * H * dv], axis=-1)
    q = q.astype(F32).reshape(B, T, H, dk) * dk ** -0.5
    k = k.astype(F32).reshape(B, T, H, dk)
    v = v.astype(F32).reshape(B, T, H, dv)
    log_a = jax.nn.log_sigmoid((a @ w_gate2 + b_gate2).astype(F32)).reshape(B, T, H, dk) / GLA_GATE_TAU
    nc = T // C

    def to_chunks(z):
        return z.reshape(B, nc, C, H, -1).transpose(1, 0, 3, 2, 4)

    qc, kc, vc = to_chunks(q), to_chunks(k), to_chunks(v)
    bc = jnp.cumsum(to_chunks(log_a), axis=3)
    causal = jnp.tril(jnp.ones((C, C), dtype=bool))

    def step(S, xs):
        qi, ki, vi, bi = xs
        b_last = bi[:, :, -1:, :]
        qe = qi * jnp.exp(bi)
        ke = ki * jnp.exp(-bi)
        att = jnp.where(causal, jnp.einsum('bhtd,bhsd->bhts', qe, ke), 0.0)
        o = jnp.einsum('bhts,bhsv->bhtv', att, vi) + jnp.einsum('bhtd,bhdv->bhtv', qe, S)
        S = S * jnp.exp(b_last)[:, :, 0, :, None] + jnp.einsum('bhsd,bhsv->bhdv', ki * jnp.exp(b_last - bi), vi)
        return S, o

    S0 = jnp.zeros((B, H, dk, dv), F32)
    _, o = lax.scan(step, S0, (qc, kc, vc, bc))
    o = o.transpose(1, 0, 3, 2, 4).reshape(B, T, H, dv)
    o = rms_norm(o, out_gain) * jax.nn.silu(g.astype(F32)).reshape(B, T, H, dv)
    return o.reshape(B, T, H * dv).astype(h.dtype) @ w_out


def nsa_shared_kv(h, kv_gain, w_kv, k_gain, cmp_pe, cmp_w1, cmp_w2):
    B, T, _ = h.shape
    G, Dh = NSA_GROUPS, NSA_HEAD_DIM
    z = rms_norm(h, kv_gain) @ w_kv
    k_c, v_c, k_s, v_s, k_w, v_w = [u.reshape(B, T, G, Dh) for u in jnp.split(z, 6, axis=-1)]
    r = CMP_BLOCK // CMP_STRIDE
    n_chunks = T // CMP_STRIDE

    def compress(u, pe, w1, w2):
        ch = u.reshape(B, n_chunks, CMP_STRIDE, G, Dh)
        blocks = jnp.concatenate([ch[:, i:n_chunks - r + 1 + i] for i in range(r)], axis=2)
        hid = jax.nn.gelu(jnp.einsum('bnlgd,lde->bnge', blocks + pe[:, None, :], w1))
        return hid @ w2

    kc = rms_norm(compress(k_c, cmp_pe[0], cmp_w1[0], cmp_w2[0]), k_gain[0])
    vc = compress(v_c, cmp_pe[1], cmp_w1[1], cmp_w2[1])
    ks = rms_norm(k_s, k_gain[1])
    kw = rms_norm(k_w, k_gain[2])
    return kc, vc, ks, v_s, kw, v_w


def nsa_mixer(h, w_in, q_gain, w_out, kc, vc, ks, vs, kw, vw):
    B, T, _ = h.shape
    H, G, Dh = NSA_HEADS, NSA_GROUPS, NSA_HEAD_DIM
    HG = H // G
    L = SLC_BLOCK
    proj = h @ w_in
    q, gate_logits = jnp.split(proj, [H * Dh], axis=-1)
    q = rms_norm(q.reshape(B, T, G, HG, Dh), q_gain) * Dh ** -0.5
    gates = jax.nn.sigmoid(gate_logits.astype(F32)).reshape(B, T, G, HG, 3)
    slopes = alibi_slopes(H).reshape(G, HG)[None, :, :, None, None]

    r = CMP_BLOCK // CMP_STRIDE
    n_cmp = T // CMP_STRIDE - r + 1
    n_slc = T // L
    k_sel = min(SLC_TOPK, n_slc)
    chunks_per_slc = L // CMP_STRIDE
    cmp_end = jnp.arange(n_cmp) * CMP_STRIDE + CMP_BLOCK - 1
    blk_ids = jnp.arange(n_slc)
    ks_blocks = ks.reshape(B, n_slc, L, G, Dh).transpose(0, 3, 1, 2, 4)
    vs_blocks = vs.reshape(B, n_slc, L, G, Dh).transpose(0, 3, 1, 2, 4)
    kw_pad = jnp.pad(kw, ((0, 0), (WINDOW, 0), (0, 0), (0, 0)))
    vw_pad = jnp.pad(vw, ((0, 0), (WINDOW, 0), (0, 0), (0, 0)))
    b_idx = jnp.arange(B)[:, None, None, None]
    g_idx = jnp.arange(G)[None, :, None, None]

    def query_block(qb):
        s0 = qb * Q_BLOCK
        t = s0 + jnp.arange(Q_BLOCK)
        qi = lax.dynamic_slice_in_dim(q, s0, Q_BLOCK, axis=1)
        gq = lax.dynamic_slice_in_dim(gates, s0, Q_BLOCK, axis=1)

        s = jnp.einsum('bqghd,bngd->bghqn', qi, kc, preferred_element_type=F32)
        s = s - slopes * (t[:, None] - cmp_end[None, :])
        p_cmp = masked_softmax(s, cmp_end[None, :] <= t[:, None])
        o_cmp = jnp.einsum('bghqn,bngd->bqghd', p_cmp, vc)

        imp = jnp.sum(p_cmp, axis=2)
        chunk_imp = sum(jnp.pad(imp, ((0, 0), (0, 0), (0, 0), (i, r - 1 - i))) for i in range(r))
        slc_imp = chunk_imp.reshape(B, G, Q_BLOCK, n_slc, chunks_per_slc).sum(-1)
        cur = (t // L)[:, None]
        forced = (blk_ids == 0) | (blk_ids == cur) | (blk_ids == cur - 1)
        slc_score = jnp.where(forced, jnp.inf, jnp.where(blk_ids <= cur, slc_imp, -jnp.inf))
        _, idx = lax.top_k(slc_score, k_sel)

        ksel = ks_blocks[b_idx, g_idx, idx]
        vsel = vs_blocks[b_idx, g_idx, idx]
        pos = (idx[..., None] * L + jnp.arange(L)).reshape(B, G, Q_BLOCK, k_sel * L)
        s = jnp.einsum('bqghd,bgqkld->bghqkl', qi, ksel, preferred_element_type=F32)
        s = s.reshape(B, G, HG, Q_BLOCK, k_sel * L) - slopes * (t[:, None] - pos[:, :, None])
        p_slc = masked_softmax(s, (pos <= t[:, None])[:, :, None])
        o_slc = jnp.einsum('bghqkl,bgqkld->bqghd', p_slc.reshape(B, G, HG, Q_BLOCK, k_sel, L), vsel)

        kwin = lax.dynamic_slice_in_dim(kw_pad, s0, WINDOW + Q_BLOCK, axis=1)
        vwin = lax.dynamic_slice_in_dim(vw_pad, s0, WINDOW + Q_BLOCK, axis=1)
        wpos = s0 - WINDOW + jnp.arange(WINDOW + Q_BLOCK)
        dist = t[:, None] - wpos[None, :]
        s = jnp.einsum('bqghd,bkgd->bghqk', qi, kwin, preferred_element_type=F32) - slopes * dist
        p_win = masked_softmax(s, (dist >= 0) & (dist < WINDOW) & (wpos >= 0)[None, :])
        o_win = jnp.einsum('bghqk,bkgd->bqghd', p_win, vwin)

        return gq[..., 0:1] * o_cmp + gq[..., 1:2] * o_slc + gq[..., 2:3] * o_win

    o = lax.map(query_block, jnp.arange(T // Q_BLOCK))
    o = o.transpose(1, 0, 2, 3, 4, 5).reshape(B, T, H * Dh).astype(h.dtype)
    return o @ w_out


def moe(h, w_router, b_router, w_gate_up, b_gate_up, w_down, b_down):
    B, T, D = h.shape
    E, M = N_EXPERTS, MOE_ROW_BLOCK
    n_tok = B * T
    nk = n_tok * TOP_K
    xf = h.reshape(n_tok, D)
    logits = (xf @ w_router + b_router).astype(F32)
    top_val, top_idx = lax.top_k(logits, TOP_K)
    weights = jax.nn.softmax(top_val, axis=-1)
    flat_e = top_idx.reshape(-1)
    order = jnp.argsort(flat_e)
    e_sorted = flat_e[order]
    tok_sorted = order // TOP_K
    sizes = jnp.bincount(flat_e, length=E).astype(jnp.int32)
    padded = (sizes + M - 1) // M * M
    pad_end = jnp.cumsum(padded)
    pad_start = pad_end - padded
    raw_start = jnp.cumsum(sizes) - sizes
    dest = pad_start[e_sorted] + (jnp.arange(nk, dtype=jnp.int32) - raw_start[e_sorted])
    n_rows = (nk + M - 1) // M * M + E * M
    n_blocks = n_rows // M
    tok_buf = jnp.zeros((n_rows,), jnp.int32).at[dest].set(tok_sorted)
    blk_e = jnp.minimum(jnp.searchsorted(pad_end, jnp.arange(n_blocks, dtype=jnp.int32) * M, side='right'), E - 1)
    xb = xf[tok_buf].reshape(n_blocks, M, D)
    gu = jnp.einsum('nmd,ndf->nmf', xb, w_gate_up[blk_e]) + b_gate_up[blk_e][:, None, :]
    gate, up = jnp.split(gu, 2, axis=-1)
    gate = jnp.minimum(gate, SWIGLU_LIMIT)
    up = jnp.clip(up, -SWIGLU_LIMIT, SWIGLU_LIMIT)
    act = (up + 1.0) * (gate * jax.nn.sigmoid(SWIGLU_ALPHA * gate))
    y = jnp.einsum('nmf,nfd->nmd', act, w_down[blk_e]) + b_down[blk_e][:, None, :]
    slot_dest = jnp.zeros((nk,), jnp.int32).at[order].set(dest)
    y_tok = y.reshape(n_rows, D)[slot_dest].reshape(n_tok, TOP_K, D)
    out = jnp.einsum('nkd,nk->nd', y_tok, weights.astype(y_tok.dtype))
    return out.reshape(B, T, D)


def setup_inputs(seed: int = 0) -> dict:
    key = jax.random.key(seed)
    ks = jax.random.split(key, 24)
    D, G, Dh, H = D_MODEL, NSA_GROUPS, NSA_HEAD_DIM, NSA_HEADS
    out_scale = (2.0 * DEPTH) ** -0.5

    def nrm(k, shape, scale):
        return jax.random.normal(k, shape, F32) * scale

    def gain(k, shape):
        return 1.0 + 0.02 * jax.random.normal(k, shape, F32)

    return {
        'x': nrm(ks[0], (BATCH, SEQ, D), 1.0),
        'ln_mix': gain(ks[1], (DEPTH, D)),
        'ln_ffn': gain(ks[2], (DEPTH, D)),
        'a_w_in': nrm(ks[3], (N_A_LAYERS, D, GLA_IN), D ** -0.5),
        'a_w_gate2': nrm(ks[4], (N_A_LAYERS, GLA_GATE_RANK, GLA_HEADS * GLA_DK), GLA_GATE_RANK ** -0.5),
        'a_b_gate2': nrm(ks[5], (N_A_LAYERS, GLA_HEADS * GLA_DK), 0.1),
        'a_out_gain': gain(ks[6], (N_A_LAYERS, GLA_DV)),
        'a_w_out': nrm(ks[7], (N_A_LAYERS, D, D), D ** -0.5 * out_scale),
        'kv_gain': gain(ks[8], (D,)),
        'w_kv': nrm(ks[9], (D, 6 * G * Dh), D ** -0.5),
        'k_gain': gain(ks[10], (3, Dh)),
        'cmp_pe': nrm(ks[11], (2, CMP_BLOCK, Dh), 0.02),
        'cmp_w1': nrm(ks[12], (2, CMP_BLOCK, Dh, Dh), (CMP_BLOCK * Dh) ** -0.5),
        'cmp_w2': nrm(ks[13], (2, Dh, Dh), Dh ** -0.5),
        'b_w_in': nrm(ks[14], (N_B_LAYERS, D, NSA_IN), D ** -0.5),
        'b_q_gain': gain(ks[15], (N_B_LAYERS, Dh)),
        'b_w_out': nrm(ks[16], (N_B_LAYERS, H * Dh, D), (H * Dh) ** -0.5 * out_scale),
        'w_router': nrm(ks[17], (DEPTH, D, N_EXPERTS), D ** -0.5),
        'b_router': nrm(ks[18], (DEPTH, N_EXPERTS), 0.01),
        'w_gate_up': nrm(ks[19], (DEPTH, N_EXPERTS, D, 2 * EXPERT_FF), D ** -0.5),
        'b_gate_up': nrm(ks[20], (DEPTH, N_EXPERTS, 2 * EXPERT_FF), 0.01),
        'w_down': nrm(ks[21], (DEPTH, N_EXPERTS, EXPERT_FF, D), EXPERT_FF ** -0.5 * out_scale),
        'b_down': nrm(ks[22], (DEPTH, N_EXPERTS, D), 0.01),
    }


def reference(x, ln_mix, ln_ffn, a_w_in, a_w_gate2, a_b_gate2, a_out_gain, a_w_out,
              kv_gain, w_kv, k_gain, cmp_pe, cmp_w1, cmp_w2, b_w_in, b_q_gain, b_w_out,
              w_router, b_router, w_gate_up, b_gate_up, w_down, b_down):
    h = x
    shared = None
    for l in range(DEPTH):
        if l < N_A_LAYERS:
            h = h + gla_mixer(rms_norm(h, ln_mix[l]), a_w_in[l], a_w_gate2[l], a_b_gate2[l],
                              a_out_gain[l], a_w_out[l])
        else:
            if l == N_A_LAYERS:
                shared = nsa_shared_kv(h, kv_gain, w_kv, k_gain, cmp_pe, cmp_w1, cmp_w2)
            j = l - N_A_LAYERS
            h = h + nsa_mixer(rms_norm(h, ln_mix[l]), b_w_in[j], b_q_gain[j], b_w_out[j], *shared)
        h = h + moe(rms_norm(h, ln_ffn[l]), w_router[l], b_router[l], w_gate_up[l], b_gate_up[l],
                    w_down[l], b_down[l])
    return h
```

```python
import functools

import jax
import jax.numpy as jnp
from jax import lax
import numpy as np
from jax.experimental import pallas as pl
from jax.experimental.pallas import tpu as pltpu

F32 = jnp.float32
BF16 = jnp.bfloat16

NORM_EPS = 1e-5
GLA_HEADS = 8
GLA_GATE_RANK = 16
GLA_GATE_TAU = 16.0
GLA_CHUNK = 64
NSA_HEADS = 32
NSA_GROUPS = 2
CMP_BLOCK = 32
CMP_STRIDE = 16
SLC_BLOCK = 64
SLC_TOPK = 8
WINDOW = 512
Q_BLOCK = 128
N_EXPERTS = 32
TOP_K = 4
SWIGLU_ALPHA = 1.702
SWIGLU_LIMIT = 7.0
MOE_ROW_BLOCK = 512

LANES = 128
VMEM_LIMIT_BYTES = 56 * 1024 * 1024


def _rmsnorm_body(x_ref, g_ref, o_ref):
    x = x_ref[...].astype(F32)
    ms = jnp.mean(x * x, axis=-1, keepdims=True)
    o_ref[...] = (x * lax.rsqrt(ms + NORM_EPS) * g_ref[...].astype(F32)).astype(o_ref.dtype)


def rmsnorm(x, gain, out_dtype=BF16, tm=512):
    m, d = x.shape
    return pl.pallas_call(
        _rmsnorm_body,
        grid=(m // tm,),
        in_specs=[pl.BlockSpec((tm, d), lambda i: (i, 0)),
                  pl.BlockSpec((1, d), lambda i: (0, 0))],
        out_specs=pl.BlockSpec((tm, d), lambda i: (i, 0)),
        out_shape=jax.ShapeDtypeStruct((m, d), out_dtype),
        compiler_params=pltpu.CompilerParams(
            dimension_semantics=("arbitrary",), vmem_limit_bytes=VMEM_LIMIT_BYTES),
        name="rmsnorm",
    )(x, gain.reshape(1, d))


def _matmul_body(a_ref, w_ref, o_ref):
    o_ref[...] = jnp.dot(a_ref[...], w_ref[...], preferred_element_type=F32).astype(o_ref.dtype)


def _matmul_res_body(a_ref, w_ref, r_ref, o_ref):
    acc = jnp.dot(a_ref[...], w_ref[...], preferred_element_type=F32)
    o_ref[...] = (r_ref[...].astype(F32) + acc).astype(o_ref.dtype)


def matmul(a, w, residual=None, out_dtype=F32, tm=1024, tn=512):
    m, k = a.shape
    _, n = w.shape
    tm = min(tm, m)
    tn = min(tn, n)
    assert m % tm == 0 and n % tn == 0, (m, n, tm, tn)
    in_specs = [pl.BlockSpec((tm, k), lambda i, j: (i, 0)),
                pl.BlockSpec((k, tn), lambda i, j: (0, j))]
    args = [a, w]
    body = _matmul_body
    if residual is not None:
        in_specs.append(pl.BlockSpec((tm, tn), lambda i, j: (i, j)))
        args.append(residual)
        body = _matmul_res_body
    return pl.pallas_call(
        body,
        grid=(m // tm, n // tn),
        in_specs=in_specs,
        out_specs=pl.BlockSpec((tm, tn), lambda i, j: (i, j)),
        out_shape=jax.ShapeDtypeStruct((m, n), out_dtype),
        compiler_params=pltpu.CompilerParams(
            dimension_semantics=("arbitrary", "arbitrary"), vmem_limit_bytes=VMEM_LIMIT_BYTES),
        name="matmul",
    )(*args)


def _pad_cols(w, mult=LANES):
    n = w.shape[-1]
    pad = (-n) % mult
    return jnp.pad(w, ((0, 0), (0, pad))) if pad else w


def _rms(x, gain):
    xf = x.astype(F32)
    y = xf * lax.rsqrt(jnp.mean(xf * xf, axis=-1, keepdims=True) + NORM_EPS)
    return y * gain.astype(F32)


def _masked_softmax(s, mask):
    s = jnp.where(mask, s, -jnp.inf)
    m = jnp.max(s, axis=-1, keepdims=True)
    m = jnp.where(jnp.isfinite(m), m, 0.0)
    e = jnp.exp(s - m)
    return e / jnp.maximum(jnp.sum(e, axis=-1, keepdims=True), 1e-30)


def _gla_mixer(h, xn, w_in, w_gate2, b_gate2, out_gain, w_out):
    B, T, D = h.shape
    H, C = GLA_HEADS, GLA_CHUNK
    dk, dv = D // (2 * H), D // H
    n_main = 2 * H * dk + 2 * H * dv
    proj = matmul(xn, w_in[:, :n_main].astype(BF16)).reshape(B, T, n_main)
    a = matmul(xn, _pad_cols(w_in[:, n_main:]).astype(BF16))[:, :GLA_GATE_RANK].reshape(B, T, -1)
    q, k, v, g = jnp.split(proj, [H * dk, 2 * H * dk, 2 * H * dk + H * dv], axis=-1)
    q = q.reshape(B, T, H, dk) * dk ** -0.5
    k = k.reshape(B, T, H, dk)
    v = v.reshape(B, T, H, dv)
    log_a = jax.nn.log_sigmoid(a @ w_gate2 + b_gate2).reshape(B, T, H, dk) / GLA_GATE_TAU
    nc = T // C

    def to_chunks(z):
        return z.reshape(B, nc, C, H, -1).transpose(1, 0, 3, 2, 4)

    qc, kc, vc = to_chunks(q), to_chunks(k), to_chunks(v)
    bc = jnp.cumsum(to_chunks(log_a), axis=3)
    causal = jnp.tril(jnp.ones((C, C), dtype=bool))

    def step(S, xs):
        qi, ki, vi, bi = xs
        b_last = bi[:, :, -1:, :]
        qe = qi * jnp.exp(bi)
        ke = ki * jnp.exp(-bi)
        att = jnp.where(causal, jnp.einsum('bhtd,bhsd->bhts', qe, ke), 0.0)
        o = jnp.einsum('bhts,bhsv->bhtv', att, vi) + jnp.einsum('bhtd,bhdv->bhtv', qe, S)
        S = S * jnp.exp(b_last)[:, :, 0, :, None] + jnp.einsum('bhsd,bhsv->bhdv', ki * jnp.exp(b_last - bi), vi)
        return S, o

    S0 = jnp.zeros((B, H, dk, dv), F32)
    _, o = lax.scan(step, S0, (qc, kc, vc, bc))
    o = o.transpose(1, 0, 3, 2, 4).reshape(B, T, H, dv)
    o = _rms(o, out_gain) * jax.nn.silu(g).reshape(B, T, H, dv)
    o = o.reshape(B * T, H * dv).astype(BF16)
    return matmul(o, w_out.astype(BF16), residual=h.reshape(B * T, D)).reshape(B, T, D)


def _nsa_shared_kv(h, kv_gain, w_kv, k_gain, cmp_pe, cmp_w1, cmp_w2):
    B, T, D = h.shape
    G = NSA_GROUPS
    Dh = D // NSA_HEADS
    xn = rmsnorm(h.reshape(B * T, D), kv_gain)
    z = matmul(xn, w_kv.astype(BF16)).reshape(B, T, -1)
    k_c, v_c, k_s, v_s, k_w, v_w = [u.reshape(B, T, G, Dh) for u in jnp.split(z, 6, axis=-1)]
    r = CMP_BLOCK // CMP_STRIDE
    n_chunks = T // CMP_STRIDE

    def compress(u, pe, w1, w2):
        ch = u.reshape(B, n_chunks, CMP_STRIDE, G, Dh)
        blocks = jnp.concatenate([ch[:, i:n_chunks - r + 1 + i] for i in range(r)], axis=2)
        hid = jax.nn.gelu(jnp.einsum('bnlgd,lde->bnge', blocks + pe[:, None, :], w1))
        return hid @ w2

    kc = _rms(compress(k_c, cmp_pe[0], cmp_w1[0], cmp_w2[0]), k_gain[0])
    vc = compress(v_c, cmp_pe[1], cmp_w1[1], cmp_w2[1])
    ks = _rms(k_s, k_gain[1])
    kw = _rms(k_w, k_gain[2])
    return kc, vc, ks, v_s, kw, v_w


def _nsa_mixer(h, xn, w_in, q_gain, w_out, kc, vc, ks, vs, kw, vw):
    B, T, D = h.shape
    H, G = NSA_HEADS, NSA_GROUPS
    Dh = D // H
    HG = H // G
    L = SLC_BLOCK
    q = matmul(xn, w_in[:, :H * Dh].astype(BF16)).reshape(B, T, H * Dh)
    gate_logits = matmul(xn, _pad_cols(w_in[:, H * Dh:]).astype(BF16))[:, :3 * H].reshape(B, T, 3 * H)
    q = _rms(q.reshape(B, T, G, HG, Dh), q_gain) * Dh ** -0.5
    gates = jax.nn.sigmoid(gate_logits).reshape(B, T, G, HG, 3)
    slopes = jnp.asarray(2.0 ** (-8.0 * np.arange(1, H + 1, dtype=np.float32) / H), dtype=F32)
    slopes = slopes.reshape(G, HG)[None, :, :, None, None]

    r = CMP_BLOCK // CMP_STRIDE
    n_cmp = T // CMP_STRIDE - r + 1
    n_slc = T // L
    k_sel = min(SLC_TOPK, n_slc)
    chunks_per_slc = L // CMP_STRIDE
    cmp_end = jnp.arange(n_cmp) * CMP_STRIDE + CMP_BLOCK - 1
    blk_ids = jnp.arange(n_slc)
    ks_blocks = ks.reshape(B, n_slc, L, G, Dh).transpose(0, 3, 1, 2, 4)
    vs_blocks = vs.reshape(B, n_slc, L, G, Dh).transpose(0, 3, 1, 2, 4)
    kw_pad = jnp.pad(kw, ((0, 0), (WINDOW, 0), (0, 0), (0, 0)))
    vw_pad = jnp.pad(vw, ((0, 0), (WINDOW, 0), (0, 0), (0, 0)))
    b_idx = jnp.arange(B)[:, None, None, None]
    g_idx = jnp.arange(G)[None, :, None, None]

    def query_block(qb):
        s0 = qb * Q_BLOCK
        t = s0 + jnp.arange(Q_BLOCK)
        qi = lax.dynamic_slice_in_dim(q, s0, Q_BLOCK, axis=1)
        gq = lax.dynamic_slice_in_dim(gates, s0, Q_BLOCK, axis=1)

        s = jnp.einsum('bqghd,bngd->bghqn', qi, kc, preferred_element_type=F32)
        s = s - slopes * (t[:, None] - cmp_end[None, :])
        p_cmp = _masked_softmax(s, cmp_end[None, :] <= t[:, None])
        o_cmp = jnp.einsum('bghqn,bngd->bqghd', p_cmp, vc)

        imp = jnp.sum(p_cmp, axis=2)
        chunk_imp = sum(jnp.pad(imp, ((0, 0), (0, 0), (0, 0), (i, r - 1 - i))) for i in range(r))
        slc_imp = chunk_imp.reshape(B, G, Q_BLOCK, n_slc, chunks_per_slc).sum(-1)
        cur = (t // L)[:, None]
        forced = (blk_ids == 0) | (blk_ids == cur) | (blk_ids == cur - 1)
        slc_score = jnp.where(forced, jnp.inf, jnp.where(blk_ids <= cur, slc_imp, -jnp.inf))
        _, idx = lax.top_k(slc_score, k_sel)

        ksel = ks_blocks[b_idx, g_idx, idx]
        vsel = vs_blocks[b_idx, g_idx, idx]
        pos = (idx[..., None] * L + jnp.arange(L)).reshape(B, G, Q_BLOCK, k_sel * L)
        s = jnp.einsum('bqghd,bgqkld->bghqkl', qi, ksel, preferred_element_type=F32)
        s = s.reshape(B, G, HG, Q_BLOCK, k_sel * L) - slopes * (t[:, None] - pos[:, :, None])
        p_slc = _masked_softmax(s, (pos <= t[:, None])[:, :, None])
        o_slc = jnp.einsum('bghqkl,bgqkld->bqghd', p_slc.reshape(B, G, HG, Q_BLOCK, k_sel, L), vsel)

        kwin = lax.dynamic_slice_in_dim(kw_pad, s0, WINDOW + Q_BLOCK, axis=1)
        vwin = lax.dynamic_slice_in_dim(vw_pad, s0, WINDOW + Q_BLOCK, axis=1)
        wpos = s0 - WINDOW + jnp.arange(WINDOW + Q_BLOCK)
        dist = t[:, None] - wpos[None, :]
        s = jnp.einsum('bqghd,bkgd->bghqk', qi, kwin, preferred_element_type=F32) - slopes * dist
        p_win = _masked_softmax(s, (dist >= 0) & (dist < WINDOW) & (wpos >= 0)[None, :])
        o_win = jnp.einsum('bghqk,bkgd->bqghd', p_win, vwin)

        return gq[..., 0:1] * o_cmp + gq[..., 1:2] * o_slc + gq[..., 2:3] * o_win

    o = lax.map(query_block, jnp.arange(T // Q_BLOCK))
    o = o.transpose(1, 0, 2, 3, 4, 5).reshape(B * T, H * Dh).astype(BF16)
    return matmul(o, w_out.astype(BF16), residual=h.reshape(B * T, D)).reshape(B, T, D)


def _moe(h, xn, w_router, b_router, w_gate_up, b_gate_up, w_down, b_down):
    B, T, D = h.shape
    E, M = N_EXPERTS, MOE_ROW_BLOCK
    n_tok = B * T
    nk = n_tok * TOP_K
    logits = matmul(xn, _pad_cols(w_router).astype(BF16))[:, :E] + b_router
    top_val, top_idx = lax.top_k(logits, TOP_K)
    weights = jax.nn.softmax(top_val, axis=-1)
    flat_e = top_idx.reshape(-1)
    order = jnp.argsort(flat_e)
    e_sorted = flat_e[order]
    tok_sorted = order // TOP_K
    sizes = jnp.bincount(flat_e, length=E).astype(jnp.int32)
    padded = (sizes + M - 1) // M * M
    pad_end = jnp.cumsum(padded)
    pad_start = pad_end - padded
    raw_start = jnp.cumsum(sizes) - sizes
    dest = pad_start[e_sorted] + (jnp.arange(nk, dtype=jnp.int32) - raw_start[e_sorted])
    n_rows = (nk + M - 1) // M * M + E * M
    n_blocks = n_rows // M
    tok_buf = jnp.zeros((n_rows,), jnp.int32).at[dest].set(tok_sorted)
    blk_e = jnp.minimum(jnp.searchsorted(pad_end, jnp.arange(n_blocks, dtype=jnp.int32) * M, side='right'), E - 1)
    xb = xn[tok_buf].reshape(n_blocks, M, D)
    gu = jnp.einsum('nmd,ndf->nmf', xb, w_gate_up.astype(BF16)[blk_e],
                    preferred_element_type=F32) + b_gate_up[blk_e][:, None, :]
    gate, up = jnp.split(gu, 2, axis=-1)
    gate = jnp.minimum(gate, SWIGLU_LIMIT)
    up = jnp.clip(up, -SWIGLU_LIMIT, SWIGLU_LIMIT)
    act = (up + 1.0) * (gate * jax.nn.sigmoid(SWIGLU_ALPHA * gate))
    y = jnp.einsum('nmf,nfd->nmd', act.astype(BF16), w_down.astype(BF16)[blk_e],
                   preferred_element_type=F32) + b_down[blk_e][:, None, :]
    slot_dest = jnp.zeros((nk,), jnp.int32).at[order].set(dest)
    y_tok = y.reshape(n_rows, D)[slot_dest].reshape(n_tok, TOP_K, D)
    out = jnp.einsum('nkd,nk->nd', y_tok, weights)
    return h + out.reshape(B, T, D)


def kernel(x, ln_mix, ln_ffn, a_w_in, a_w_gate2, a_b_gate2, a_out_gain, a_w_out, kv_gain, w_kv, k_gain,
           cmp_pe, cmp_w1, cmp_w2, b_w_in, b_q_gain, b_w_out, w_router, b_router, w_gate_up, b_gate_up,
           w_down, b_down):
    B, T, D = x.shape
    depth = ln_mix.shape[0]
    n_a = a_w_in.shape[0]
    h = x
    shared = None
    for l in range(depth):
        xn = rmsnorm(h.reshape(B * T, D), ln_mix[l])
        if l < n_a:
            h = _gla_mixer(h, xn, a_w_in[l], a_w_gate2[l], a_b_gate2[l], a_out_gain[l], a_w_out[l])
        else:
            if l == n_a:
                shared = _nsa_shared_kv(h, kv_gain, w_kv, k_gain, cmp_pe, cmp_w1, cmp_w2)
            j = l - n_a
            h = _nsa_mixer(h, xn, b_w_in[j], b_q_gain[j], b_w_out[j], *shared)
        xn = rmsnorm(h.reshape(B * T, D), ln_ffn[l])
        h = _moe(h, xn, w_router[l], b_router[l], w_gate_up[l], b_gate_up[l], w_down[l], b_down[l])
    return h
```

```python
import functools

import jax
import jax.numpy as jnp
from jax import lax
import numpy as np
from jax.experimental import pallas as pl
from jax.experimental.pallas import tpu as pltpu

F32 = jnp.float32
BF16 = jnp.bfloat16
I32 = jnp.int32
U32 = jnp.uint32

NORM_EPS = 1e-5
GLA_HEADS = 8
GLA_GATE_RANK = 16
GLA_GATE_TAU = 16.0
GLA_CHUNK = 64
NSA_HEADS = 32
NSA_GROUPS = 2
NSA_HG = NSA_HEADS // NSA_GROUPS
CMP_BLOCK = 32
CMP_STRIDE = 16
SLC_BLOCK = 64
SLC_TOPK = 8
WINDOW = 512
Q_BLOCK = 128
N_EXPERTS = 32
TOP_K = 4
SWIGLU_ALPHA = 1.702
SWIGLU_LIMIT = 7.0

LANES = 128
VMEM_LIMIT_BYTES = 56 * 1024 * 1024
MOE_BLOCK_ROWS = 256
MOE_COMBINE_TOKENS = 128
SLC_KEY_CHUNK = 512
NEG_INF = float("-inf")


def _params(*sem):
    return pltpu.CompilerParams(dimension_semantics=sem, vmem_limit_bytes=VMEM_LIMIT_BYTES)


def _rmsnorm_body(x_ref, g_ref, o_ref):
    x = x_ref[...].astype(F32)
    ms = jnp.mean(x * x, axis=-1, keepdims=True)
    o_ref[...] = (x * lax.rsqrt(ms + NORM_EPS) * g_ref[...].astype(F32)).astype(o_ref.dtype)


def rmsnorm(x, gain, out_dtype=BF16, tm=512):
    m, d = x.shape
    tm = min(tm, m)
    return pl.pallas_call(
        _rmsnorm_body,
        grid=(m // tm,),
        in_specs=[pl.BlockSpec((tm, d), lambda i: (i, 0)),
                  pl.BlockSpec((1, d), lambda i: (0, 0))],
        out_specs=pl.BlockSpec((tm, d), lambda i: (i, 0)),
        out_shape=jax.ShapeDtypeStruct((m, d), out_dtype),
        compiler_params=_params("arbitrary"),
        name="rmsnorm",
    )(x, gain.reshape(1, d))


def _matmul_body(a_ref, w_ref, o_ref):
    o_ref[...] = jnp.dot(a_ref[...], w_ref[...], preferred_element_type=F32).astype(o_ref.dtype)


def _matmul_res_body(a_ref, w_ref, r_ref, o_ref):
    acc = jnp.dot(a_ref[...], w_ref[...], preferred_element_type=F32)
    o_ref[...] = (r_ref[...].astype(F32) + acc).astype(o_ref.dtype)


def matmul(a, w, residual=None, out_dtype=F32, tm=1024, tn=512):
    m, k = a.shape
    _, n = w.shape
    tm = min(tm, m)
    tn = min(tn, n)
    assert m % tm == 0 and n % tn == 0, (m, n, tm, tn)
    in_specs = [pl.BlockSpec((tm, k), lambda i, j: (i, 0)),
                pl.BlockSpec((k, tn), lambda i, j: (0, j))]
    args = [a, w]
    body = _matmul_body
    if residual is not None:
        in_specs.append(pl.BlockSpec((tm, tn), lambda i, j: (i, j)))
        args.append(residual)
        body = _matmul_res_body
    return pl.pallas_call(
        body,
        grid=(m // tm, n // tn),
        in_specs=in_specs,
        out_specs=pl.BlockSpec((tm, tn), lambda i, j: (i, j)),
        out_shape=jax.ShapeDtypeStruct((m, n), out_dtype),
        compiler_params=_params("arbitrary", "arbitrary"),
        name="matmul",
    )(*args)


def _matmul_groupnorm_body(a_ref, w_ref, g_ref, o_ref, *, norm_groups, scale, split_out):
    acc = jnp.dot(a_ref[...], w_ref[...], preferred_element_type=F32)
    for c, do_norm in enumerate(norm_groups):
        seg = acc[:, c * LANES:(c + 1) * LANES]
        if do_norm:
            ms = jnp.mean(seg * seg, axis=-1, keepdims=True)
            seg = seg * lax.rsqrt(ms + NORM_EPS) * g_ref[:, c * LANES:(c + 1) * LANES] * scale
        if split_out:
            o_ref[c] = seg.astype(o_ref.dtype)
        else:
            o_ref[:, c * LANES:(c + 1) * LANES] = seg.astype(o_ref.dtype)


def matmul_groupnorm(a, w, gain_cols, norm_groups, scale=1.0, split_out=False, out_dtype=BF16, tm=512, tn=512):
    m, k = a.shape
    _, n = w.shape
    tm = min(tm, m)
    tn = min(tn, n)
    assert m % tm == 0 and n % tn == 0 and len(norm_groups) == tn // LANES
    if split_out:
        out_shape = jax.ShapeDtypeStruct((n // LANES, m, LANES), out_dtype)
        out_spec = pl.BlockSpec((tn // LANES, tm, LANES), lambda i, j: (j, i, 0))
    else:
        out_shape = jax.ShapeDtypeStruct((m, n), out_dtype)
        out_spec = pl.BlockSpec((tm, tn), lambda i, j: (i, j))
    body = functools.partial(_matmul_groupnorm_body, norm_groups=tuple(norm_groups), scale=scale,
                             split_out=split_out)
    return pl.pallas_call(
        body,
        grid=(m // tm, n // tn),
        in_specs=[pl.BlockSpec((tm, k), lambda i, j: (i, 0)),
                  pl.BlockSpec((k, tn), lambda i, j: (0, j)),
                  pl.BlockSpec((1, tn), lambda i, j: (0, j))],
        out_specs=out_spec,
        out_shape=out_shape,
        compiler_params=_params("arbitrary", "arbitrary"),
        name="matmul_groupnorm",
    )(a, w, gain_cols.reshape(1, n).astype(F32))


def _pad_cols(w, mult=LANES):
    pad = (-w.shape[-1]) % mult
    return jnp.pad(w, ((0, 0), (0, pad))) if pad else w


def _nsa_compress_body(u_ref, w1_ref, w2_ref, pe_ref, kg_ref, o_ref):
    kv = pl.program_id(0)
    u = u_ref[0, 0]
    half = u.shape[1]
    w1 = w1_ref[0]
    a = jnp.dot(u, w1[:half], preferred_element_type=F32)
    b = jnp.dot(u, w1[half:], preferred_element_type=F32)
    pe_term = jnp.dot(pe_ref[0], w1, preferred_element_type=F32)[0:1]
    n_chunks = u.shape[0]
    hid = jax.nn.gelu(a + pltpu.roll(b, n_chunks - 1, 0) + pe_term, approximate=True)
    out = jnp.dot(hid.astype(BF16), w2_ref[0], preferred_element_type=F32)
    ms = jnp.mean(out * out, axis=-1, keepdims=True)
    normed = out * lax.rsqrt(ms + NORM_EPS) * kg_ref[...]
    res = jnp.where(kv == 0, normed, out)
    row = lax.broadcasted_iota(I32, res.shape, 0)
    o_ref[0, 0, 0] = jnp.where(row < n_chunks - 1, res, 0.0).astype(o_ref.dtype)


def nsa_compress(zz, cmp_pe, cmp_w1, cmp_w2, kc_gain, B, T):
    G = NSA_GROUPS
    Dh = zz.shape[-1]
    n_chunks = T // CMP_STRIDE
    u = zz.reshape(zz.shape[0], B, n_chunks, CMP_STRIDE * Dh)
    w1 = cmp_w1.reshape(2, CMP_BLOCK * Dh, Dh).astype(BF16)
    pe = jnp.broadcast_to(cmp_pe.reshape(2, 1, CMP_BLOCK * Dh), (2, 8, CMP_BLOCK * Dh)).astype(BF16)
    return pl.pallas_call(
        _nsa_compress_body,
        grid=(2, B, G),
        in_specs=[pl.BlockSpec((1, 1, n_chunks, CMP_STRIDE * Dh), lambda kv, b, g: (kv * G + g, b, 0, 0)),
                  pl.BlockSpec((1, CMP_BLOCK * Dh, Dh), lambda kv, b, g: (kv, 0, 0)),
                  pl.BlockSpec((1, Dh, Dh), lambda kv, b, g: (kv, 0, 0)),
                  pl.BlockSpec((1, 8, CMP_BLOCK * Dh), lambda kv, b, g: (kv, 0, 0)),
                  pl.BlockSpec((1, Dh), lambda kv, b, g: (0, 0))],
        out_specs=pl.BlockSpec((1, 1, 1, n_chunks, Dh), lambda kv, b, g: (kv, b, g, 0, 0)),
        out_shape=jax.ShapeDtypeStruct((2, B, G, n_chunks, Dh), BF16),
        compiler_params=_params("arbitrary", "arbitrary", "arbitrary"),
        name="nsa_compress",
    )(u, w1, cmp_w2.astype(BF16), pe, kc_gain.reshape(1, Dh).astype(F32))


def _dot_nt(a, b):
    return lax.dot_general(a, b, (((1,), (1,)), ((), ())), preferred_element_type=F32)


def _masked_softmax_rows(s, valid):
    s = jnp.where(valid, s, NEG_INF)
    m = jnp.max(s, axis=-1, keepdims=True)
    m = jnp.where(m == NEG_INF, 0.0, m)
    e = jnp.exp(s - m)
    denom = jnp.maximum(jnp.sum(e, axis=-1, keepdims=True), 1e-30)
    return e * (1.0 / denom)


def _nsa_attn_body(slopes_ref, q_ref, glog_ref, kc_ref, vc_ref, ks_ref, vs_ref, kw_ref, vw_ref,
                   msel_ref, efull_ref, o_ref, acc_ref, selk_ref):
    g = pl.program_id(1)
    qb = pl.program_id(2)
    Dh = LANES
    n_cmp_pad = kc_ref.shape[3]
    n_slc = msel_ref.shape[1]
    win_keys = WINDOW + Q_BLOCK
    s0 = qb * Q_BLOCK
    t = s0 + lax.broadcasted_iota(I32, (Q_BLOCK, 1), 0)

    gates = jax.nn.sigmoid(glog_ref[0])

    cmp_end = lax.broadcasted_iota(I32, (1, n_cmp_pad), 1) * CMP_STRIDE + (CMP_BLOCK - 1)
    cmp_valid = cmp_end <= t
    cmp_dist = (t - cmp_end).astype(F32)

    ws = pl.multiple_of(jnp.maximum(s0 - WINDOW, 0), Q_BLOCK)
    wdist = t - (ws + lax.broadcasted_iota(I32, (1, win_keys), 1))
    win_valid = (wdist >= 0) & (wdist < WINDOW)
    win_dist = wdist.astype(F32)

    kc = kc_ref[0, 0, 0]
    vc = vc_ref[0, 0, 0]
    kwin = kw_ref[0, 0, pl.ds(ws, win_keys), :]
    vwin = vw_ref[0, 0, pl.ds(ws, win_keys), :]

    imp = jnp.zeros((Q_BLOCK, n_cmp_pad), F32)
    for hg in range(NSA_HG):
        slope = slopes_ref[g, hg]
        hs = slice(hg * Dh, (hg + 1) * Dh)
        qh = q_ref[0, :, hs]
        p = _masked_softmax_rows(_dot_nt(qh, kc) - slope * cmp_dist, cmp_valid)
        imp = imp + p
        o_c = jnp.dot(p.astype(BF16), vc, preferred_element_type=F32)
        p = _masked_softmax_rows(_dot_nt(qh, kwin) - slope * win_dist, win_valid)
        o_w = jnp.dot(p.astype(BF16), vwin, preferred_element_type=F32)
        acc_ref[:, hs] = gates[:, hg:hg + 1] * o_c + gates[:, 2 * NSA_HG + hg:2 * NSA_HG + hg + 1] * o_w

    hi = imp.astype(BF16)
    r1 = imp - hi.astype(F32)
    mid = r1.astype(BF16)
    lo = (r1 - mid.astype(F32)).astype(BF16)
    msel = msel_ref[...]
    slc_imp = (jnp.dot(hi, msel, preferred_element_type=F32) + jnp.dot(mid, msel, preferred_element_type=F32)
               + jnp.dot(lo, msel, preferred_element_type=F32))

    blk = lax.broadcasted_iota(I32, (1, n_slc), 1)
    cur = lax.shift_right_logical(t, 6)
    forced = (blk == 0) | (blk == cur) | (blk == cur - 1)
    score = jnp.where(forced, jnp.inf, jnp.where(blk <= cur, slc_imp, NEG_INF))
    sel = jnp.zeros((Q_BLOCK, n_slc), F32)
    for _ in range(SLC_TOPK):
        m = jnp.max(score, axis=-1, keepdims=True)
        cand = (score == m) & (m > NEG_INF)
        first = jnp.min(jnp.where(cand, blk, n_slc), axis=-1, keepdims=True)
        one = blk == first
        sel = jnp.where(one, 1.0, sel)
        score = jnp.where(one, NEG_INF, score)
    sel_b = sel.astype(BF16)
    n_key_chunks = selk_ref.shape[0]
    for c in range(n_key_chunks):
        selk_ref[c] = jnp.dot(sel_b, efull_ref[:, c * SLC_KEY_CHUNK:(c + 1) * SLC_KEY_CHUNK],
                              preferred_element_type=F32)

    n_chunks = lax.shift_right_logical(s0 + Q_BLOCK + SLC_KEY_CHUNK - 1, 9)
    lane_pos = lax.broadcasted_iota(I32, (1, SLC_KEY_CHUNK), 1)
    for hg in range(NSA_HG):
        slope = slopes_ref[g, hg]
        hs = slice(hg * Dh, (hg + 1) * Dh)
        qh = q_ref[0, :, hs]

        def chunk_step(c, carry, qh=qh, slope=slope):
            m, l, acc = carry
            k0 = pl.multiple_of(c * SLC_KEY_CHUNK, SLC_KEY_CHUNK)
            kch = ks_ref[0, 0, pl.ds(k0, SLC_KEY_CHUNK), :]
            vch = vs_ref[0, 0, pl.ds(k0, SLC_KEY_CHUNK), :]
            dist = t - (k0 + lane_pos)
            valid = (selk_ref[c] > 0.5) & (dist >= 0)
            s = jnp.where(valid, _dot_nt(qh, kch) - slope * dist.astype(F32), NEG_INF)
            m_new = jnp.maximum(m, jnp.max(s, axis=-1, keepdims=True))
            m_safe = jnp.where(m_new == NEG_INF, 0.0, m_new)
            alpha = jnp.exp(m - m_safe)
            p = jnp.exp(s - m_safe)
            l = alpha * l + jnp.sum(p, axis=-1, keepdims=True)
            acc = alpha * acc + jnp.dot(p.astype(BF16), vch, preferred_element_type=F32)
            return m_new, l, acc

        init = (jnp.full((Q_BLOCK, 1), NEG_INF, F32), jnp.zeros((Q_BLOCK, 1), F32), jnp.zeros((Q_BLOCK, Dh), F32))
        _, l, acc = lax.fori_loop(0, n_chunks, chunk_step, init)
        o_s = acc * (1.0 / jnp.maximum(l, 1e-30))
        o_ref[0, :, hs] = (acc_ref[:, hs] + gates[:, NSA_HG + hg:NSA_HG + hg + 1] * o_s).astype(o_ref.dtype)


def _slc_constants(T):
    n_cmp_pad = T // CMP_STRIDE
    n_slc = T // SLC_BLOCK
    per = SLC_BLOCK // CMP_STRIDE
    n = np.arange(n_cmp_pad)[:, None]
    j = np.arange(n_slc)[None, :]
    msel = ((n // per == j).astype(np.float32) + ((n + 1) // per == j).astype(np.float32))
    msel[n_cmp_pad - 1] = 0.0
    efull = (np.arange(T)[None, :] // SLC_BLOCK == np.arange(n_slc)[:, None]).astype(np.float32)
    return jnp.asarray(msel, BF16), jnp.asarray(efull, BF16)


def nsa_attention(q, glog, kvc, zz, B, T):
    G, HG, Dh = NSA_GROUPS, NSA_HG, LANES
    H = NSA_HEADS
    slopes = jnp.asarray(2.0 ** (-8.0 * np.arange(1, H + 1, dtype=np.float32) / H), dtype=F32).reshape(G, HG)
    msel, efull = _slc_constants(T)
    n_cmp_pad = T // CMP_STRIDE
    n_slc = T // SLC_BLOCK

    def kv_spec(branch):
        return pl.BlockSpec((1, 1, T, Dh), lambda b, g, i: (branch * G + g, b, 0, 0))

    return pl.pallas_call(
        _nsa_attn_body,
        grid=(B, G, T // Q_BLOCK),
        in_specs=[pl.BlockSpec(memory_space=pltpu.SMEM),
                  pl.BlockSpec((1, Q_BLOCK, HG * Dh), lambda b, g, i: (b, i, g)),
                  pl.BlockSpec((1, Q_BLOCK, LANES), lambda b, g, i: (b, i, g)),
                  pl.BlockSpec((1, 1, 1, n_cmp_pad, Dh), lambda b, g, i: (0, b, g, 0, 0)),
                  pl.BlockSpec((1, 1, 1, n_cmp_pad, Dh), lambda b, g, i: (1, b, g, 0, 0)),
                  kv_spec(2), kv_spec(3), kv_spec(4), kv_spec(5),
                  pl.BlockSpec((n_cmp_pad, n_slc), lambda b, g, i: (0, 0)),
                  pl.BlockSpec((n_slc, T), lambda b, g, i: (0, 0))],
        out_specs=pl.BlockSpec((1, Q_BLOCK, HG * Dh), lambda b, g, i: (b, i, g)),
        out_shape=jax.ShapeDtypeStruct((B, T, H * Dh), BF16),
        scratch_shapes=[pltpu.VMEM((Q_BLOCK, HG * Dh), F32),
                        pltpu.VMEM((T // SLC_KEY_CHUNK, Q_BLOCK, SLC_KEY_CHUNK), F32)],
        compiler_params=_params("arbitrary", "arbitrary", "arbitrary"),
        name="nsa_attention",
    )(slopes, q, glog, kvc, kvc, zz, zz, zz, zz, msel, efull)


def _pack_rows(y, o_ref):
    m, d = y.shape
    bits = lax.bitcast_convert_type(y.astype(BF16).astype(F32), U32)
    for c in range(d // (2 * LANES)):
        lo = lax.shift_right_logical(bits[:, c * LANES:(c + 1) * LANES], jnp.uint32(16))
        hi = bits[:, d // 2 + c * LANES:d // 2 + (c + 1) * LANES]
        o_ref[pl.ds(c, m, stride=d // (2 * LANES)), :] = lo | hi


def _unpack_chunk(words):
    lo = lax.bitcast_convert_type(lax.shift_left(words, jnp.uint32(16)), F32)
    hi = lax.bitcast_convert_type(words & jnp.uint32(0xFFFF0000), F32)
    return lo, hi


def _moe_router_body(h_ref, g_ref, w_ref, b_ref, xp_ref, idx_ref, wgt_ref):
    x = h_ref[...]
    ms = jnp.mean(x * x, axis=-1, keepdims=True)
    xn = x * lax.rsqrt(ms + NORM_EPS) * g_ref[...]
    _pack_rows(xn, xp_ref)
    logits = jnp.dot(xn.astype(BF16), w_ref[...], preferred_element_type=F32) + b_ref[...]
    lane = lax.broadcasted_iota(I32, logits.shape, 1)
    logits = jnp.where(lane < N_EXPERTS, logits, NEG_INF)
    idx_out = jnp.zeros(logits.shape, I32)
    val_out = jnp.full(logits.shape, NEG_INF, F32)
    for k in range(TOP_K):
        m = jnp.max(logits, axis=-1, keepdims=True)
        first = jnp.min(jnp.where(logits == m, lane, LANES), axis=-1, keepdims=True)
        idx_out = jnp.where(lane == k, first, idx_out)
        val_out = jnp.where(lane == k, m, val_out)
        logits = jnp.where(lane == first, NEG_INF, logits)
    e = jnp.exp(val_out - jnp.max(val_out, axis=-1, keepdims=True))
    idx_ref[...] = idx_out
    wgt_ref[...] = e * (1.0 / jnp.sum(e, axis=-1, keepdims=True))


def moe_router(h, gain, w_router, b_router, tm=256):
    n, d = h.shape
    tm = min(tm, n)
    pr = d // (2 * LANES)
    w = _pad_cols(w_router).astype(BF16)
    b = _pad_cols(b_router.reshape(1, -1)).astype(F32)
    return pl.pallas_call(
        _moe_router_body,
        grid=(n // tm,),
        in_specs=[pl.BlockSpec((tm, d), lambda i: (i, 0)),
                  pl.BlockSpec((1, d), lambda i: (0, 0)),
                  pl.BlockSpec((d, LANES), lambda i: (0, 0)),
                  pl.BlockSpec((1, LANES), lambda i: (0, 0))],
        out_specs=[pl.BlockSpec((tm * pr, LANES), lambda i: (i, 0)),
                   pl.BlockSpec((tm, LANES), lambda i: (i, 0)),
                   pl.BlockSpec((tm, LANES), lambda i: (i, 0))],
        out_shape=[jax.ShapeDtypeStruct((n * pr, LANES), U32),
                   jax.ShapeDtypeStruct((n, LANES), I32), jax.ShapeDtypeStruct((n, LANES), F32)],
        compiler_params=_params("arbitrary"),
        name="moe_router",
    )(h, gain.reshape(1, d).astype(F32), w, b)


def _row_gather_copy(src_hbm, src_row, dst_buf, slot, r, sem, pr):
    return pltpu.make_async_copy(src_hbm.at[pl.ds(pl.multiple_of(src_row, pr), pr)],
                                 dst_buf.at[slot, pl.ds(pl.multiple_of(r * pr, pr), pr)], sem.at[slot])


def _start_row_gather(ids_ref, src_hbm, dst_buf, slot, sem, n_rows, pr):
    def body(r, carry):
        _row_gather_copy(src_hbm, ids_ref[0, 0, r], dst_buf, slot, r, sem, pr).start()
        return carry
    lax.fori_loop(0, n_rows, body, 0, unroll=8)


def _wait_row_gather(src_hbm, dst_buf, slot, sem):
    pltpu.make_async_copy(src_hbm.at[pl.ds(0, dst_buf.shape[1])], dst_buf.at[slot], sem.at[slot]).wait()


def _moe_expert_body(blk_e_ref, n_used_ref, ids0_ref, idsn_ref, x_hbm, wgu_ref, bgu_ref, wd_ref, bd_ref,
                     y_ref, xbuf, xs, sem):
    i = pl.program_id(0)
    n_used = n_used_ref[0]
    rows, d = xs.shape
    pr = d // (2 * LANES)
    ff = wd_ref.shape[1]

    @pl.when(i == 0)
    def _():
        _start_row_gather(ids0_ref, x_hbm, xbuf, 0, sem, rows, pr)

    @pl.when(i + 1 < n_used)
    def _():
        _start_row_gather(idsn_ref, x_hbm, xbuf, (i + 1) % 2, sem, rows, pr)

    @pl.when(i < n_used)
    def _():
        slot = i % 2
        _wait_row_gather(x_hbm, xbuf, slot, sem)
        for c in range(pr):
            lo, hi = _unpack_chunk(xbuf[slot, pl.ds(c, rows, stride=pr), :])
            xs[:, c * LANES:(c + 1) * LANES] = lo.astype(BF16)
            xs[:, d // 2 + c * LANES:d // 2 + (c + 1) * LANES] = hi.astype(BF16)
        gu = jnp.dot(xs[...], wgu_ref[0], preferred_element_type=F32) + bgu_ref[0]
        gate = jnp.minimum(gu[:, :ff], SWIGLU_LIMIT)
        up = jnp.clip(gu[:, ff:], -SWIGLU_LIMIT, SWIGLU_LIMIT)
        act = (up + 1.0) * (gate * jax.nn.sigmoid(SWIGLU_ALPHA * gate))
        y = jnp.dot(act.astype(BF16), wd_ref[0], preferred_element_type=F32) + bd_ref[0]
        _pack_rows(y, y_ref)

    @pl.when(i >= n_used)
    def _():
        y_ref[...] = jnp.zeros(y_ref.shape, y_ref.dtype)


def moe_experts(xp, tok_rows, blk_e, n_used, w_gate_up, b_gate_up, w_down, b_down):
    E, d, ff2 = w_gate_up.shape
    ff = ff2 // 2
    pr = d // (2 * LANES)
    rows = MOE_BLOCK_ROWS
    n_blocks = tok_rows.shape[0] // rows
    ids = tok_rows.reshape(n_blocks, 1, rows)

    def used(i, nu):
        return jnp.minimum(i, nu[0] - 1)

    grid_spec = pltpu.PrefetchScalarGridSpec(
        num_scalar_prefetch=2,
        grid=(n_blocks,),
        in_specs=[pl.BlockSpec((1, 1, rows), lambda i, be, nu: (0, 0, 0), memory_space=pltpu.SMEM),
                  pl.BlockSpec((1, 1, rows), lambda i, be, nu: (jnp.minimum(i + 1, n_blocks - 1), 0, 0),
                               memory_space=pltpu.SMEM),
                  pl.BlockSpec(memory_space=pl.ANY),
                  pl.BlockSpec((1, d, ff2), lambda i, be, nu: (be[used(i, nu)], 0, 0)),
                  pl.BlockSpec((1, 1, ff2), lambda i, be, nu: (be[used(i, nu)], 0, 0)),
                  pl.BlockSpec((1, ff, d), lambda i, be, nu: (be[used(i, nu)], 0, 0)),
                  pl.BlockSpec((1, 1, d), lambda i, be, nu: (be[used(i, nu)], 0, 0))],
        out_specs=pl.BlockSpec((rows * pr, LANES), lambda i, be, nu: (i, 0)),
        scratch_shapes=[pltpu.VMEM((2, rows * pr, LANES), U32), pltpu.VMEM((rows, d), BF16),
                        pltpu.SemaphoreType.DMA((2,))],
    )
    return pl.pallas_call(
        _moe_expert_body,
        grid_spec=grid_spec,
        out_shape=jax.ShapeDtypeStruct((n_blocks * rows * pr, LANES), U32),
        compiler_params=_params("arbitrary"),
        name="moe_experts",
    )(blk_e, n_used, ids, ids, xp, w_gate_up.astype(BF16), b_gate_up.reshape(E, 1, ff2).astype(F32),
      w_down.astype(BF16), b_down.reshape(E, 1, d).astype(F32))


def _moe_combine_body(ids0_ref, idsn_ref, y_hbm, h_ref, w_ref, o_ref, ybuf, sem):
    i = pl.program_id(0)
    n_steps = pl.num_programs(0)
    tt, d = h_ref.shape
    pr = d // (2 * LANES)
    rows = TOP_K * tt

    @pl.when(i == 0)
    def _():
        _start_row_gather(ids0_ref, y_hbm, ybuf, 0, sem, rows, pr)

    @pl.when(i + 1 < n_steps)
    def _():
        _start_row_gather(idsn_ref, y_hbm, ybuf, (i + 1) % 2, sem, rows, pr)

    slot = i % 2
    _wait_row_gather(y_hbm, ybuf, slot, sem)
    w = w_ref[...]
    wk = [jnp.broadcast_to(w[:, k:k + 1], (tt, LANES)) for k in range(TOP_K)]
    for c in range(pr):
        lo_cols = slice(c * LANES, (c + 1) * LANES)
        hi_cols = slice(d // 2 + c * LANES, d // 2 + (c + 1) * LANES)
        acc_lo = h_ref[:, lo_cols]
        acc_hi = h_ref[:, hi_cols]
        for k in range(TOP_K):
            lo, hi = _unpack_chunk(ybuf[slot, pl.ds(k * tt * pr + c, tt, stride=pr), :])
            acc_lo = acc_lo + wk[k] * lo
            acc_hi = acc_hi + wk[k] * hi
        o_ref[:, lo_cols] = acc_lo
        o_ref[:, hi_cols] = acc_hi


def moe_combine(yp, slot_rows, weights, h):
    n, d = h.shape
    pr = d // (2 * LANES)
    tt = min(MOE_COMBINE_TOKENS, n)
    n_steps = n // tt
    ids = slot_rows.reshape(n_steps, tt, TOP_K).transpose(0, 2, 1).reshape(n_steps, 1, TOP_K * tt)
    return pl.pallas_call(
        _moe_combine_body,
        grid=(n_steps,),
        in_specs=[pl.BlockSpec((1, 1, TOP_K * tt), lambda i: (0, 0, 0), memory_space=pltpu.SMEM),
                  pl.BlockSpec((1, 1, TOP_K * tt), lambda i: (jnp.minimum(i + 1, n_steps - 1), 0, 0),
                               memory_space=pltpu.SMEM),
                  pl.BlockSpec(memory_space=pl.ANY),
                  pl.BlockSpec((tt, d), lambda i: (i, 0)),
                  pl.BlockSpec((tt, LANES), lambda i: (i, 0))],
        out_specs=pl.BlockSpec((tt, d), lambda i: (i, 0)),
        out_shape=jax.ShapeDtypeStruct((n, d), F32),
        scratch_shapes=[pltpu.VMEM((2, TOP_K * tt * pr, LANES), U32), pltpu.SemaphoreType.DMA((2,))],
        compiler_params=_params("arbitrary"),
        name="moe_combine",
    )(ids, ids, yp, h, weights)


def _moe_plan(top_idx, n_rows):
    E, rows = N_EXPERTS, MOE_BLOCK_ROWS
    flat_e = top_idx.reshape(-1)
    nk = flat_e.shape[0]
    onehot = (flat_e[:, None] == jnp.arange(E, dtype=I32)[None, :]).astype(I32)
    csum = jnp.cumsum(onehot, axis=0)
    rank = jnp.sum(onehot * csum, axis=1) - 1
    sizes = csum[-1]
    nblk = (sizes + rows - 1) // rows
    blk_end = jnp.cumsum(nblk)
    pad_start = (blk_end - nblk) * rows
    slot_dest = pad_start[flat_e] + rank
    tok_buf = jnp.zeros((n_rows,), I32).at[slot_dest].set(jnp.arange(nk, dtype=I32) // TOP_K)
    n_blocks = n_rows // rows
    blk_e = jnp.minimum(jnp.searchsorted(blk_end, jnp.arange(n_blocks, dtype=I32), side='right'), E - 1).astype(I32)
    n_used = blk_end[-1:].astype(I32)
    return slot_dest, tok_buf, blk_e, n_used


def moe_layer(h, gain, w_router, b_router, w_gate_up, b_gate_up, w_down, b_down):
    n, d = h.shape
    pr = d // (2 * LANES)
    rows = MOE_BLOCK_ROWS
    xp, top_idx, weights = moe_router(h, gain, w_router, b_router)
    n_rows = n * TOP_K + N_EXPERTS * rows
    slot_dest, tok_buf, blk_e, n_used = _moe_plan(top_idx[:, :TOP_K], n_rows)
    yp = moe_experts(xp, tok_buf * pr, blk_e, n_used, w_gate_up, b_gate_up, w_down, b_down)
    return moe_combine(yp, slot_dest.reshape(n, TOP_K) * pr, weights, h)


def _rms(x, gain):
    xf = x.astype(F32)
    y = xf * lax.rsqrt(jnp.mean(xf * xf, axis=-1, keepdims=True) + NORM_EPS)
    return y * gain.astype(F32)


def _gla_mixer(h, xn, w_in, w_gate2, b_gate2, out_gain, w_out, B, T):
    D = h.shape[-1]
    H, C = GLA_HEADS, GLA_CHUNK
    dk, dv = D // (2 * H), D // H
    n_main = 2 * H * dk + 2 * H * dv
    proj = matmul(xn, w_in[:, :n_main].astype(BF16)).reshape(B, T, n_main)
    a = matmul(xn, _pad_cols(w_in[:, n_main:]).astype(BF16))[:, :GLA_GATE_RANK].reshape(B, T, -1)
    q, k, v, g = jnp.split(proj, [H * dk, 2 * H * dk, 2 * H * dk + H * dv], axis=-1)
    q = q.reshape(B, T, H, dk) * dk ** -0.5
    k = k.reshape(B, T, H, dk)
    v = v.reshape(B, T, H, dv)
    log_a = jax.nn.log_sigmoid(a @ w_gate2 + b_gate2).reshape(B, T, H, dk) / GLA_GATE_TAU
    nc = T // C

    def to_chunks(z):
        return z.reshape(B, nc, C, H, -1).transpose(1, 0, 3, 2, 4)

    qc, kc, vc = to_chunks(q), to_chunks(k), to_chunks(v)
    bc = jnp.cumsum(to_chunks(log_a), axis=3)
    causal = jnp.tril(jnp.ones((C, C), dtype=bool))

    def step(S, xs):
        qi, ki, vi, bi = xs
        b_last = bi[:, :, -1:, :]
        qe = qi * jnp.exp(bi)
        ke = ki * jnp.exp(-bi)
        att = jnp.where(causal, jnp.einsum('bhtd,bhsd->bhts', qe, ke), 0.0)
        o = jnp.einsum('bhts,bhsv->bhtv', att, vi) + jnp.einsum('bhtd,bhdv->bhtv', qe, S)
        S = S * jnp.exp(b_last)[:, :, 0, :, None] + jnp.einsum('bhsd,bhsv->bhdv', ki * jnp.exp(b_last - bi), vi)
        return S, o

    S0 = jnp.zeros((B, H, dk, dv), F32)
    _, o = lax.scan(step, S0, (qc, kc, vc, bc))
    o = o.transpose(1, 0, 3, 2, 4).reshape(B, T, H, dv)
    o = _rms(o, out_gain) * jax.nn.silu(g).reshape(B, T, H, dv)
    o = o.reshape(B * T, H * dv).astype(BF16)
    return matmul(o, w_out.astype(BF16), residual=h)


def _nsa_shared_kv(h, kv_gain, w_kv, k_gain, cmp_pe, cmp_w1, cmp_w2, B, T):
    G = NSA_GROUPS
    Dh = LANES
    xn = rmsnorm(h, kv_gain)
    ones = jnp.ones((G * Dh,), F32)
    gain_cols = jnp.concatenate([ones, ones, jnp.tile(k_gain[1], G), ones, jnp.tile(k_gain[2], G), ones])
    norm_groups = [False] * (2 * G) + [True] * G + [False] * G + [True] * G + [False] * G
    zz = matmul_groupnorm(xn, w_kv.astype(BF16), gain_cols, norm_groups, split_out=True, tn=6 * G * Dh)
    kvc = nsa_compress(zz, cmp_pe, cmp_w1, cmp_w2, k_gain[0], B, T)
    return kvc, zz.reshape(6 * G, B, T, Dh)


def _nsa_mixer(h, xn, w_in, q_gain, w_out, kvc, zz, B, T):
    H, G, HG, Dh = NSA_HEADS, NSA_GROUPS, NSA_HG, LANES
    q = matmul_groupnorm(xn, w_in[:, :H * Dh].astype(BF16), jnp.tile(q_gain, H), [True] * 4,
                         scale=Dh ** -0.5, tm=1024, tn=512)
    wg = w_in[:, H * Dh:].reshape(-1, G, HG, 3).transpose(0, 1, 3, 2).reshape(-1, G, 3 * HG)
    wg = jnp.pad(wg, ((0, 0), (0, 0), (0, LANES - 3 * HG))).reshape(-1, G * LANES)
    glog = matmul(xn, wg.astype(BF16))
    o = nsa_attention(q.reshape(B, T, H * Dh), glog.reshape(B, T, G * LANES), kvc, zz, B, T)
    return matmul(o.reshape(B * T, H * Dh), w_out.astype(BF16), residual=h)


def kernel(x, ln_mix, ln_ffn, a_w_in, a_w_gate2, a_b_gate2, a_out_gain, a_w_out, kv_gain, w_kv, k_gain,
           cmp_pe, cmp_w1, cmp_w2, b_w_in, b_q_gain, b_w_out, w_router, b_router, w_gate_up, b_gate_up,
           w_down, b_down):
    B, T, D = x.shape
    depth = ln_mix.shape[0]
    n_a = a_w_in.shape[0]
    h = x.reshape(B * T, D)
    shared = None
    for l in range(depth):
        xn = rmsnorm(h, ln_mix[l])
        if l < n_a:
            h = _gla_mixer(h, xn, a_w_in[l], a_w_gate2[l], a_b_gate2[l], a_out_gain[l], a_w_out[l], B, T)
        else:
            if l == n_a:
                shared = _nsa_shared_kv(h, kv_gain, w_kv, k_gain, cmp_pe, cmp_w1, cmp_w2, B, T)
            j = l - n_a
            h = _nsa_mixer(h, xn, b_w_in[j], b_q_gain[j], b_w_out[j], *shared, B, T)
        h = moe_layer(h, ln_ffn[l], w_router[l], b_router[l], w_gate_up[l], b_gate_up[l], w_down[l], b_down[l])
    return h.reshape(B, T, D)
```

```python
import functools
import math

import jax
import jax.numpy as jnp
from jax import lax
import numpy as np
from jax.experimental import pallas as pl
from jax.experimental.pallas import tpu as pltpu

F32 = jnp.float32
BF16 = jnp.bfloat16
I32 = jnp.int32
U32 = jnp.uint32

NORM_EPS = 1e-5
GLA_HEADS = 8
GLA_GATE_RANK = 16
GLA_GATE_TAU = 16.0
NSA_HEADS = 32
NSA_GROUPS = 2
NSA_HG = NSA_HEADS // NSA_GROUPS
CMP_BLOCK = 32
CMP_STRIDE = 16
SLC_BLOCK = 64
SLC_TOPK = 8
WINDOW = 512
Q_BLOCK = 128
N_EXPERTS = 32
TOP_K = 4
SWIGLU_ALPHA = 1.702
SWIGLU_LIMIT = 7.0

LANES = 128
MXU_DIM = 256
VMEM_LIMIT_BYTES = 56 * 1024 * 1024
MOE_BLOCK_ROWS = 512
MOE_COMBINE_TOKENS = 128
SLC_KEY_CHUNK = 512
NSA_ROW_BLOCK = 256
GLA_BLOCK = 128
GLA_STEP_TOKENS = 512
NEG_INF = float("-inf")
LOG2E = math.log2(math.e)


def _params(*sem):
    return pltpu.CompilerParams(dimension_semantics=sem, vmem_limit_bytes=VMEM_LIMIT_BYTES)


def _split3(x):
    hi = x.astype(BF16)
    r1 = x - hi.astype(F32)
    mid = r1.astype(BF16)
    lo = (r1 - mid.astype(F32)).astype(BF16)
    return hi, mid, lo


def _rmsnorm_body(x_ref, g_ref, o_ref):
    x = x_ref[...].astype(F32)
    ms = jnp.mean(x * x, axis=-1, keepdims=True)
    o_ref[...] = (x * lax.rsqrt(ms + NORM_EPS) * g_ref[...].astype(F32)).astype(o_ref.dtype)


def rmsnorm(x, gain, out_dtype=BF16, tm=512):
    m, d = x.shape
    tm = min(tm, m)
    return pl.pallas_call(
        _rmsnorm_body,
        grid=(m // tm,),
        in_specs=[pl.BlockSpec((tm, d), lambda i: (i, 0)),
                  pl.BlockSpec((1, d), lambda i: (0, 0))],
        out_specs=pl.BlockSpec((tm, d), lambda i: (i, 0)),
        out_shape=jax.ShapeDtypeStruct((m, d), out_dtype),
        compiler_params=_params("arbitrary"),
        name="rmsnorm",
    )(x, gain.reshape(1, d))


def _matmul_body(a_ref, w_ref, o_ref):
    o_ref[...] = jnp.dot(a_ref[...], w_ref[...], preferred_element_type=F32).astype(o_ref.dtype)


def _matmul_res_body(a_ref, w_ref, r_ref, o_ref):
    acc = jnp.dot(a_ref[...], w_ref[...], preferred_element_type=F32)
    o_ref[...] = (r_ref[...].astype(F32) + acc).astype(o_ref.dtype)


def matmul(a, w, residual=None, out_dtype=F32, tm=1024, tn=512):
    m, k = a.shape
    _, n = w.shape
    tm = min(tm, m)
    tn = min(tn, n)
    assert m % tm == 0 and n % tn == 0, (m, n, tm, tn)
    in_specs = [pl.BlockSpec((tm, k), lambda i, j: (i, 0)),
                pl.BlockSpec((k, tn), lambda i, j: (0, j))]
    args = [a, w]
    body = _matmul_body
    if residual is not None:
        in_specs.append(pl.BlockSpec((tm, tn), lambda i, j: (i, j)))
        args.append(residual)
        body = _matmul_res_body
    return pl.pallas_call(
        body,
        grid=(m // tm, n // tn),
        in_specs=in_specs,
        out_specs=pl.BlockSpec((tm, tn), lambda i, j: (i, j)),
        out_shape=jax.ShapeDtypeStruct((m, n), out_dtype),
        compiler_params=_params("arbitrary", "arbitrary"),
        name="matmul",
    )(*args)


def _matmul_groupnorm_body(a_ref, w_ref, g_ref, o_ref, *, norm_groups, scale, split_out):
    acc = jnp.dot(a_ref[...], w_ref[...], preferred_element_type=F32)
    for c, do_norm in enumerate(norm_groups):
        seg = acc[:, c * LANES:(c + 1) * LANES]
        if do_norm:
            ms = jnp.mean(seg * seg, axis=-1, keepdims=True)
            seg = seg * lax.rsqrt(ms + NORM_EPS) * g_ref[:, c * LANES:(c + 1) * LANES] * scale
        if split_out:
            o_ref[c] = seg.astype(o_ref.dtype)
        else:
            o_ref[:, c * LANES:(c + 1) * LANES] = seg.astype(o_ref.dtype)


def matmul_groupnorm(a, w, gain_cols, norm_groups, scale=1.0, split_out=False, out_dtype=BF16, tm=512, tn=512):
    m, k = a.shape
    _, n = w.shape
    tm = min(tm, m)
    tn = min(tn, n)
    assert m % tm == 0 and n % tn == 0 and len(norm_groups) == tn // LANES
    if split_out:
        out_shape = jax.ShapeDtypeStruct((n // LANES, m, LANES), out_dtype)
        out_spec = pl.BlockSpec((tn // LANES, tm, LANES), lambda i, j: (j, i, 0))
    else:
        out_shape = jax.ShapeDtypeStruct((m, n), out_dtype)
        out_spec = pl.BlockSpec((tm, tn), lambda i, j: (i, j))
    body = functools.partial(_matmul_groupnorm_body, norm_groups=tuple(norm_groups), scale=scale,
                             split_out=split_out)
    return pl.pallas_call(
        body,
        grid=(m // tm, n // tn),
        in_specs=[pl.BlockSpec((tm, k), lambda i, j: (i, 0)),
                  pl.BlockSpec((k, tn), lambda i, j: (0, j)),
                  pl.BlockSpec((1, tn), lambda i, j: (0, j))],
        out_specs=out_spec,
        out_shape=out_shape,
        compiler_params=_params("arbitrary", "arbitrary"),
        name="matmul_groupnorm",
    )(a, w, gain_cols.reshape(1, n).astype(F32))


def _pad_cols(w, mult=LANES):
    pad = (-w.shape[-1]) % mult
    return jnp.pad(w, ((0, 0), (0, pad))) if pad else w


def _gla_body(q_ref, k_ref, v_ref, g_ref, a_ref, wg_ref, bg_ref, og_ref, tri_ref, o_ref, s_ref):
    dk = q_ref.shape[1]
    C = GLA_BLOCK

    @pl.when(pl.program_id(2) == 0)
    def _():
        s_ref[...] = jnp.zeros(s_ref.shape, F32)

    tri = tri_ref[...]
    row = lax.broadcasted_iota(I32, (C, C), 0)
    col = lax.broadcasted_iota(I32, (C, C), 1)
    for c in range(q_ref.shape[0] // C):
        rows = slice(c * C, (c + 1) * C)
        gate_in = jnp.dot(a_ref[rows, :].astype(BF16), wg_ref[...], preferred_element_type=F32) + bg_ref[...]
        log_a = jax.nn.log_sigmoid(gate_in) * (1.0 / GLA_GATE_TAU)
        hi, mid, lo = _split3(log_a)
        bcum = (jnp.dot(tri, hi, preferred_element_type=F32) + jnp.dot(tri, mid, preferred_element_type=F32)
                + jnp.dot(tri, lo, preferred_element_type=F32))
        b_mid = bcum[C // 2 - 1:C // 2, :]
        q = q_ref[rows, :].astype(F32) * dk ** -0.5
        k = k_ref[rows, :].astype(F32)
        v = v_ref[rows, :]
        att = lax.dot_general((q * jnp.exp(bcum - b_mid)).astype(BF16), (k * jnp.exp(b_mid - bcum)).astype(BF16),
                              (((1,), (1,)), ((), ())), preferred_element_type=F32)
        att = jnp.where(col <= row, att, 0.0)
        o = jnp.dot(att.astype(BF16), v, preferred_element_type=F32)
        o = o + jnp.dot((q * jnp.exp(bcum)).astype(BF16), s_ref[...].astype(BF16), preferred_element_type=F32)
        bcum_t = bcum.T
        b_last = bcum_t[:, C - 1:C]
        k_t = (k.T * jnp.exp(b_last - bcum_t)).astype(BF16)
        s_ref[...] = s_ref[...] * jnp.exp(b_last) + jnp.dot(k_t, v, preferred_element_type=F32)
        ms = jnp.mean(o * o, axis=-1, keepdims=True)
        o = o * lax.rsqrt(ms + NORM_EPS) * og_ref[...]
        o_ref[rows, :] = (o * jax.nn.silu(g_ref[rows, :].astype(F32))).astype(o_ref.dtype)


def gla_core(proj, a_pad, w_gate2, b_gate2, out_gain, B, T):
    H = GLA_HEADS
    n = proj.shape[0]
    dv = out_gain.shape[0]
    dk = w_gate2.shape[1] // H
    tb = min(GLA_STEP_TOKENS, T)
    nt = T // tb
    wg = jnp.pad(w_gate2, ((0, LANES - w_gate2.shape[0]), (0, 0))).astype(BF16)
    tri = jnp.asarray(np.tril(np.ones((GLA_BLOCK, GLA_BLOCK), np.float32)), BF16)
    v0 = 2 * H * dk // dv
    return pl.pallas_call(
        _gla_body,
        grid=(B, H, nt),
        in_specs=[pl.BlockSpec((tb, dk), lambda b, h, i: (b * nt + i, h)),
                  pl.BlockSpec((tb, dk), lambda b, h, i: (b * nt + i, H + h)),
                  pl.BlockSpec((tb, dv), lambda b, h, i: (b * nt + i, v0 + h)),
                  pl.BlockSpec((tb, dv), lambda b, h, i: (b * nt + i, v0 + H + h)),
                  pl.BlockSpec((tb, LANES), lambda b, h, i: (b * nt + i, 0)),
                  pl.BlockSpec((LANES, dk), lambda b, h, i: (0, h)),
                  pl.BlockSpec((1, dk), lambda b, h, i: (0, h)),
                  pl.BlockSpec((1, dv), lambda b, h, i: (0, 0)),
                  pl.BlockSpec((GLA_BLOCK, GLA_BLOCK), lambda b, h, i: (0, 0))],
        out_specs=pl.BlockSpec((tb, dv), lambda b, h, i: (b * nt + i, h)),
        out_shape=jax.ShapeDtypeStruct((n, H * dv), BF16),
        scratch_shapes=[pltpu.VMEM((dk, dv), F32)],
        compiler_params=_params("arbitrary", "arbitrary", "arbitrary"),
        name="gla_core",
    )(proj, proj, proj, proj, a_pad, wg, b_gate2.reshape(1, -1).astype(F32), out_gain.reshape(1, dv).astype(F32), tri)


def _nsa_compress_body(u_ref, w1_ref, w2_ref, pe_ref, kg_ref, o_ref):
    kv = pl.program_id(0)
    u = u_ref[0, 0]
    half = u.shape[1]
    w1 = w1_ref[0]
    a = jnp.dot(u, w1[:half], preferred_element_type=F32)
    b = jnp.dot(u, w1[half:], preferred_element_type=F32)
    pe_term = jnp.dot(pe_ref[0], w1, preferred_element_type=F32)[0:1]
    n_chunks = u.shape[0]
    hid = jax.nn.gelu(a + pltpu.roll(b, n_chunks - 1, 0) + pe_term, approximate=True)
    out = jnp.dot(hid.astype(BF16), w2_ref[0], preferred_element_type=F32)
    ms = jnp.mean(out * out, axis=-1, keepdims=True)
    normed = out * lax.rsqrt(ms + NORM_EPS) * kg_ref[...]
    res = jnp.where(kv == 0, normed, out)
    row = lax.broadcasted_iota(I32, res.shape, 0)
    o_ref[0, 0, 0] = jnp.where(row < n_chunks - 1, res, 0.0).astype(o_ref.dtype)


def nsa_compress(zz, cmp_pe, cmp_w1, cmp_w2, kc_gain, B, T):
    G = NSA_GROUPS
    Dh = zz.shape[-1]
    n_chunks = T // CMP_STRIDE
    u = zz.reshape(zz.shape[0], B, n_chunks, CMP_STRIDE * Dh)
    w1 = cmp_w1.reshape(2, CMP_BLOCK * Dh, Dh).astype(BF16)
    pe = jnp.broadcast_to(cmp_pe.reshape(2, 1, CMP_BLOCK * Dh), (2, 8, CMP_BLOCK * Dh)).astype(BF16)
    return pl.pallas_call(
        _nsa_compress_body,
        grid=(2, B, G),
        in_specs=[pl.BlockSpec((1, 1, n_chunks, CMP_STRIDE * Dh), lambda kv, b, g: (kv * G + g, b, 0, 0)),
                  pl.BlockSpec((1, CMP_BLOCK * Dh, Dh), lambda kv, b, g: (kv, 0, 0)),
                  pl.BlockSpec((1, Dh, Dh), lambda kv, b, g: (kv, 0, 0)),
                  pl.BlockSpec((1, 8, CMP_BLOCK * Dh), lambda kv, b, g: (kv, 0, 0)),
                  pl.BlockSpec((1, Dh), lambda kv, b, g: (0, 0))],
        out_specs=pl.BlockSpec((1, 1, 1, n_chunks, Dh), lambda kv, b, g: (kv, b, g, 0, 0)),
        out_shape=jax.ShapeDtypeStruct((2, B, G, n_chunks, Dh), BF16),
        compiler_params=_params("arbitrary", "arbitrary", "arbitrary"),
        name="nsa_compress",
    )(u, w1, cmp_w2.astype(BF16), pe, kc_gain.reshape(1, Dh).astype(F32))


def _dot_nt(a, b):
    return lax.dot_general(a, b, (((1,), (1,)), ((), ())), preferred_element_type=F32)


def _col_softmax_terms(s):
    m = jnp.max(s, axis=0, keepdims=True)
    m = jnp.where(m == NEG_INF, 0.0, m)
    e = jnp.exp2(s - m)
    return e, 1.0 / jnp.maximum(jnp.sum(e, axis=0, keepdims=True), 1e-30)


def _tile_lanes(x, n):
    return jnp.concatenate([x] * n, axis=1)


def _nsa_attn_body(q_ref, qaug_ref, glog_ref, kc_ref, vct_ref, ks_ref, vst_ref, kw_ref, vwt_ref,
                   mselt_ref, efullt_ref, o_ref, q2_ref, acct_ref, m_ref, l_ref, oacct_ref, s_ref, sc_ref, sw_ref):
    qb = pl.program_id(2)
    Dh = LANES
    RB = NSA_ROW_BLOCK
    hpb = RB // Q_BLOCK
    n_rb = NSA_HG // hpb
    n_cmp_pad = kc_ref.shape[2]
    n_slc = mselt_ref.shape[0]
    win_keys = WINDOW + Q_BLOCK
    s0 = qb * Q_BLOCK
    t_q = s0 + lax.broadcasted_iota(I32, (1, Q_BLOCK), 1)

    gates_t = jax.nn.sigmoid(glog_ref[0]).T

    for hg in range(NSA_HG):
        q2_ref[hg * Q_BLOCK:(hg + 1) * Q_BLOCK, :Dh] = q_ref[0, :, hg * Dh:(hg + 1) * Dh]
    q2_ref[:, Dh:] = qaug_ref[0]

    cmp_end = lax.broadcasted_iota(I32, (n_cmp_pad, 1), 0) * CMP_STRIDE + (CMP_BLOCK - 1)
    cmp_mask = _tile_lanes(jnp.where(cmp_end <= t_q, 0.0, NEG_INF), hpb)

    ws = pl.multiple_of(jnp.maximum(s0 - WINDOW, 0), Q_BLOCK)
    wdist = t_q - (ws + lax.broadcasted_iota(I32, (win_keys, 1), 0))
    win_mask = _tile_lanes(jnp.where((wdist >= 0) & (wdist < WINDOW), 0.0, NEG_INF), hpb)

    kc = kc_ref[0, 0]
    vct = jnp.concatenate([vct_ref[0, 0, j] for j in range(n_cmp_pad // Dh)], axis=1)
    kwin = kw_ref[0, 0, pl.ds(ws, win_keys), :]
    wblk = ws // Dh
    vwt = jnp.concatenate([vwt_ref[0, 0, wblk + j] for j in range(win_keys // Dh)], axis=1)

    imp_t = jnp.zeros((n_cmp_pad, Q_BLOCK), F32)
    for rb in range(n_rb):
        q_rb = q2_ref[rb * RB:(rb + 1) * RB, :]
        sc_ref[rb] = _dot_nt(kc, q_rb) + cmp_mask
        sw_ref[rb] = _dot_nt(kwin, q_rb) + win_mask
    for rb in range(n_rb):
        e, inv = _col_softmax_terms(sc_ref[rb])
        p = e * inv
        for j in range(hpb):
            imp_t = imp_t + p[:, j * Q_BLOCK:(j + 1) * Q_BLOCK]
        o_c = jnp.dot(vct, p.astype(BF16), preferred_element_type=F32)
        e, inv = _col_softmax_terms(sw_ref[rb])
        o_w = jnp.dot(vwt, e.astype(BF16), preferred_element_type=F32) * inv
        for j in range(hpb):
            hg = rb * hpb + j
            sub = slice(j * Q_BLOCK, (j + 1) * Q_BLOCK)
            acct_ref[:, hg * Q_BLOCK:(hg + 1) * Q_BLOCK] = (
                gates_t[hg:hg + 1, :] * o_c[:, sub] + gates_t[2 * NSA_HG + hg:2 * NSA_HG + hg + 1, :] * o_w[:, sub])

    hi, mid, lo = _split3(imp_t)
    mselt = mselt_ref[...]
    slc_imp = (jnp.dot(mselt, hi, preferred_element_type=F32) + jnp.dot(mselt, mid, preferred_element_type=F32)
               + jnp.dot(mselt, lo, preferred_element_type=F32))

    blk = lax.broadcasted_iota(I32, (n_slc, 1), 0)
    cur = lax.shift_right_logical(t_q, 6)
    forced = (blk == 0) | (blk == cur) | (blk == cur - 1)
    score = jnp.where(forced, jnp.inf, jnp.where(blk <= cur, slc_imp, NEG_INF))
    sel = jnp.zeros((n_slc, Q_BLOCK), F32)
    for _ in range(SLC_TOPK):
        m = jnp.max(score, axis=0, keepdims=True)
        cand = (score == m) & (m > NEG_INF)
        first = jnp.min(jnp.where(cand, blk, n_slc), axis=0, keepdims=True)
        one = blk == first
        sel = jnp.where(one, 1.0, sel)
        score = jnp.where(one, NEG_INF, score)
    sel_b = sel.astype(BF16)

    m_ref[...] = jnp.full(m_ref.shape, NEG_INF, F32)
    l_ref[...] = jnp.zeros(l_ref.shape, F32)
    oacct_ref[...] = jnp.zeros(oacct_ref.shape, F32)
    KC = SLC_KEY_CHUNK
    key_iota = lax.broadcasted_iota(I32, (KC, 1), 0)

    def chunk_step(c, carry):
        k0 = pl.multiple_of(c * KC, KC)
        kch = ks_ref[0, 0, pl.ds(k0, KC), :]
        vt = jnp.concatenate([vst_ref[0, 0, c * (KC // Dh) + j] for j in range(KC // Dh)], axis=1)
        selk = jnp.dot(efullt_ref[pl.ds(k0, KC), :], sel_b, preferred_element_type=F32)
        mask = _tile_lanes(jnp.where((selk > 0.5) & (k0 + key_iota <= t_q), 0.0, NEG_INF), hpb)
        for rb in range(n_rb):
            s_ref[rb] = _dot_nt(kch, q2_ref[rb * RB:(rb + 1) * RB, :]) + mask
        for rb in range(n_rb):
            cols = slice(rb * RB, (rb + 1) * RB)
            s = s_ref[rb]
            m_old = m_ref[rb]
            m_new = jnp.maximum(m_old, jnp.max(s, axis=0, keepdims=True))
            m_safe = jnp.where(m_new == NEG_INF, 0.0, m_new)
            alpha = jnp.exp2(m_old - m_safe)
            p = jnp.exp2(s - m_safe)
            l_ref[rb] = alpha * l_ref[rb] + jnp.sum(p, axis=0, keepdims=True)
            oacct_ref[:, cols] = alpha * oacct_ref[:, cols] + jnp.dot(vt, p.astype(BF16), preferred_element_type=F32)
            m_ref[rb] = m_new
        return carry

    n_chunks = (s0 + Q_BLOCK + KC - 1) // KC
    lax.fori_loop(0, n_chunks, chunk_step, 0)
    for hg in range(NSA_HG):
        rb, j = divmod(hg, hpb)
        cols = slice(hg * Q_BLOCK, (hg + 1) * Q_BLOCK)
        inv = 1.0 / jnp.maximum(l_ref[rb][:, j * Q_BLOCK:(j + 1) * Q_BLOCK], 1e-30)
        o_t = acct_ref[:, cols] + gates_t[NSA_HG + hg:NSA_HG + hg + 1, :] * (oacct_ref[:, cols] * inv)
        o_ref[0, :, hg * Dh:(hg + 1) * Dh] = o_t.T.astype(o_ref.dtype)


def _pos_pieces(pos):
    pos = np.asarray(pos)
    out = np.zeros((pos.shape[0], LANES), np.float32)
    for i in range(3):
        out[:, 2 * i] = 64 * (pos // 64)
        out[:, 2 * i + 1] = pos % 64
    return jnp.asarray(out, BF16)


def _slope_pieces():
    H, G, HG = NSA_HEADS, NSA_GROUPS, NSA_HG
    slopes = jnp.asarray(LOG2E * 2.0 ** (-8.0 * np.arange(1, H + 1, dtype=np.float64) / H), F32)
    pieces = jnp.stack(_split3(slopes), axis=-1)
    cols = jnp.repeat(pieces, 2, axis=-1)
    cols = jnp.pad(cols, ((0, 0), (0, LANES - cols.shape[-1])))
    return jnp.repeat(cols.reshape(G, HG, 1, LANES), Q_BLOCK, axis=2).reshape(G, HG * Q_BLOCK, LANES)


def _blocked_transpose(v):
    lead, (t, dh) = v.shape[:-2], v.shape[-2:]
    return jnp.swapaxes(v.reshape(*lead, t // LANES, LANES, dh), -1, -2)


def nsa_attention(q, glog, kvc, zz, B, T):
    G, HG, Dh = NSA_GROUPS, NSA_HG, LANES
    H = NSA_HEADS
    n_cmp = T // CMP_STRIDE - CMP_BLOCK // CMP_STRIDE + 1
    n_cmp_pad = -(-T // CMP_STRIDE // LANES) * LANES
    n_slc = T // SLC_BLOCK
    per = SLC_BLOCK // CMP_STRIDE
    n = np.arange(n_cmp_pad)[None, :]
    j = np.arange(n_slc)[:, None]
    mselt = ((n // per == j).astype(np.float32) + ((n + 1) // per == j).astype(np.float32))
    mselt[:, n_cmp:] = 0.0
    kvc = jnp.pad(kvc, ((0, 0), (0, 0), (0, 0), (0, n_cmp_pad - kvc.shape[3]), (0, 0)))
    efullt = (np.arange(T)[:, None] // SLC_BLOCK == np.arange(n_slc)[None, :]).astype(np.float32)

    key_aug = jnp.broadcast_to(_pos_pieces(np.arange(T)), (G, B, T, LANES))
    cmp_aug = jnp.broadcast_to(_pos_pieces(np.arange(n_cmp_pad) * CMP_STRIDE + CMP_BLOCK - 1),
                               (B, G, n_cmp_pad, LANES))
    kc_aug = jnp.concatenate([kvc[0], cmp_aug], axis=-1)
    vct = _blocked_transpose(kvc[1])
    ks_aug = jnp.concatenate([zz[2 * G:3 * G], key_aug], axis=-1)
    kw_aug = jnp.concatenate([zz[4 * G:5 * G], key_aug], axis=-1)
    vst = _blocked_transpose(zz[3 * G:4 * G])
    vwt = _blocked_transpose(zz[5 * G:6 * G])

    def k_spec():
        return pl.BlockSpec((1, 1, T, 2 * Dh), lambda b, g, i: (g, b, 0, 0))

    def vt_spec():
        return pl.BlockSpec((1, 1, T // LANES, Dh, LANES), lambda b, g, i: (g, b, 0, 0, 0))

    n_rb = HG * Q_BLOCK // NSA_ROW_BLOCK
    return pl.pallas_call(
        _nsa_attn_body,
        grid=(B, G, T // Q_BLOCK),
        in_specs=[pl.BlockSpec((1, Q_BLOCK, HG * Dh), lambda b, g, i: (b, i, g)),
                  pl.BlockSpec((1, HG * Q_BLOCK, LANES), lambda b, g, i: (g, 0, 0)),
                  pl.BlockSpec((1, Q_BLOCK, LANES), lambda b, g, i: (b, i, g)),
                  pl.BlockSpec((1, 1, n_cmp_pad, 2 * Dh), lambda b, g, i: (b, g, 0, 0)),
                  pl.BlockSpec((1, 1, n_cmp_pad // LANES, Dh, LANES), lambda b, g, i: (b, g, 0, 0, 0)),
                  k_spec(), vt_spec(), k_spec(), vt_spec(),
                  pl.BlockSpec((n_slc, n_cmp_pad), lambda b, g, i: (0, 0)),
                  pl.BlockSpec((T, n_slc), lambda b, g, i: (0, 0))],
        out_specs=pl.BlockSpec((1, Q_BLOCK, HG * Dh), lambda b, g, i: (b, i, g)),
        out_shape=jax.ShapeDtypeStruct((B, T, H * Dh), BF16),
        scratch_shapes=[pltpu.VMEM((HG * Q_BLOCK, 2 * Dh), BF16),
                        pltpu.VMEM((Dh, HG * Q_BLOCK), F32),
                        pltpu.VMEM((n_rb, 1, NSA_ROW_BLOCK), F32),
                        pltpu.VMEM((n_rb, 1, NSA_ROW_BLOCK), F32),
                        pltpu.VMEM((Dh, HG * Q_BLOCK), F32),
                        pltpu.VMEM((n_rb, SLC_KEY_CHUNK, NSA_ROW_BLOCK), F32),
                        pltpu.VMEM((n_rb, n_cmp_pad, NSA_ROW_BLOCK), F32),
                        pltpu.VMEM((n_rb, WINDOW + Q_BLOCK, NSA_ROW_BLOCK), F32)],
        compiler_params=_params("arbitrary", "arbitrary", "arbitrary"),
        name="nsa_attention",
    )(q, _slope_pieces(), glog, kc_aug, vct, ks_aug, vst, kw_aug, vwt,
      jnp.asarray(mselt, BF16), jnp.asarray(efullt, BF16))


def _pack_rows(y, o_ref):
    m, d = y.shape
    bits = lax.bitcast_convert_type(y.astype(BF16).astype(F32), U32)
    for c in range(d // (2 * LANES)):
        lo = lax.shift_right_logical(bits[:, c * LANES:(c + 1) * LANES], jnp.uint32(16))
        hi = bits[:, d // 2 + c * LANES:d // 2 + (c + 1) * LANES]
        o_ref[pl.ds(c, m, stride=d // (2 * LANES)), :] = lo | hi


def _unpack_chunk(words):
    lo = lax.bitcast_convert_type(lax.shift_left(words, jnp.uint32(16)), F32)
    hi = lax.bitcast_convert_type(words & jnp.uint32(0xFFFF0000), F32)
    return lo, hi


def _moe_router_body(h_ref, g_ref, w_ref, b_ref, xp_ref, idx_ref, wgt_ref):
    x = h_ref[...]
    ms = jnp.mean(x * x, axis=-1, keepdims=True)
    xn = x * lax.rsqrt(ms + NORM_EPS) * g_ref[...]
    _pack_rows(xn, xp_ref)
    logits = jnp.dot(xn.astype(BF16), w_ref[...], preferred_element_type=F32) + b_ref[...]
    lane = lax.broadcasted_iota(I32, logits.shape, 1)
    logits = jnp.where(lane < N_EXPERTS, logits, NEG_INF)
    idx_out = jnp.zeros(logits.shape, I32)
    val_out = jnp.full(logits.shape, NEG_INF, F32)
    for k in range(TOP_K):
        m = jnp.max(logits, axis=-1, keepdims=True)
        first = jnp.min(jnp.where(logits == m, lane, LANES), axis=-1, keepdims=True)
        idx_out = jnp.where(lane == k, first, idx_out)
        val_out = jnp.where(lane == k, m, val_out)
        logits = jnp.where(lane == first, NEG_INF, logits)
    e = jnp.exp(val_out - jnp.max(val_out, axis=-1, keepdims=True))
    idx_ref[...] = idx_out
    wgt_ref[...] = e * (1.0 / jnp.sum(e, axis=-1, keepdims=True))


def moe_router(h, gain, w_router, b_router, tm=256):
    n, d = h.shape
    tm = min(tm, n)
    pr = d // (2 * LANES)
    w = _pad_cols(w_router).astype(BF16)
    b = _pad_cols(b_router.reshape(1, -1)).astype(F32)
    return pl.pallas_call(
        _moe_router_body,
        grid=(n // tm,),
        in_specs=[pl.BlockSpec((tm, d), lambda i: (i, 0)),
                  pl.BlockSpec((1, d), lambda i: (0, 0)),
                  pl.BlockSpec((d, LANES), lambda i: (0, 0)),
                  pl.BlockSpec((1, LANES), lambda i: (0, 0))],
        out_specs=[pl.BlockSpec((tm * pr, LANES), lambda i: (i, 0)),
                   pl.BlockSpec((tm, LANES), lambda i: (i, 0)),
                   pl.BlockSpec((tm, LANES), lambda i: (i, 0))],
        out_shape=[jax.ShapeDtypeStruct((n * pr, LANES), U32),
                   jax.ShapeDtypeStruct((n, LANES), I32), jax.ShapeDtypeStruct((n, LANES), F32)],
        compiler_params=_params("arbitrary"),
        name="moe_router",
    )(h, gain.reshape(1, d).astype(F32), w, b)


def _row_gather_copy(src_hbm, src_row, dst_buf, slot, r, sem, pr):
    return pltpu.make_async_copy(src_hbm.at[pl.ds(pl.multiple_of(src_row, pr), pr)],
                                 dst_buf.at[slot, pl.ds(pl.multiple_of(r * pr, pr), pr)], sem.at[slot])


def _start_row_gather(ids_ref, src_hbm, dst_buf, slot, sem, n_rows, pr):
    def body(r, carry):
        _row_gather_copy(src_hbm, ids_ref[0, 0, r], dst_buf, slot, r, sem, pr).start()
        return carry
    lax.fori_loop(0, n_rows, body, 0, unroll=8)


def _wait_row_gather(src_hbm, dst_buf, slot, sem):
    pltpu.make_async_copy(src_hbm.at[pl.ds(0, dst_buf.shape[1])], dst_buf.at[slot], sem.at[slot]).wait()


def _moe_expert_body(blk_e_ref, n_used_ref, ids0_ref, idsn_ref, x_hbm, wgu_ref, bgu_ref, wd_ref, bd_ref,
                     y_ref, xbuf, xs, sem):
    i = pl.program_id(0)
    n_used = n_used_ref[0]
    rows, d = xs.shape
    pr = d // (2 * LANES)
    ff = wd_ref.shape[1]

    @pl.when(i == 0)
    def _():
        _start_row_gather(ids0_ref, x_hbm, xbuf, 0, sem, rows, pr)

    @pl.when(i + 1 < n_used)
    def _():
        _start_row_gather(idsn_ref, x_hbm, xbuf, (i + 1) % 2, sem, rows, pr)

    @pl.when(i < n_used)
    def _():
        slot = i % 2
        _wait_row_gather(x_hbm, xbuf, slot, sem)
        for c in range(pr):
            lo, hi = _unpack_chunk(xbuf[slot, pl.ds(c, rows, stride=pr), :])
            xs[:, c * LANES:(c + 1) * LANES] = lo.astype(BF16)
            xs[:, d // 2 + c * LANES:d // 2 + (c + 1) * LANES] = hi.astype(BF16)
        gu = jnp.dot(xs[...], wgu_ref[0], preferred_element_type=F32) + bgu_ref[0]
        gate = jnp.minimum(gu[:, :ff], SWIGLU_LIMIT)
        up = jnp.clip(gu[:, ff:], -SWIGLU_LIMIT, SWIGLU_LIMIT)
        act = (up + 1.0) * (gate * jax.nn.sigmoid(SWIGLU_ALPHA * gate))
        y = jnp.dot(act.astype(BF16), wd_ref[0], preferred_element_type=F32) + bd_ref[0]
        _pack_rows(y, y_ref)

    @pl.when(i >= n_used)
    def _():
        y_ref[...] = jnp.zeros(y_ref.shape, y_ref.dtype)


def moe_experts(xp, tok_rows, blk_e, n_used, w_gate_up, b_gate_up, w_down, b_down):
    E, d, ff2 = w_gate_up.shape
    ff = ff2 // 2
    pr = d // (2 * LANES)
    rows = MOE_BLOCK_ROWS
    n_blocks = tok_rows.shape[0] // rows
    ids = tok_rows.reshape(n_blocks, 1, rows)

    def used(i, nu):
        return jnp.minimum(i, nu[0] - 1)

    grid_spec = pltpu.PrefetchScalarGridSpec(
        num_scalar_prefetch=2,
        grid=(n_blocks,),
        in_specs=[pl.BlockSpec((1, 1, rows), lambda i, be, nu: (0, 0, 0), memory_space=pltpu.SMEM),
                  pl.BlockSpec((1, 1, rows), lambda i, be, nu: (jnp.minimum(i + 1, n_blocks - 1), 0, 0),
                               memory_space=pltpu.SMEM),
                  pl.BlockSpec(memory_space=pl.ANY),
                  pl.BlockSpec((1, d, ff2), lambda i, be, nu: (be[used(i, nu)], 0, 0)),
                  pl.BlockSpec((1, 1, ff2), lambda i, be, nu: (be[used(i, nu)], 0, 0)),
                  pl.BlockSpec((1, ff, d), lambda i, be, nu: (be[used(i, nu)], 0, 0)),
                  pl.BlockSpec((1, 1, d), lambda i, be, nu: (be[used(i, nu)], 0, 0))],
        out_specs=pl.BlockSpec((rows * pr, LANES), lambda i, be, nu: (i, 0)),
        scratch_shapes=[pltpu.VMEM((2, rows * pr, LANES), U32), pltpu.VMEM((rows, d), BF16),
                        pltpu.SemaphoreType.DMA((2,))],
    )
    return pl.pallas_call(
        _moe_expert_body,
        grid_spec=grid_spec,
        out_shape=jax.ShapeDtypeStruct((n_blocks * rows * pr, LANES), U32),
        compiler_params=_params("arbitrary"),
        name="moe_experts",
    )(blk_e, n_used, ids, ids, xp, w_gate_up.astype(BF16), b_gate_up.reshape(E, 1, ff2).astype(F32),
      w_down.astype(BF16), b_down.reshape(E, 1, d).astype(F32))


def _moe_combine_body(ids0_ref, idsn_ref, y_hbm, h_ref, w_ref, o_ref, ybuf, sem):
    i = pl.program_id(0)
    n_steps = pl.num_programs(0)
    tt, d = h_ref.shape
    pr = d // (2 * LANES)
    rows = TOP_K * tt

    @pl.when(i == 0)
    def _():
        _start_row_gather(ids0_ref, y_hbm, ybuf, 0, sem, rows, pr)

    @pl.when(i + 1 < n_steps)
    def _():
        _start_row_gather(idsn_ref, y_hbm, ybuf, (i + 1) % 2, sem, rows, pr)

    slot = i % 2
    _wait_row_gather(y_hbm, ybuf, slot, sem)
    w = w_ref[...]
    wk = [jnp.broadcast_to(w[:, k:k + 1], (tt, LANES)) for k in range(TOP_K)]
    for c in range(pr):
        lo_cols = slice(c * LANES, (c + 1) * LANES)
        hi_cols = slice(d // 2 + c * LANES, d // 2 + (c + 1) * LANES)
        acc_lo = h_ref[:, lo_cols]
        acc_hi = h_ref[:, hi_cols]
        for k in range(TOP_K):
            lo, hi = _unpack_chunk(ybuf[slot, pl.ds(k * tt * pr + c, tt, stride=pr), :])
            acc_lo = acc_lo + wk[k] * lo
            acc_hi = acc_hi + wk[k] * hi
        o_ref[:, lo_cols] = acc_lo
        o_ref[:, hi_cols] = acc_hi


def moe_combine(yp, slot_rows, weights, h):
    n, d = h.shape
    pr = d // (2 * LANES)
    tt = min(MOE_COMBINE_TOKENS, n)
    n_steps = n // tt
    ids = slot_rows.reshape(n_steps, tt, TOP_K).transpose(0, 2, 1).reshape(n_steps, 1, TOP_K * tt)
    return pl.pallas_call(
        _moe_combine_body,
        grid=(n_steps,),
        in_specs=[pl.BlockSpec((1, 1, TOP_K * tt), lambda i: (0, 0, 0), memory_space=pltpu.SMEM),
                  pl.BlockSpec((1, 1, TOP_K * tt), lambda i: (jnp.minimum(i + 1, n_steps - 1), 0, 0),
                               memory_space=pltpu.SMEM),
                  pl.BlockSpec(memory_space=pl.ANY),
                  pl.BlockSpec((tt, d), lambda i: (i, 0)),
                  pl.BlockSpec((tt, LANES), lambda i: (i, 0))],
        out_specs=pl.BlockSpec((tt, d), lambda i: (i, 0)),
        out_shape=jax.ShapeDtypeStruct((n, d), F32),
        scratch_shapes=[pltpu.VMEM((2, TOP_K * tt * pr, LANES), U32), pltpu.SemaphoreType.DMA((2,))],
        compiler_params=_params("arbitrary"),
        name="moe_combine",
    )(ids, ids, yp, h, weights)


def _moe_plan(top_idx, n_rows):
    E, rows = N_EXPERTS, MOE_BLOCK_ROWS
    flat_e = top_idx.reshape(-1)
    nk = flat_e.shape[0]
    onehot = (flat_e[:, None] == jnp.arange(E, dtype=I32)[None, :]).astype(I32)
    csum = jnp.cumsum(onehot, axis=0)
    rank = jnp.sum(onehot * csum, axis=1) - 1
    sizes = csum[-1]
    nblk = (sizes + rows - 1) // rows
    blk_end = jnp.cumsum(nblk)
    pad_start = (blk_end - nblk) * rows
    slot_dest = pad_start[flat_e] + rank
    tok_buf = jnp.zeros((n_rows,), I32).at[slot_dest].set(jnp.arange(nk, dtype=I32) // TOP_K)
    n_blocks = n_rows // rows
    blk_e = jnp.minimum(jnp.searchsorted(blk_end, jnp.arange(n_blocks, dtype=I32), side='right'), E - 1).astype(I32)
    n_used = blk_end[-1:].astype(I32)
    return slot_dest, tok_buf, blk_e, n_used


def moe_layer(h, gain, w_router, b_router, w_gate_up, b_gate_up, w_down, b_down):
    n, d = h.shape
    pr = d // (2 * LANES)
    rows = MOE_BLOCK_ROWS
    xp, top_idx, weights = moe_router(h, gain, w_router, b_router)
    n_rows = n * TOP_K + N_EXPERTS * rows
    slot_dest, tok_buf, blk_e, n_used = _moe_plan(top_idx[:, :TOP_K], n_rows)
    yp = moe_experts(xp, tok_buf * pr, blk_e, n_used, w_gate_up, b_gate_up, w_down, b_down)
    return moe_combine(yp, slot_dest.reshape(n, TOP_K) * pr, weights, h)


def _gla_mixer(h, xn, w_in, w_gate2, b_gate2, out_gain, w_out, B, T):
    n_main = w_in.shape[1] - GLA_GATE_RANK
    proj = matmul(xn, w_in[:, :n_main].astype(BF16), out_dtype=BF16)
    a_pad = matmul(xn, _pad_cols(w_in[:, n_main:]).astype(BF16))
    o = gla_core(proj, a_pad, w_gate2, b_gate2, out_gain, B, T)
    return matmul(o, w_out.astype(BF16), residual=h)


def _nsa_shared_kv(h, kv_gain, w_kv, k_gain, cmp_pe, cmp_w1, cmp_w2, B, T):
    G = NSA_GROUPS
    Dh = LANES
    xn = rmsnorm(h, kv_gain)
    ones = jnp.ones((G * Dh,), F32)
    gain_cols = jnp.concatenate([ones, ones, jnp.tile(k_gain[1], G), ones, jnp.tile(k_gain[2], G), ones])
    norm_groups = [False] * (2 * G) + [True] * G + [False] * G + [True] * G + [False] * G
    zz = matmul_groupnorm(xn, w_kv.astype(BF16), gain_cols, norm_groups, split_out=True, tn=6 * G * Dh)
    kvc = nsa_compress(zz, cmp_pe, cmp_w1, cmp_w2, k_gain[0], B, T)
    return kvc, zz.reshape(6 * G, B, T, Dh)


def _nsa_mixer(h, xn, w_in, q_gain, w_out, kvc, zz, B, T):
    H, G, HG, Dh = NSA_HEADS, NSA_GROUPS, NSA_HG, LANES
    q = matmul_groupnorm(xn, w_in[:, :H * Dh].astype(BF16), jnp.tile(q_gain, H), [True] * 4,
                         scale=Dh ** -0.5 * LOG2E, tm=1024, tn=512)
    wg = w_in[:, H * Dh:].reshape(-1, G, HG, 3).transpose(0, 1, 3, 2).reshape(-1, G, 3 * HG)
    wg = jnp.pad(wg, ((0, 0), (0, 0), (0, LANES - 3 * HG))).reshape(-1, G * LANES)
    glog = matmul(xn, wg.astype(BF16))
    o = nsa_attention(q.reshape(B, T, H * Dh), glog.reshape(B, T, G * LANES), kvc, zz, B, T)
    return matmul(o.reshape(B * T, H * Dh), w_out.astype(BF16), residual=h)


def kernel(x, ln_mix, ln_ffn, a_w_in, a_w_gate2, a_b_gate2, a_out_gain, a_w_out, kv_gain, w_kv, k_gain,
           cmp_pe, cmp_w1, cmp_w2, b_w_in, b_q_gain, b_w_out, w_router, b_router, w_gate_up, b_gate_up,
           w_down, b_down):
    B, T, D = x.shape
    depth = ln_mix.shape[0]
    n_a = a_w_in.shape[0]
    h = x.reshape(B * T, D)
    shared = None
    for l in range(depth):
        xn = rmsnorm(h, ln_mix[l])
        if l < n_a:
            h = _gla_mixer(h, xn, a_w_in[l], a_w_gate2[l], a_b_gate2[l], a_out_gain[l], a_w_out[l], B, T)
        else:
            if l == n_a:
                shared = _nsa_shared_kv(h, kv_gain, w_kv, k_gain, cmp_pe, cmp_w1, cmp_w2, B, T)
            j = l - n_a
            h = _nsa_mixer(h, xn, b_w_in[j], b_q_gain[j], b_w_out[j], *shared, B, T)
        h = moe_layer(h, ln_ffn[l], w_router[l], b_router[l], w_gate_up[l], b_gate_up[l], w_down[l], b_down[l])
    return h.reshape(B, T, D)
```

```python
import functools
import math

import jax
import jax.numpy as jnp
from jax import lax
import numpy as np
from jax.experimental import pallas as pl
from jax.experimental.pallas import tpu as pltpu

F32 = jnp.float32
BF16 = jnp.bfloat16
I32 = jnp.int32
U32 = jnp.uint32

NORM_EPS = 1e-5
GLA_HEADS = 8
GLA_GATE_RANK = 16
GLA_GATE_TAU = 16.0
NSA_HEADS = 32
NSA_GROUPS = 2
NSA_HG = NSA_HEADS // NSA_GROUPS
CMP_BLOCK = 32
CMP_STRIDE = 16
SLC_BLOCK = 64
SLC_TOPK = 8
WINDOW = 512
Q_BLOCK = 128
N_EXPERTS = 32
TOP_K = 4
SWIGLU_ALPHA = 1.702
SWIGLU_LIMIT = 7.0

LANES = 128
MXU_DIM = 256
VMEM_LIMIT_BYTES = 56 * 1024 * 1024
MOE_BLOCK_ROWS = 512
MOE_SUB_BLOCKS = 4
MOE_COMBINE_TOKENS = 128
SLC_KEY_CHUNK = 512
NSA_ROW_BLOCK = 256
GLA_BLOCK = 128
GLA_STEP_TOKENS = 512
NEG_INF = float("-inf")
LOG2E = math.log2(math.e)


def _params(*sem):
    return pltpu.CompilerParams(dimension_semantics=sem, vmem_limit_bytes=VMEM_LIMIT_BYTES)


def _split3(x):
    hi = x.astype(BF16)
    r1 = x - hi.astype(F32)
    mid = r1.astype(BF16)
    lo = (r1 - mid.astype(F32)).astype(BF16)
    return hi, mid, lo


def _rmsnorm_body(x_ref, g_ref, o_ref):
    x = x_ref[...].astype(F32)
    ms = jnp.mean(x * x, axis=-1, keepdims=True)
    o_ref[...] = (x * lax.rsqrt(ms + NORM_EPS) * g_ref[...].astype(F32)).astype(o_ref.dtype)


def rmsnorm(x, gain, out_dtype=BF16, tm=512):
    m, d = x.shape
    tm = min(tm, m)
    return pl.pallas_call(
        _rmsnorm_body,
        grid=(m // tm,),
        in_specs=[pl.BlockSpec((tm, d), lambda i: (i, 0)),
                  pl.BlockSpec((1, d), lambda i: (0, 0))],
        out_specs=pl.BlockSpec((tm, d), lambda i: (i, 0)),
        out_shape=jax.ShapeDtypeStruct((m, d), out_dtype),
        compiler_params=_params("arbitrary"),
        name="rmsnorm",
    )(x, gain.reshape(1, d))


def _matmul_body(a_ref, w_ref, o_ref):
    o_ref[...] = jnp.dot(a_ref[...], w_ref[0].astype(BF16), preferred_element_type=F32).astype(o_ref.dtype)


def _matmul_res_body(a_ref, w_ref, r_ref, o_ref):
    acc = jnp.dot(a_ref[...], w_ref[0].astype(BF16), preferred_element_type=F32)
    o_ref[...] = (r_ref[...].astype(F32) + acc).astype(o_ref.dtype)


def matmul(a, w, layer=0, n=None, residual=None, out_dtype=F32, tm=1024, tn=512):
    m, k = a.shape
    n = w.shape[2] if n is None else n
    tm = min(tm, m)
    tn = min(tn, n)
    assert m % tm == 0 and n % tn == 0, (m, n, tm, tn)
    in_specs = [pl.BlockSpec((tm, k), lambda i, j: (i, 0)),
                pl.BlockSpec((1, k, tn), lambda i, j: (layer, 0, j))]
    args = [a, w]
    body = _matmul_body
    if residual is not None:
        in_specs.append(pl.BlockSpec((tm, tn), lambda i, j: (i, j)))
        args.append(residual)
        body = _matmul_res_body
    return pl.pallas_call(
        body,
        grid=(m // tm, n // tn),
        in_specs=in_specs,
        out_specs=pl.BlockSpec((tm, tn), lambda i, j: (i, j)),
        out_shape=jax.ShapeDtypeStruct((m, n), out_dtype),
        compiler_params=_params("arbitrary", "arbitrary"),
        name="matmul",
    )(*args)


def _matmul_groupnorm_body(a_ref, w_ref, g_ref, o_ref, *, norm_groups, scale, split_out):
    acc = jnp.dot(a_ref[...], w_ref[0].astype(BF16), preferred_element_type=F32)
    for c, do_norm in enumerate(norm_groups):
        seg = acc[:, c * LANES:(c + 1) * LANES]
        if do_norm:
            ms = jnp.mean(seg * seg, axis=-1, keepdims=True)
            seg = seg * lax.rsqrt(ms + NORM_EPS) * g_ref[:, c * LANES:(c + 1) * LANES] * scale
        if split_out:
            o_ref[c] = seg.astype(o_ref.dtype)
        else:
            o_ref[:, c * LANES:(c + 1) * LANES] = seg.astype(o_ref.dtype)


def matmul_groupnorm(a, w, gain_cols, norm_groups, layer=0, n=None, scale=1.0, split_out=False, out_dtype=BF16,
                     tm=512, tn=512):
    m, k = a.shape
    n = w.shape[2] if n is None else n
    tm = min(tm, m)
    tn = min(tn, n)
    assert m % tm == 0 and n % tn == 0 and len(norm_groups) == tn // LANES
    if split_out:
        out_shape = jax.ShapeDtypeStruct((n // LANES, m, LANES), out_dtype)
        out_spec = pl.BlockSpec((tn // LANES, tm, LANES), lambda i, j: (j, i, 0))
    else:
        out_shape = jax.ShapeDtypeStruct((m, n), out_dtype)
        out_spec = pl.BlockSpec((tm, tn), lambda i, j: (i, j))
    body = functools.partial(_matmul_groupnorm_body, norm_groups=tuple(norm_groups), scale=scale,
                             split_out=split_out)
    return pl.pallas_call(
        body,
        grid=(m // tm, n // tn),
        in_specs=[pl.BlockSpec((tm, k), lambda i, j: (i, 0)),
                  pl.BlockSpec((1, k, tn), lambda i, j: (layer, 0, j)),
                  pl.BlockSpec((1, tn), lambda i, j: (0, j))],
        out_specs=out_spec,
        out_shape=out_shape,
        compiler_params=_params("arbitrary", "arbitrary"),
        name="matmul_groupnorm",
    )(a, w, gain_cols.reshape(1, n).astype(F32))


def _pad_cols(w, mult=LANES):
    pad = (-w.shape[-1]) % mult
    return jnp.pad(w, ((0, 0), (0, pad))) if pad else w


def _gla_body(q_ref, k_ref, v_ref, g_ref, a_ref, wg_ref, bg_ref, og_ref, tri_ref, o_ref, s_ref):
    dk = q_ref.shape[1]
    C = GLA_BLOCK

    @pl.when(pl.program_id(2) == 0)
    def _():
        s_ref[...] = jnp.zeros(s_ref.shape, F32)

    tri = tri_ref[...]
    row = lax.broadcasted_iota(I32, (C, C), 0)
    col = lax.broadcasted_iota(I32, (C, C), 1)
    for c in range(q_ref.shape[0] // C):
        rows = slice(c * C, (c + 1) * C)
        gate_in = jnp.dot(a_ref[rows, :].astype(BF16), wg_ref[...], preferred_element_type=F32) + bg_ref[...]
        log_a = jax.nn.log_sigmoid(gate_in) * (1.0 / GLA_GATE_TAU)
        hi, mid, lo = _split3(log_a)
        bcum = (jnp.dot(tri, hi, preferred_element_type=F32) + jnp.dot(tri, mid, preferred_element_type=F32)
                + jnp.dot(tri, lo, preferred_element_type=F32))
        b_mid = bcum[C // 2 - 1:C // 2, :]
        q = q_ref[rows, :].astype(F32) * dk ** -0.5
        k = k_ref[rows, :].astype(F32)
        v = v_ref[rows, :]
        att = lax.dot_general((q * jnp.exp(bcum - b_mid)).astype(BF16), (k * jnp.exp(b_mid - bcum)).astype(BF16),
                              (((1,), (1,)), ((), ())), preferred_element_type=F32)
        att = jnp.where(col <= row, att, 0.0)
        o = jnp.dot(att.astype(BF16), v, preferred_element_type=F32)
        o = o + jnp.dot((q * jnp.exp(bcum)).astype(BF16), s_ref[...].astype(BF16), preferred_element_type=F32)
        bcum_t = bcum.T
        b_last = bcum_t[:, C - 1:C]
        k_t = (k.T * jnp.exp(b_last - bcum_t)).astype(BF16)
        s_ref[...] = s_ref[...] * jnp.exp(b_last) + jnp.dot(k_t, v, preferred_element_type=F32)
        ms = jnp.mean(o * o, axis=-1, keepdims=True)
        o = o * lax.rsqrt(ms + NORM_EPS) * og_ref[...]
        o_ref[rows, :] = (o * jax.nn.silu(g_ref[rows, :].astype(F32))).astype(o_ref.dtype)


def gla_core(proj, a_pad, w_gate2, b_gate2, out_gain, B, T):
    H = GLA_HEADS
    n = proj.shape[0]
    dv = out_gain.shape[0]
    dk = w_gate2.shape[1] // H
    tb = min(GLA_STEP_TOKENS, T)
    nt = T // tb
    wg = jnp.pad(w_gate2, ((0, LANES - w_gate2.shape[0]), (0, 0))).astype(BF16)
    tri = jnp.asarray(np.tril(np.ones((GLA_BLOCK, GLA_BLOCK), np.float32)), BF16)
    v0 = 2 * H * dk // dv
    return pl.pallas_call(
        _gla_body,
        grid=(B, H, nt),
        in_specs=[pl.BlockSpec((tb, dk), lambda b, h, i: (b * nt + i, h)),
                  pl.BlockSpec((tb, dk), lambda b, h, i: (b * nt + i, H + h)),
                  pl.BlockSpec((tb, dv), lambda b, h, i: (b * nt + i, v0 + h)),
                  pl.BlockSpec((tb, dv), lambda b, h, i: (b * nt + i, v0 + H + h)),
                  pl.BlockSpec((tb, LANES), lambda b, h, i: (b * nt + i, 0)),
                  pl.BlockSpec((LANES, dk), lambda b, h, i: (0, h)),
                  pl.BlockSpec((1, dk), lambda b, h, i: (0, h)),
                  pl.BlockSpec((1, dv), lambda b, h, i: (0, 0)),
                  pl.BlockSpec((GLA_BLOCK, GLA_BLOCK), lambda b, h, i: (0, 0))],
        out_specs=pl.BlockSpec((tb, dv), lambda b, h, i: (b * nt + i, h)),
        out_shape=jax.ShapeDtypeStruct((n, H * dv), BF16),
        scratch_shapes=[pltpu.VMEM((dk, dv), F32)],
        compiler_params=_params("arbitrary", "arbitrary", "arbitrary"),
        name="gla_core",
    )(proj, proj, proj, proj, a_pad, wg, b_gate2.reshape(1, -1).astype(F32), out_gain.reshape(1, dv).astype(F32), tri)


def _nsa_compress_body(u_ref, w1_ref, w2_ref, pe_ref, kg_ref, o_ref):
    kv = pl.program_id(0)
    u = u_ref[0, 0]
    half = u.shape[1]
    w1 = w1_ref[0]
    a = jnp.dot(u, w1[:half], preferred_element_type=F32)
    b = jnp.dot(u, w1[half:], preferred_element_type=F32)
    pe_term = jnp.dot(pe_ref[0], w1, preferred_element_type=F32)[0:1]
    n_chunks = u.shape[0]
    hid = jax.nn.gelu(a + pltpu.roll(b, n_chunks - 1, 0) + pe_term, approximate=True)
    out = jnp.dot(hid.astype(BF16), w2_ref[0], preferred_element_type=F32)
    ms = jnp.mean(out * out, axis=-1, keepdims=True)
    normed = out * lax.rsqrt(ms + NORM_EPS) * kg_ref[...]
    res = jnp.where(kv == 0, normed, out)
    row = lax.broadcasted_iota(I32, res.shape, 0)
    o_ref[0, 0, 0] = jnp.where(row < n_chunks - 1, res, 0.0).astype(o_ref.dtype)


def nsa_compress(zz, cmp_pe, cmp_w1, cmp_w2, kc_gain, B, T):
    G = NSA_GROUPS
    Dh = zz.shape[-1]
    n_chunks = T // CMP_STRIDE
    u = zz.reshape(zz.shape[0], B, n_chunks, CMP_STRIDE * Dh)
    w1 = cmp_w1.reshape(2, CMP_BLOCK * Dh, Dh).astype(BF16)
    pe = jnp.broadcast_to(cmp_pe.reshape(2, 1, CMP_BLOCK * Dh), (2, 8, CMP_BLOCK * Dh)).astype(BF16)
    return pl.pallas_call(
        _nsa_compress_body,
        grid=(2, B, G),
        in_specs=[pl.BlockSpec((1, 1, n_chunks, CMP_STRIDE * Dh), lambda kv, b, g: (kv * G + g, b, 0, 0)),
                  pl.BlockSpec((1, CMP_BLOCK * Dh, Dh), lambda kv, b, g: (kv, 0, 0)),
                  pl.BlockSpec((1, Dh, Dh), lambda kv, b, g: (kv, 0, 0)),
                  pl.BlockSpec((1, 8, CMP_BLOCK * Dh), lambda kv, b, g: (kv, 0, 0)),
                  pl.BlockSpec((1, Dh), lambda kv, b, g: (0, 0))],
        out_specs=pl.BlockSpec((1, 1, 1, n_chunks, Dh), lambda kv, b, g: (kv, b, g, 0, 0)),
        out_shape=jax.ShapeDtypeStruct((2, B, G, n_chunks, Dh), BF16),
        compiler_params=_params("arbitrary", "arbitrary", "arbitrary"),
        name="nsa_compress",
    )(u, w1, cmp_w2.astype(BF16), pe, kc_gain.reshape(1, Dh).astype(F32))


def _dot_nt(a, b):
    return lax.dot_general(a, b, (((1,), (1,)), ((), ())), preferred_element_type=F32)


def _col_softmax_terms(s):
    m = jnp.max(s, axis=0, keepdims=True)
    m = jnp.where(m == NEG_INF, 0.0, m)
    e = jnp.exp2(s - m)
    return e, 1.0 / jnp.maximum(jnp.sum(e, axis=0, keepdims=True), 1e-30)


def _tile_lanes(x, n):
    return jnp.concatenate([x] * n, axis=1)


def _nsa_attn_body(q_ref, qaug_ref, glog_ref, kc_ref, vct_ref, ks_ref, vst_ref, kw_ref, vwt_ref,
                   mselt_ref, efullt_ref, o_ref, q2_ref, acct_ref, m_ref, l_ref, oacct_ref, s_ref, sc_ref, sw_ref):
    qb = pl.program_id(2)
    Dh = LANES
    RB = NSA_ROW_BLOCK
    hpb = RB // Q_BLOCK
    n_rb = NSA_HG // hpb
    n_cmp_pad = kc_ref.shape[2]
    n_slc = mselt_ref.shape[0]
    win_keys = WINDOW + Q_BLOCK
    s0 = qb * Q_BLOCK
    t_q = s0 + lax.broadcasted_iota(I32, (1, Q_BLOCK), 1)

    gates_t = jax.nn.sigmoid(glog_ref[0]).T

    for hg in range(NSA_HG):
        q2_ref[hg * Q_BLOCK:(hg + 1) * Q_BLOCK, :Dh] = q_ref[0, :, hg * Dh:(hg + 1) * Dh]
    q2_ref[:, Dh:] = qaug_ref[0]

    cmp_end = lax.broadcasted_iota(I32, (n_cmp_pad, 1), 0) * CMP_STRIDE + (CMP_BLOCK - 1)
    cmp_mask = _tile_lanes(jnp.where(cmp_end <= t_q, 0.0, NEG_INF), hpb)

    ws = pl.multiple_of(jnp.maximum(s0 - WINDOW, 0), Q_BLOCK)
    wdist = t_q - (ws + lax.broadcasted_iota(I32, (win_keys, 1), 0))
    win_mask = _tile_lanes(jnp.where((wdist >= 0) & (wdist < WINDOW), 0.0, NEG_INF), hpb)

    kc = kc_ref[0, 0]
    vct = jnp.concatenate([vct_ref[0, 0, j] for j in range(n_cmp_pad // Dh)], axis=1)
    kwin = kw_ref[0, 0, pl.ds(ws, win_keys), :]
    wblk = ws // Dh
    vwt = jnp.concatenate([vwt_ref[0, 0, wblk + j] for j in range(win_keys // Dh)], axis=1)

    imp_t = jnp.zeros((n_cmp_pad, Q_BLOCK), F32)
    for rb in range(n_rb):
        q_rb = q2_ref[rb * RB:(rb + 1) * RB, :]
        sc_ref[rb] = _dot_nt(kc, q_rb) + cmp_mask
        sw_ref[rb] = _dot_nt(kwin, q_rb) + win_mask
    for rb in range(n_rb):
        e, inv = _col_softmax_terms(sc_ref[rb])
        p = e * inv
        for j in range(hpb):
            imp_t = imp_t + p[:, j * Q_BLOCK:(j + 1) * Q_BLOCK]
        o_c = jnp.dot(vct, p.astype(BF16), preferred_element_type=F32)
        e, inv = _col_softmax_terms(sw_ref[rb])
        o_w = jnp.dot(vwt, e.astype(BF16), preferred_element_type=F32) * inv
        for j in range(hpb):
            hg = rb * hpb + j
            sub = slice(j * Q_BLOCK, (j + 1) * Q_BLOCK)
            acct_ref[:, hg * Q_BLOCK:(hg + 1) * Q_BLOCK] = (
                gates_t[hg:hg + 1, :] * o_c[:, sub] + gates_t[2 * NSA_HG + hg:2 * NSA_HG + hg + 1, :] * o_w[:, sub])

    hi, mid, lo = _split3(imp_t)
    mselt = mselt_ref[...]
    slc_imp = (jnp.dot(mselt, hi, preferred_element_type=F32) + jnp.dot(mselt, mid, preferred_element_type=F32)
               + jnp.dot(mselt, lo, preferred_element_type=F32))

    blk = lax.broadcasted_iota(I32, (n_slc, 1), 0)
    cur = lax.shift_right_logical(t_q, 6)
    forced = (blk == 0) | (blk == cur) | (blk == cur - 1)
    score = jnp.where(forced, jnp.inf, jnp.where(blk <= cur, slc_imp, NEG_INF))
    sel = jnp.zeros((n_slc, Q_BLOCK), F32)
    for _ in range(SLC_TOPK):
        m = jnp.max(score, axis=0, keepdims=True)
        cand = (score == m) & (m > NEG_INF)
        first = jnp.min(jnp.where(cand, blk, n_slc), axis=0, keepdims=True)
        one = blk == first
        sel = jnp.where(one, 1.0, sel)
        score = jnp.where(one, NEG_INF, score)
    sel_b = sel.astype(BF16)

    m_ref[...] = jnp.full(m_ref.shape, NEG_INF, F32)
    l_ref[...] = jnp.zeros(l_ref.shape, F32)
    oacct_ref[...] = jnp.zeros(oacct_ref.shape, F32)
    KC = SLC_KEY_CHUNK
    key_iota = lax.broadcasted_iota(I32, (KC, 1), 0)

    def chunk_step(c, carry):
        k0 = pl.multiple_of(c * KC, KC)
        kch = ks_ref[0, 0, pl.ds(k0, KC), :]
        vt = jnp.concatenate([vst_ref[0, 0, c * (KC // Dh) + j] for j in range(KC // Dh)], axis=1)
        selk = jnp.dot(efullt_ref[pl.ds(k0, KC), :], sel_b, preferred_element_type=F32)
        mask = _tile_lanes(jnp.where((selk > 0.5) & (k0 + key_iota <= t_q), 0.0, NEG_INF), hpb)
        for rb in range(n_rb):
            s_ref[rb] = _dot_nt(kch, q2_ref[rb * RB:(rb + 1) * RB, :]) + mask
        for rb in range(n_rb):
            cols = slice(rb * RB, (rb + 1) * RB)
            s = s_ref[rb]
            m_old = m_ref[rb]
            m_new = jnp.maximum(m_old, jnp.max(s, axis=0, keepdims=True))
            m_safe = jnp.where(m_new == NEG_INF, 0.0, m_new)
            alpha = jnp.exp2(m_old - m_safe)
            p = jnp.exp2(s - m_safe)
            l_ref[rb] = alpha * l_ref[rb] + jnp.sum(p, axis=0, keepdims=True)
            oacct_ref[:, cols] = alpha * oacct_ref[:, cols] + jnp.dot(vt, p.astype(BF16), preferred_element_type=F32)
            m_ref[rb] = m_new
        return carry

    n_chunks = (s0 + Q_BLOCK + KC - 1) // KC
    lax.fori_loop(0, n_chunks, chunk_step, 0)
    for hg in range(NSA_HG):
        rb, j = divmod(hg, hpb)
        cols = slice(hg * Q_BLOCK, (hg + 1) * Q_BLOCK)
        inv = 1.0 / jnp.maximum(l_ref[rb][:, j * Q_BLOCK:(j + 1) * Q_BLOCK], 1e-30)
        o_t = acct_ref[:, cols] + gates_t[NSA_HG + hg:NSA_HG + hg + 1, :] * (oacct_ref[:, cols] * inv)
        o_ref[0, :, hg * Dh:(hg + 1) * Dh] = o_t.T.astype(o_ref.dtype)


def _pos_pieces(pos):
    pos = np.asarray(pos)
    out = np.zeros((pos.shape[0], LANES), np.float32)
    for i in range(3):
        out[:, 2 * i] = 64 * (pos // 64)
        out[:, 2 * i + 1] = pos % 64
    return jnp.asarray(out, BF16)


def _slope_pieces():
    H, G, HG = NSA_HEADS, NSA_GROUPS, NSA_HG
    slopes = jnp.asarray(LOG2E * 2.0 ** (-8.0 * np.arange(1, H + 1, dtype=np.float64) / H), F32)
    pieces = jnp.stack(_split3(slopes), axis=-1)
    cols = jnp.repeat(pieces, 2, axis=-1)
    cols = jnp.pad(cols, ((0, 0), (0, LANES - cols.shape[-1])))
    return jnp.repeat(cols.reshape(G, HG, 1, LANES), Q_BLOCK, axis=2).reshape(G, HG * Q_BLOCK, LANES)


def _blocked_transpose(v):
    lead, (t, dh) = v.shape[:-2], v.shape[-2:]
    return jnp.swapaxes(v.reshape(*lead, t // LANES, LANES, dh), -1, -2)


def nsa_attention(q, glog, kvc, zz, B, T):
    G, HG, Dh = NSA_GROUPS, NSA_HG, LANES
    H = NSA_HEADS
    n_cmp = T // CMP_STRIDE - CMP_BLOCK // CMP_STRIDE + 1
    n_cmp_pad = -(-T // CMP_STRIDE // LANES) * LANES
    n_slc = T // SLC_BLOCK
    per = SLC_BLOCK // CMP_STRIDE
    n = np.arange(n_cmp_pad)[None, :]
    j = np.arange(n_slc)[:, None]
    mselt = ((n // per == j).astype(np.float32) + ((n + 1) // per == j).astype(np.float32))
    mselt[:, n_cmp:] = 0.0
    kvc = jnp.pad(kvc, ((0, 0), (0, 0), (0, 0), (0, n_cmp_pad - kvc.shape[3]), (0, 0)))
    efullt = (np.arange(T)[:, None] // SLC_BLOCK == np.arange(n_slc)[None, :]).astype(np.float32)

    key_aug = jnp.broadcast_to(_pos_pieces(np.arange(T)), (G, B, T, LANES))
    cmp_aug = jnp.broadcast_to(_pos_pieces(np.arange(n_cmp_pad) * CMP_STRIDE + CMP_BLOCK - 1),
                               (B, G, n_cmp_pad, LANES))
    kc_aug = jnp.concatenate([kvc[0], cmp_aug], axis=-1)
    vct = _blocked_transpose(kvc[1])
    ks_aug = jnp.concatenate([zz[2 * G:3 * G], key_aug], axis=-1)
    kw_aug = jnp.concatenate([zz[4 * G:5 * G], key_aug], axis=-1)
    vst = _blocked_transpose(zz[3 * G:4 * G])
    vwt = _blocked_transpose(zz[5 * G:6 * G])

    def k_spec():
        return pl.BlockSpec((1, 1, T, 2 * Dh), lambda b, g, i: (g, b, 0, 0))

    def vt_spec():
        return pl.BlockSpec((1, 1, T // LANES, Dh, LANES), lambda b, g, i: (g, b, 0, 0, 0))

    n_rb = HG * Q_BLOCK // NSA_ROW_BLOCK
    return pl.pallas_call(
        _nsa_attn_body,
        grid=(B, G, T // Q_BLOCK),
        in_specs=[pl.BlockSpec((1, Q_BLOCK, HG * Dh), lambda b, g, i: (b, i, g)),
                  pl.BlockSpec((1, HG * Q_BLOCK, LANES), lambda b, g, i: (g, 0, 0)),
                  pl.BlockSpec((1, Q_BLOCK, LANES), lambda b, g, i: (b, i, g)),
                  pl.BlockSpec((1, 1, n_cmp_pad, 2 * Dh), lambda b, g, i: (b, g, 0, 0)),
                  pl.BlockSpec((1, 1, n_cmp_pad // LANES, Dh, LANES), lambda b, g, i: (b, g, 0, 0, 0)),
                  k_spec(), vt_spec(), k_spec(), vt_spec(),
                  pl.BlockSpec((n_slc, n_cmp_pad), lambda b, g, i: (0, 0)),
                  pl.BlockSpec((T, n_slc), lambda b, g, i: (0, 0))],
        out_specs=pl.BlockSpec((1, Q_BLOCK, HG * Dh), lambda b, g, i: (b, i, g)),
        out_shape=jax.ShapeDtypeStruct((B, T, H * Dh), BF16),
        scratch_shapes=[pltpu.VMEM((HG * Q_BLOCK, 2 * Dh), BF16),
                        pltpu.VMEM((Dh, HG * Q_BLOCK), F32),
                        pltpu.VMEM((n_rb, 1, NSA_ROW_BLOCK), F32),
                        pltpu.VMEM((n_rb, 1, NSA_ROW_BLOCK), F32),
                        pltpu.VMEM((Dh, HG * Q_BLOCK), F32),
                        pltpu.VMEM((n_rb, SLC_KEY_CHUNK, NSA_ROW_BLOCK), F32),
                        pltpu.VMEM((n_rb, n_cmp_pad, NSA_ROW_BLOCK), F32),
                        pltpu.VMEM((n_rb, WINDOW + Q_BLOCK, NSA_ROW_BLOCK), F32)],
        compiler_params=_params("arbitrary", "arbitrary", "arbitrary"),
        name="nsa_attention",
    )(q, _slope_pieces(), glog, kc_aug, vct, ks_aug, vst, kw_aug, vwt,
      jnp.asarray(mselt, BF16), jnp.asarray(efullt, BF16))


def _pack_rows(y, o_ref):
    m, d = y.shape
    bits = lax.bitcast_convert_type(y.astype(BF16).astype(F32), U32)
    for c in range(d // (2 * LANES)):
        lo = lax.shift_right_logical(bits[:, c * LANES:(c + 1) * LANES], jnp.uint32(16))
        hi = bits[:, d // 2 + c * LANES:d // 2 + (c + 1) * LANES]
        o_ref[pl.ds(c, m, stride=d // (2 * LANES)), :] = lo | hi


def _unpack_chunk(words):
    lo = lax.bitcast_convert_type(lax.shift_left(words, jnp.uint32(16)), F32)
    hi = lax.bitcast_convert_type(words & jnp.uint32(0xFFFF0000), F32)
    return lo, hi


def _moe_router_body(h_ref, g_ref, w_ref, b_ref, xp_ref, idx_ref, wgt_ref):
    x = h_ref[...]
    ms = jnp.mean(x * x, axis=-1, keepdims=True)
    xn = x * lax.rsqrt(ms + NORM_EPS) * g_ref[...]
    _pack_rows(xn, xp_ref)
    logits = jnp.dot(xn.astype(BF16), w_ref[...], preferred_element_type=F32) + b_ref[...]
    lane = lax.broadcasted_iota(I32, logits.shape, 1)
    logits = jnp.where(lane < N_EXPERTS, logits, NEG_INF)
    idx_out = jnp.zeros(logits.shape, I32)
    val_out = jnp.full(logits.shape, NEG_INF, F32)
    for k in range(TOP_K):
        m = jnp.max(logits, axis=-1, keepdims=True)
        first = jnp.min(jnp.where(logits == m, lane, LANES), axis=-1, keepdims=True)
        idx_out = jnp.where(lane == k, first, idx_out)
        val_out = jnp.where(lane == k, m, val_out)
        logits = jnp.where(lane == first, NEG_INF, logits)
    e = jnp.exp(val_out - jnp.max(val_out, axis=-1, keepdims=True))
    idx_ref[...] = idx_out
    wgt_ref[...] = e * (1.0 / jnp.sum(e, axis=-1, keepdims=True))


def moe_router(h, gain, w_router, b_router, tm=256):
    n, d = h.shape
    tm = min(tm, n)
    pr = d // (2 * LANES)
    w = _pad_cols(w_router).astype(BF16)
    b = _pad_cols(b_router.reshape(1, -1)).astype(F32)
    return pl.pallas_call(
        _moe_router_body,
        grid=(n // tm,),
        in_specs=[pl.BlockSpec((tm, d), lambda i: (i, 0)),
                  pl.BlockSpec((1, d), lambda i: (0, 0)),
                  pl.BlockSpec((d, LANES), lambda i: (0, 0)),
                  pl.BlockSpec((1, LANES), lambda i: (0, 0))],
        out_specs=[pl.BlockSpec((tm * pr, LANES), lambda i: (i, 0)),
                   pl.BlockSpec((tm, LANES), lambda i: (i, 0)),
                   pl.BlockSpec((tm, LANES), lambda i: (i, 0))],
        out_shape=[jax.ShapeDtypeStruct((n * pr, LANES), U32),
                   jax.ShapeDtypeStruct((n, LANES), I32), jax.ShapeDtypeStruct((n, LANES), F32)],
        compiler_params=_params("arbitrary"),
        name="moe_router",
    )(h, gain.reshape(1, d).astype(F32), w, b)


def _row_gather_copy(src_hbm, src_row, dst_buf, slot, r, sem, pr):
    return pltpu.make_async_copy(src_hbm.at[pl.ds(pl.multiple_of(src_row, pr), pr)],
                                 dst_buf.at[slot, pl.ds(pl.multiple_of(r * pr, pr), pr)], sem.at[slot])


def _start_row_gather(ids_ref, src_hbm, dst_buf, slot, sem, n_rows, pr):
    def body(r, carry):
        _row_gather_copy(src_hbm, ids_ref[0, 0, r], dst_buf, slot, r, sem, pr).start()
        return carry
    lax.fori_loop(0, n_rows, body, 0, unroll=8)


def _wait_row_gather(src_hbm, dst_buf, slot, sem):
    pltpu.make_async_copy(src_hbm.at[pl.ds(0, dst_buf.shape[1])], dst_buf.at[slot], sem.at[slot]).wait()


def _moe_expert_body(blk_e_ref, n_used_ref, ids0_ref, idsn_ref, x_hbm, wgu_ref, bgu_ref, wd_ref, bd_ref,
                     y_ref, xbuf, xs, sem):
    i = pl.program_id(0)
    n_used = n_used_ref[0]
    rows, d = xs.shape
    pr = d // (2 * LANES)
    ff = wd_ref.shape[1]

    @pl.when(i == 0)
    def _():
        _start_row_gather(ids0_ref, x_hbm, xbuf, 0, sem, rows, pr)

    @pl.when(i + 1 < n_used)
    def _():
        _start_row_gather(idsn_ref, x_hbm, xbuf, (i + 1) % 2, sem, rows, pr)

    @pl.when(i < n_used)
    def _():
        slot = i % 2
        _wait_row_gather(x_hbm, xbuf, slot, sem)
        sub = rows // MOE_SUB_BLOCKS
        for sb in range(MOE_SUB_BLOCKS):
            r0 = sb * sub
            for c in range(pr):
                lo, hi = _unpack_chunk(xbuf[slot, pl.ds(r0 * pr + c, sub, stride=pr), :])
                xs[r0:r0 + sub, c * LANES:(c + 1) * LANES] = lo.astype(BF16)
                xs[r0:r0 + sub, d // 2 + c * LANES:d // 2 + (c + 1) * LANES] = hi.astype(BF16)
            gu = jnp.dot(xs[r0:r0 + sub, :], wgu_ref[0], preferred_element_type=F32) + bgu_ref[0]
            gate = jnp.minimum(gu[:, :ff], SWIGLU_LIMIT)
            up = jnp.clip(gu[:, ff:], -SWIGLU_LIMIT, SWIGLU_LIMIT)
            act = (up + 1.0) * (gate * jax.nn.sigmoid(SWIGLU_ALPHA * gate))
            y = jnp.dot(act.astype(BF16), wd_ref[0], preferred_element_type=F32) + bd_ref[0]
            _pack_rows(y, y_ref.at[pl.ds(r0 * pr, sub * pr), :])

    @pl.when(i >= n_used)
    def _():
        y_ref[...] = jnp.zeros(y_ref.shape, y_ref.dtype)


def moe_experts(xp, tok_rows, blk_e, n_used, w_gate_up, b_gate_up, w_down, b_down):
    E, d, ff2 = w_gate_up.shape
    ff = ff2 // 2
    pr = d // (2 * LANES)
    rows = MOE_BLOCK_ROWS
    n_blocks = tok_rows.shape[0] // rows
    ids = tok_rows.reshape(n_blocks, 1, rows)

    def used(i, nu):
        return jnp.minimum(i, nu[0] - 1)

    grid_spec = pltpu.PrefetchScalarGridSpec(
        num_scalar_prefetch=2,
        grid=(n_blocks,),
        in_specs=[pl.BlockSpec((1, 1, rows), lambda i, be, nu: (0, 0, 0), memory_space=pltpu.SMEM),
                  pl.BlockSpec((1, 1, rows), lambda i, be, nu: (jnp.minimum(i + 1, n_blocks - 1), 0, 0),
                               memory_space=pltpu.SMEM),
                  pl.BlockSpec(memory_space=pl.ANY),
                  pl.BlockSpec((1, d, ff2), lambda i, be, nu: (be[used(i, nu)], 0, 0)),
                  pl.BlockSpec((1, 1, ff2), lambda i, be, nu: (be[used(i, nu)], 0, 0)),
                  pl.BlockSpec((1, ff, d), lambda i, be, nu: (be[used(i, nu)], 0, 0)),
                  pl.BlockSpec((1, 1, d), lambda i, be, nu: (be[used(i, nu)], 0, 0))],
        out_specs=pl.BlockSpec((rows * pr, LANES), lambda i, be, nu: (i, 0)),
        scratch_shapes=[pltpu.VMEM((2, rows * pr, LANES), U32), pltpu.VMEM((rows, d), BF16),
                        pltpu.SemaphoreType.DMA((2,))],
    )
    return pl.pallas_call(
        _moe_expert_body,
        grid_spec=grid_spec,
        out_shape=jax.ShapeDtypeStruct((n_blocks * rows * pr, LANES), U32),
        compiler_params=_params("arbitrary"),
        name="moe_experts",
    )(blk_e, n_used, ids, ids, xp, w_gate_up.astype(BF16), b_gate_up.reshape(E, 1, ff2).astype(F32),
      w_down.astype(BF16), b_down.reshape(E, 1, d).astype(F32))


def _moe_combine_body(ids0_ref, idsn_ref, y_hbm, h_ref, w_ref, g_ref, o_ref, xn_ref, ybuf, sem):
    i = pl.program_id(0)
    n_steps = pl.num_programs(0)
    tt, d = h_ref.shape
    pr = d // (2 * LANES)
    rows = TOP_K * tt

    @pl.when(i == 0)
    def _():
        _start_row_gather(ids0_ref, y_hbm, ybuf, 0, sem, rows, pr)

    @pl.when(i + 1 < n_steps)
    def _():
        _start_row_gather(idsn_ref, y_hbm, ybuf, (i + 1) % 2, sem, rows, pr)

    slot = i % 2
    _wait_row_gather(y_hbm, ybuf, slot, sem)
    w = w_ref[...]
    wk = [jnp.broadcast_to(w[:, k:k + 1], (tt, LANES)) for k in range(TOP_K)]
    ssq = jnp.zeros((tt, LANES), F32)
    for c in range(pr):
        lo_cols = slice(c * LANES, (c + 1) * LANES)
        hi_cols = slice(d // 2 + c * LANES, d // 2 + (c + 1) * LANES)
        acc_lo = h_ref[:, lo_cols]
        acc_hi = h_ref[:, hi_cols]
        for k in range(TOP_K):
            lo, hi = _unpack_chunk(ybuf[slot, pl.ds(k * tt * pr + c, tt, stride=pr), :])
            acc_lo = acc_lo + wk[k] * lo
            acc_hi = acc_hi + wk[k] * hi
        o_ref[:, lo_cols] = acc_lo
        o_ref[:, hi_cols] = acc_hi
        ssq = ssq + acc_lo * acc_lo + acc_hi * acc_hi
    inv = lax.rsqrt(jnp.sum(ssq, axis=-1, keepdims=True) * (1.0 / d) + NORM_EPS)
    xn_ref[...] = (o_ref[...] * inv * g_ref[...]).astype(xn_ref.dtype)


def moe_combine(yp, slot_rows, weights, h, next_gain):
    n, d = h.shape
    pr = d // (2 * LANES)
    tt = min(MOE_COMBINE_TOKENS, n)
    n_steps = n // tt
    ids = slot_rows.reshape(n_steps, tt, TOP_K).transpose(0, 2, 1).reshape(n_steps, 1, TOP_K * tt)
    return pl.pallas_call(
        _moe_combine_body,
        grid=(n_steps,),
        in_specs=[pl.BlockSpec((1, 1, TOP_K * tt), lambda i: (0, 0, 0), memory_space=pltpu.SMEM),
                  pl.BlockSpec((1, 1, TOP_K * tt), lambda i: (jnp.minimum(i + 1, n_steps - 1), 0, 0),
                               memory_space=pltpu.SMEM),
                  pl.BlockSpec(memory_space=pl.ANY),
                  pl.BlockSpec((tt, d), lambda i: (i, 0)),
                  pl.BlockSpec((tt, LANES), lambda i: (i, 0)),
                  pl.BlockSpec((1, d), lambda i: (0, 0))],
        out_specs=[pl.BlockSpec((tt, d), lambda i: (i, 0)),
                   pl.BlockSpec((tt, d), lambda i: (i, 0))],
        out_shape=[jax.ShapeDtypeStruct((n, d), F32), jax.ShapeDtypeStruct((n, d), BF16)],
        scratch_shapes=[pltpu.VMEM((2, TOP_K * tt * pr, LANES), U32), pltpu.SemaphoreType.DMA((2,))],
        compiler_params=_params("arbitrary"),
        name="moe_combine",
    )(ids, ids, yp, h, weights, next_gain.reshape(1, d).astype(F32))


def _moe_plan(top_idx, n_rows):
    E, rows = N_EXPERTS, MOE_BLOCK_ROWS
    flat_e = top_idx.reshape(-1)
    nk = flat_e.shape[0]
    onehot = (flat_e[:, None] == jnp.arange(E, dtype=I32)[None, :]).astype(I32)
    csum = jnp.cumsum(onehot, axis=0)
    rank = jnp.sum(onehot * csum, axis=1) - 1
    sizes = csum[-1]
    nblk = (sizes + rows - 1) // rows
    blk_end = jnp.cumsum(nblk)
    pad_start = (blk_end - nblk) * rows
    slot_dest = pad_start[flat_e] + rank
    tok_buf = jnp.zeros((n_rows,), I32).at[slot_dest].set(jnp.arange(nk, dtype=I32) // TOP_K)
    n_blocks = n_rows // rows
    blk_e = jnp.minimum(jnp.searchsorted(blk_end, jnp.arange(n_blocks, dtype=I32), side='right'), E - 1).astype(I32)
    n_used = blk_end[-1:].astype(I32)
    return slot_dest, tok_buf, blk_e, n_used


def moe_layer(h, gain, w_router, b_router, w_gate_up, b_gate_up, w_down, b_down, next_gain):
    n, d = h.shape
    pr = d // (2 * LANES)
    rows = MOE_BLOCK_ROWS
    xp, top_idx, weights = moe_router(h, gain, w_router, b_router)
    n_rows = n * TOP_K + N_EXPERTS * rows
    slot_dest, tok_buf, blk_e, n_used = _moe_plan(top_idx[:, :TOP_K], n_rows)
    yp = moe_experts(xp, tok_buf * pr, blk_e, n_used, w_gate_up, b_gate_up, w_down, b_down)
    return moe_combine(yp, slot_dest.reshape(n, TOP_K) * pr, weights, h, next_gain)


def _gla_mixer(h, xn, w_in, w_gate2, b_gate2, out_gain, w_out, l, B, T):
    n_main = w_in.shape[2] - GLA_GATE_RANK
    proj = matmul(xn, w_in, layer=l, n=n_main, out_dtype=BF16)
    a_pad = matmul(xn, _pad_cols(w_in[l, :, n_main:])[None])
    o = gla_core(proj, a_pad, w_gate2, b_gate2, out_gain, B, T)
    return matmul(o, w_out, layer=l, residual=h)


def _nsa_shared_kv(h, kv_gain, w_kv, k_gain, cmp_pe, cmp_w1, cmp_w2, B, T):
    G = NSA_GROUPS
    Dh = LANES
    xn = rmsnorm(h, kv_gain)
    ones = jnp.ones((G * Dh,), F32)
    gain_cols = jnp.concatenate([ones, ones, jnp.tile(k_gain[1], G), ones, jnp.tile(k_gain[2], G), ones])
    norm_groups = [False] * (2 * G) + [True] * G + [False] * G + [True] * G + [False] * G
    zz = matmul_groupnorm(xn, w_kv.astype(BF16)[None], gain_cols, norm_groups, split_out=True, tn=6 * G * Dh)
    kvc = nsa_compress(zz, cmp_pe, cmp_w1, cmp_w2, k_gain[0], B, T)
    return kvc, zz.reshape(6 * G, B, T, Dh)


def _nsa_mixer(h, xn, w_in, q_gain, w_out, kvc, zz, j, B, T):
    H, G, HG, Dh = NSA_HEADS, NSA_GROUPS, NSA_HG, LANES
    q = matmul_groupnorm(xn, w_in, jnp.tile(q_gain, H), [True] * 4, layer=j, n=H * Dh,
                         scale=Dh ** -0.5 * LOG2E, tm=1024, tn=512)
    wg = w_in[j, :, H * Dh:].reshape(-1, G, HG, 3).transpose(0, 1, 3, 2).reshape(-1, G, 3 * HG)
    wg = jnp.pad(wg, ((0, 0), (0, 0), (0, LANES - 3 * HG))).reshape(1, -1, G * LANES)
    glog = matmul(xn, wg)
    o = nsa_attention(q.reshape(B, T, H * Dh), glog.reshape(B, T, G * LANES), kvc, zz, B, T)
    return matmul(o.reshape(B * T, H * Dh), w_out, layer=j, residual=h)


def kernel(x, ln_mix, ln_ffn, a_w_in, a_w_gate2, a_b_gate2, a_out_gain, a_w_out, kv_gain, w_kv, k_gain,
           cmp_pe, cmp_w1, cmp_w2, b_w_in, b_q_gain, b_w_out, w_router, b_router, w_gate_up, b_gate_up,
           w_down, b_down):
    B, T, D = x.shape
    depth = ln_mix.shape[0]
    n_a = a_w_in.shape[0]
    h = x.reshape(B * T, D)
    shared = None
    xn = rmsnorm(h, ln_mix[0])
    for l in range(depth):
        if l < n_a:
            h = _gla_mixer(h, xn, a_w_in, a_w_gate2[l], a_b_gate2[l], a_out_gain[l], a_w_out, l, B, T)
        else:
            if l == n_a:
                shared = _nsa_shared_kv(h, kv_gain, w_kv, k_gain, cmp_pe, cmp_w1, cmp_w2, B, T)
            j = l - n_a
            h = _nsa_mixer(h, xn, b_w_in, b_q_gain[j], b_w_out, *shared, j, B, T)
        h, xn = moe_layer(h, ln_ffn[l], w_router[l], b_router[l], w_gate_up[l], b_gate_up[l], w_down[l], b_down[l],
                          ln_mix[min(l + 1, depth - 1)])
    return h.reshape(B, T, D)
```

```python
import functools
import math

import jax
import jax.numpy as jnp
from jax import lax
import numpy as np
from jax.experimental import pallas as pl
from jax.experimental.pallas import tpu as pltpu

F32 = jnp.float32
BF16 = jnp.bfloat16
I32 = jnp.int32
U32 = jnp.uint32

NORM_EPS = 1e-5
GLA_HEADS = 8
GLA_GATE_RANK = 16
GLA_GATE_TAU = 16.0
NSA_HEADS = 32
NSA_GROUPS = 2
NSA_HG = NSA_HEADS // NSA_GROUPS
CMP_BLOCK = 32
CMP_STRIDE = 16
SLC_BLOCK = 64
SLC_TOPK = 8
WINDOW = 512
Q_BLOCK = 128
N_EXPERTS = 32
TOP_K = 4
SWIGLU_ALPHA = 1.702
SWIGLU_LIMIT = 7.0

LANES = 128
MXU_DIM = 256
VMEM_LIMIT_BYTES = 56 * 1024 * 1024
MOE_BLOCK_ROWS = 512
MOE_SUB_BLOCKS = 4
MOE_COMBINE_TOKENS = 128
SLC_KEY_CHUNK = 512
NSA_ROW_BLOCK = 256
GLA_BLOCK = 128
GLA_STEP_TOKENS = 512
GLA_HEADS_PER_STEP = 2
NEG_INF = float("-inf")
LOG2E = math.log2(math.e)


def _params(*sem):
    return pltpu.CompilerParams(dimension_semantics=sem, vmem_limit_bytes=VMEM_LIMIT_BYTES)


def _split3(x):
    hi = x.astype(BF16)
    r1 = x - hi.astype(F32)
    mid = r1.astype(BF16)
    lo = (r1 - mid.astype(F32)).astype(BF16)
    return hi, mid, lo


def _rmsnorm_body(x_ref, g_ref, o_ref):
    x = x_ref[...].astype(F32)
    ms = jnp.mean(x * x, axis=-1, keepdims=True)
    o_ref[...] = (x * lax.rsqrt(ms + NORM_EPS) * g_ref[...].astype(F32)).astype(o_ref.dtype)


def rmsnorm(x, gain, out_dtype=BF16, tm=512):
    m, d = x.shape
    tm = min(tm, m)
    return pl.pallas_call(
        _rmsnorm_body,
        grid=(m // tm,),
        in_specs=[pl.BlockSpec((tm, d), lambda i: (i, 0)),
                  pl.BlockSpec((1, d), lambda i: (0, 0))],
        out_specs=pl.BlockSpec((tm, d), lambda i: (i, 0)),
        out_shape=jax.ShapeDtypeStruct((m, d), out_dtype),
        compiler_params=_params("arbitrary"),
        name="rmsnorm",
    )(x, gain.reshape(1, d))


def _matmul_body(a_ref, w_ref, o_ref):
    o_ref[...] = jnp.dot(a_ref[...], w_ref[0].astype(BF16), preferred_element_type=F32).astype(o_ref.dtype)


def _matmul_res_body(a_ref, w_ref, r_ref, o_ref):
    acc = jnp.dot(a_ref[...], w_ref[0].astype(BF16), preferred_element_type=F32)
    o_ref[...] = (r_ref[...].astype(F32) + acc).astype(o_ref.dtype)


def matmul(a, w, layer=0, n=None, residual=None, out_dtype=F32, tm=1024, tn=512):
    m, k = a.shape
    n = w.shape[2] if n is None else n
    tm = min(tm, m)
    tn = min(tn, n)
    assert m % tm == 0 and n % tn == 0, (m, n, tm, tn)
    in_specs = [pl.BlockSpec((tm, k), lambda i, j: (i, 0)),
                pl.BlockSpec((1, k, tn), lambda i, j: (layer, 0, j))]
    args = [a, w]
    body = _matmul_body
    if residual is not None:
        in_specs.append(pl.BlockSpec((tm, tn), lambda i, j: (i, j)))
        args.append(residual)
        body = _matmul_res_body
    return pl.pallas_call(
        body,
        grid=(m // tm, n // tn),
        in_specs=in_specs,
        out_specs=pl.BlockSpec((tm, tn), lambda i, j: (i, j)),
        out_shape=jax.ShapeDtypeStruct((m, n), out_dtype),
        compiler_params=_params("arbitrary", "arbitrary"),
        name="matmul",
    )(*args)


def _matmul_groupnorm_body(a_ref, w_ref, g_ref, o_ref, *, norm_groups, scale, split_out):
    acc = jnp.dot(a_ref[...], w_ref[0].astype(BF16), preferred_element_type=F32)
    for c, do_norm in enumerate(norm_groups):
        seg = acc[:, c * LANES:(c + 1) * LANES]
        if do_norm:
            ms = jnp.mean(seg * seg, axis=-1, keepdims=True)
            seg = seg * lax.rsqrt(ms + NORM_EPS) * g_ref[:, c * LANES:(c + 1) * LANES] * scale
        if split_out:
            o_ref[c] = seg.astype(o_ref.dtype)
        else:
            o_ref[:, c * LANES:(c + 1) * LANES] = seg.astype(o_ref.dtype)


def matmul_groupnorm(a, w, gain_cols, norm_groups, layer=0, n=None, scale=1.0, split_out=False, out_dtype=BF16,
                     tm=512, tn=512):
    m, k = a.shape
    n = w.shape[2] if n is None else n
    tm = min(tm, m)
    tn = min(tn, n)
    assert m % tm == 0 and n % tn == 0 and len(norm_groups) == tn // LANES
    if split_out:
        out_shape = jax.ShapeDtypeStruct((n // LANES, m, LANES), out_dtype)
        out_spec = pl.BlockSpec((tn // LANES, tm, LANES), lambda i, j: (j, i, 0))
    else:
        out_shape = jax.ShapeDtypeStruct((m, n), out_dtype)
        out_spec = pl.BlockSpec((tm, tn), lambda i, j: (i, j))
    body = functools.partial(_matmul_groupnorm_body, norm_groups=tuple(norm_groups), scale=scale,
                             split_out=split_out)
    return pl.pallas_call(
        body,
        grid=(m // tm, n // tn),
        in_specs=[pl.BlockSpec((tm, k), lambda i, j: (i, 0)),
                  pl.BlockSpec((1, k, tn), lambda i, j: (layer, 0, j)),
                  pl.BlockSpec((1, tn), lambda i, j: (0, j))],
        out_specs=out_spec,
        out_shape=out_shape,
        compiler_params=_params("arbitrary", "arbitrary"),
        name="matmul_groupnorm",
    )(a, w, gain_cols.reshape(1, n).astype(F32))


def _pad_cols(w, mult=LANES):
    pad = (-w.shape[-1]) % mult
    return jnp.pad(w, ((0, 0), (0, pad))) if pad else w


def _gla_body(q_ref, k_ref, v_ref, g_ref, a_ref, wg_ref, bg_ref, og_ref, tri_ref, o_ref, s_ref):
    dk, dv = s_ref.shape[1], s_ref.shape[2]
    C = GLA_BLOCK

    @pl.when(pl.program_id(2) == 0)
    def _():
        s_ref[...] = jnp.zeros(s_ref.shape, F32)

    tri = tri_ref[...]
    row = lax.broadcasted_iota(I32, (C, C), 0)
    col = lax.broadcasted_iota(I32, (C, C), 1)
    for c in range(q_ref.shape[0] // C):
        rows = slice(c * C, (c + 1) * C)
        a = a_ref[rows, :].astype(BF16)
        for hh in range(s_ref.shape[0]):
            kc = slice(hh * dk, (hh + 1) * dk)
            vc = slice(hh * dv, (hh + 1) * dv)
            gate_in = jnp.dot(a, wg_ref[:, kc], preferred_element_type=F32) + bg_ref[:, kc]
            log_a = jax.nn.log_sigmoid(gate_in) * (1.0 / GLA_GATE_TAU)
            hi, mid, lo = _split3(log_a)
            bcum = (jnp.dot(tri, hi, preferred_element_type=F32) + jnp.dot(tri, mid, preferred_element_type=F32)
                    + jnp.dot(tri, lo, preferred_element_type=F32))
            b_mid = bcum[C // 2 - 1:C // 2, :]
            q = q_ref[rows, kc].astype(F32) * dk ** -0.5
            k = k_ref[rows, kc].astype(F32)
            v = v_ref[rows, vc]
            att = lax.dot_general((q * jnp.exp(bcum - b_mid)).astype(BF16),
                                  (k * jnp.exp(b_mid - bcum)).astype(BF16),
                                  (((1,), (1,)), ((), ())), preferred_element_type=F32)
            att = jnp.where(col <= row, att, 0.0)
            o = jnp.dot(att.astype(BF16), v, preferred_element_type=F32)
            o = o + jnp.dot((q * jnp.exp(bcum)).astype(BF16), s_ref[hh].astype(BF16), preferred_element_type=F32)
            bcum_t = bcum.T
            b_last = bcum_t[:, C - 1:C]
            k_t = (k.T * jnp.exp(b_last - bcum_t)).astype(BF16)
            s_ref[hh] = s_ref[hh] * jnp.exp(b_last) + jnp.dot(k_t, v, preferred_element_type=F32)
            ms = jnp.mean(o * o, axis=-1, keepdims=True)
            o = o * lax.rsqrt(ms + NORM_EPS) * og_ref[...]
            o_ref[rows, vc] = (o * jax.nn.silu(g_ref[rows, vc].astype(F32))).astype(o_ref.dtype)


def gla_core(proj, a_pad, w_gate2, b_gate2, out_gain, B, T):
    H = GLA_HEADS
    n = proj.shape[0]
    dv = out_gain.shape[0]
    dk = w_gate2.shape[1] // H
    tb = min(GLA_STEP_TOKENS, T)
    nt = T // tb
    wg = jnp.pad(w_gate2, ((0, LANES - w_gate2.shape[0]), (0, 0))).astype(BF16)
    tri = jnp.asarray(np.tril(np.ones((GLA_BLOCK, GLA_BLOCK), np.float32)), BF16)
    hp = GLA_HEADS_PER_STEP
    hs = H // hp
    v0 = 2 * H * dk // (hp * dv)
    return pl.pallas_call(
        _gla_body,
        grid=(B, hs, nt),
        in_specs=[pl.BlockSpec((tb, hp * dk), lambda b, h, i: (b * nt + i, h)),
                  pl.BlockSpec((tb, hp * dk), lambda b, h, i: (b * nt + i, hs + h)),
                  pl.BlockSpec((tb, hp * dv), lambda b, h, i: (b * nt + i, v0 + h)),
                  pl.BlockSpec((tb, hp * dv), lambda b, h, i: (b * nt + i, v0 + hs + h)),
                  pl.BlockSpec((tb, LANES), lambda b, h, i: (b * nt + i, 0)),
                  pl.BlockSpec((LANES, hp * dk), lambda b, h, i: (0, h)),
                  pl.BlockSpec((1, hp * dk), lambda b, h, i: (0, h)),
                  pl.BlockSpec((1, dv), lambda b, h, i: (0, 0)),
                  pl.BlockSpec((GLA_BLOCK, GLA_BLOCK), lambda b, h, i: (0, 0))],
        out_specs=pl.BlockSpec((tb, hp * dv), lambda b, h, i: (b * nt + i, h)),
        out_shape=jax.ShapeDtypeStruct((n, H * dv), BF16),
        scratch_shapes=[pltpu.VMEM((hp, dk, dv), F32)],
        compiler_params=_params("arbitrary", "arbitrary", "arbitrary"),
        name="gla_core",
    )(proj, proj, proj, proj, a_pad, wg, b_gate2.reshape(1, -1).astype(F32), out_gain.reshape(1, dv).astype(F32), tri)


def _nsa_compress_body(u_ref, w1_ref, w2_ref, pe_ref, kg_ref, o_ref):
    kv = pl.program_id(0)
    u = u_ref[0, 0]
    half = u.shape[1]
    w1 = w1_ref[0]
    a = jnp.dot(u, w1[:half], preferred_element_type=F32)
    b = jnp.dot(u, w1[half:], preferred_element_type=F32)
    pe_term = jnp.dot(pe_ref[0], w1, preferred_element_type=F32)[0:1]
    n_chunks = u.shape[0]
    hid = jax.nn.gelu(a + pltpu.roll(b, n_chunks - 1, 0) + pe_term, approximate=True)
    out = jnp.dot(hid.astype(BF16), w2_ref[0], preferred_element_type=F32)
    ms = jnp.mean(out * out, axis=-1, keepdims=True)
    normed = out * lax.rsqrt(ms + NORM_EPS) * kg_ref[...]
    res = jnp.where(kv == 0, normed, out)
    row = lax.broadcasted_iota(I32, res.shape, 0)
    o_ref[0, 0, 0] = jnp.where(row < n_chunks - 1, res, 0.0).astype(o_ref.dtype)


def nsa_compress(zz, cmp_pe, cmp_w1, cmp_w2, kc_gain, B, T):
    G = NSA_GROUPS
    Dh = zz.shape[-1]
    n_chunks = T // CMP_STRIDE
    u = zz.reshape(zz.shape[0], B, n_chunks, CMP_STRIDE * Dh)
    w1 = cmp_w1.reshape(2, CMP_BLOCK * Dh, Dh).astype(BF16)
    pe = jnp.broadcast_to(cmp_pe.reshape(2, 1, CMP_BLOCK * Dh), (2, 8, CMP_BLOCK * Dh)).astype(BF16)
    return pl.pallas_call(
        _nsa_compress_body,
        grid=(2, B, G),
        in_specs=[pl.BlockSpec((1, 1, n_chunks, CMP_STRIDE * Dh), lambda kv, b, g: (kv * G + g, b, 0, 0)),
                  pl.BlockSpec((1, CMP_BLOCK * Dh, Dh), lambda kv, b, g: (kv, 0, 0)),
                  pl.BlockSpec((1, Dh, Dh), lambda kv, b, g: (kv, 0, 0)),
                  pl.BlockSpec((1, 8, CMP_BLOCK * Dh), lambda kv, b, g: (kv, 0, 0)),
                  pl.BlockSpec((1, Dh), lambda kv, b, g: (0, 0))],
        out_specs=pl.BlockSpec((1, 1, 1, n_chunks, Dh), lambda kv, b, g: (kv, b, g, 0, 0)),
        out_shape=jax.ShapeDtypeStruct((2, B, G, n_chunks, Dh), BF16),
        compiler_params=_params("arbitrary", "arbitrary", "arbitrary"),
        name="nsa_compress",
    )(u, w1, cmp_w2.astype(BF16), pe, kc_gain.reshape(1, Dh).astype(F32))


def _dot_nt(a, b):
    return lax.dot_general(a, b, (((1,), (1,)), ((), ())), preferred_element_type=F32)


def _col_softmax_terms(s):
    m = jnp.max(s, axis=0, keepdims=True)
    m = jnp.where(m == NEG_INF, 0.0, m)
    e = jnp.exp2(s - m)
    return e, 1.0 / jnp.maximum(jnp.sum(e, axis=0, keepdims=True), 1e-30)


def _tile_lanes(x, n):
    return jnp.concatenate([x] * n, axis=1)


def _nsa_attn_body(q_ref, qaug_ref, glog_ref, kc_ref, vct_ref, ks_ref, vst_ref, kw_ref, vwt_ref,
                   mselt_ref, efullt_ref, o_ref, q2_ref, acct_ref, m_ref, l_ref, oacct_ref, s_ref, sc_ref, sw_ref):
    qb = pl.program_id(2)
    Dh = LANES
    RB = NSA_ROW_BLOCK
    hpb = RB // Q_BLOCK
    n_rb = NSA_HG // hpb
    n_cmp_pad = kc_ref.shape[2]
    n_slc = mselt_ref.shape[0]
    win_keys = WINDOW + Q_BLOCK
    s0 = qb * Q_BLOCK
    t_q = s0 + lax.broadcasted_iota(I32, (1, Q_BLOCK), 1)

    gates_t = jax.nn.sigmoid(glog_ref[0]).T

    for hg in range(NSA_HG):
        q2_ref[hg * Q_BLOCK:(hg + 1) * Q_BLOCK, :Dh] = q_ref[0, :, hg * Dh:(hg + 1) * Dh]
    q2_ref[:, Dh:] = qaug_ref[0]

    cmp_end = lax.broadcasted_iota(I32, (n_cmp_pad, 1), 0) * CMP_STRIDE + (CMP_BLOCK - 1)
    cmp_mask = _tile_lanes(jnp.where(cmp_end <= t_q, 0.0, NEG_INF), hpb)

    ws = pl.multiple_of(jnp.maximum(s0 - WINDOW, 0), Q_BLOCK)
    wdist = t_q - (ws + lax.broadcasted_iota(I32, (win_keys, 1), 0))
    win_mask = _tile_lanes(jnp.where((wdist >= 0) & (wdist < WINDOW), 0.0, NEG_INF), hpb)

    kc = kc_ref[0, 0]
    vct = jnp.concatenate([vct_ref[0, 0, j] for j in range(n_cmp_pad // Dh)], axis=1)
    kwin = kw_ref[0, 0, pl.ds(ws, win_keys), :]
    wblk = ws // Dh
    vwt = jnp.concatenate([vwt_ref[0, 0, wblk + j] for j in range(win_keys // Dh)], axis=1)

    imp_t = jnp.zeros((n_cmp_pad, Q_BLOCK), F32)
    for rb in range(n_rb):
        q_rb = q2_ref[rb * RB:(rb + 1) * RB, :]
        sc_ref[rb] = _dot_nt(kc, q_rb) + cmp_mask
        sw_ref[rb] = _dot_nt(kwin, q_rb) + win_mask
    for rb in range(n_rb):
        e, inv = _col_softmax_terms(sc_ref[rb])
        p = e * inv
        for j in range(hpb):
            imp_t = imp_t + p[:, j * Q_BLOCK:(j + 1) * Q_BLOCK]
        o_c = jnp.dot(vct, p.astype(BF16), preferred_element_type=F32)
        e, inv = _col_softmax_terms(sw_ref[rb])
        o_w = jnp.dot(vwt, e.astype(BF16), preferred_element_type=F32) * inv
        for j in range(hpb):
            hg = rb * hpb + j
            sub = slice(j * Q_BLOCK, (j + 1) * Q_BLOCK)
            acct_ref[:, hg * Q_BLOCK:(hg + 1) * Q_BLOCK] = (
                gates_t[hg:hg + 1, :] * o_c[:, sub] + gates_t[2 * NSA_HG + hg:2 * NSA_HG + hg + 1, :] * o_w[:, sub])

    hi, mid, lo = _split3(imp_t)
    mselt = mselt_ref[...]
    slc_imp = (jnp.dot(mselt, hi, preferred_element_type=F32) + jnp.dot(mselt, mid, preferred_element_type=F32)
               + jnp.dot(mselt, lo, preferred_element_type=F32))

    blk = lax.broadcasted_iota(I32, (n_slc, 1), 0)
    cur = lax.shift_right_logical(t_q, 6)
    forced = (blk == 0) | (blk == cur) | (blk == cur - 1)
    score = jnp.where(forced, jnp.inf, jnp.where(blk <= cur, slc_imp, NEG_INF))
    sel = jnp.zeros((n_slc, Q_BLOCK), F32)
    for _ in range(SLC_TOPK):
        m = jnp.max(score, axis=0, keepdims=True)
        cand = (score == m) & (m > NEG_INF)
        first = jnp.min(jnp.where(cand, blk, n_slc), axis=0, keepdims=True)
        one = blk == first
        sel = jnp.where(one, 1.0, sel)
        score = jnp.where(one, NEG_INF, score)
    sel_b = sel.astype(BF16)

    m_ref[...] = jnp.full(m_ref.shape, NEG_INF, F32)
    l_ref[...] = jnp.zeros(l_ref.shape, F32)
    oacct_ref[...] = jnp.zeros(oacct_ref.shape, F32)
    KC = SLC_KEY_CHUNK
    key_iota = lax.broadcasted_iota(I32, (KC, 1), 0)

    def chunk_step(c, carry):
        k0 = pl.multiple_of(c * KC, KC)
        kch = ks_ref[0, 0, pl.ds(k0, KC), :]
        vt = jnp.concatenate([vst_ref[0, 0, c * (KC // Dh) + j] for j in range(KC // Dh)], axis=1)
        selk = jnp.dot(efullt_ref[pl.ds(k0, KC), :], sel_b, preferred_element_type=F32)
        mask = _tile_lanes(jnp.where((selk > 0.5) & (k0 + key_iota <= t_q), 0.0, NEG_INF), hpb)
        for rb in range(n_rb):
            s_ref[rb] = _dot_nt(kch, q2_ref[rb * RB:(rb + 1) * RB, :]) + mask
        for rb in range(n_rb):
            cols = slice(rb * RB, (rb + 1) * RB)
            s = s_ref[rb]
            m_old = m_ref[rb]
            m_new = jnp.maximum(m_old, jnp.max(s, axis=0, keepdims=True))
            m_safe = jnp.where(m_new == NEG_INF, 0.0, m_new)
            alpha = jnp.exp2(m_old - m_safe)
            p = jnp.exp2(s - m_safe)
            l_ref[rb] = alpha * l_ref[rb] + jnp.sum(p, axis=0, keepdims=True)
            oacct_ref[:, cols] = alpha * oacct_ref[:, cols] + jnp.dot(vt, p.astype(BF16), preferred_element_type=F32)
            m_ref[rb] = m_new
        return carry

    n_chunks = (s0 + Q_BLOCK + KC - 1) // KC
    lax.fori_loop(0, n_chunks, chunk_step, 0)
    for hg in range(NSA_HG):
        rb, j = divmod(hg, hpb)
        cols = slice(hg * Q_BLOCK, (hg + 1) * Q_BLOCK)
        inv = 1.0 / jnp.maximum(l_ref[rb][:, j * Q_BLOCK:(j + 1) * Q_BLOCK], 1e-30)
        o_t = acct_ref[:, cols] + gates_t[NSA_HG + hg:NSA_HG + hg + 1, :] * (oacct_ref[:, cols] * inv)
        o_ref[0, :, hg * Dh:(hg + 1) * Dh] = o_t.T.astype(o_ref.dtype)


def _pos_pieces(pos):
    pos = np.asarray(pos)
    out = np.zeros((pos.shape[0], LANES), np.float32)
    for i in range(3):
        out[:, 2 * i] = 64 * (pos // 64)
        out[:, 2 * i + 1] = pos % 64
    return jnp.asarray(out, BF16)


def _slope_pieces():
    H, G, HG = NSA_HEADS, NSA_GROUPS, NSA_HG
    slopes = jnp.asarray(LOG2E * 2.0 ** (-8.0 * np.arange(1, H + 1, dtype=np.float64) / H), F32)
    pieces = jnp.stack(_split3(slopes), axis=-1)
    cols = jnp.repeat(pieces, 2, axis=-1)
    cols = jnp.pad(cols, ((0, 0), (0, LANES - cols.shape[-1])))
    return jnp.repeat(cols.reshape(G, HG, 1, LANES), Q_BLOCK, axis=2).reshape(G, HG * Q_BLOCK, LANES)


def _blocked_transpose(v):
    lead, (t, dh) = v.shape[:-2], v.shape[-2:]
    return jnp.swapaxes(v.reshape(*lead, t // LANES, LANES, dh), -1, -2)


def nsa_attention(q, glog, kvc, zz, B, T):
    G, HG, Dh = NSA_GROUPS, NSA_HG, LANES
    H = NSA_HEADS
    n_cmp = T // CMP_STRIDE - CMP_BLOCK // CMP_STRIDE + 1
    n_cmp_pad = -(-T // CMP_STRIDE // LANES) * LANES
    n_slc = T // SLC_BLOCK
    per = SLC_BLOCK // CMP_STRIDE
    n = np.arange(n_cmp_pad)[None, :]
    j = np.arange(n_slc)[:, None]
    mselt = ((n // per == j).astype(np.float32) + ((n + 1) // per == j).astype(np.float32))
    mselt[:, n_cmp:] = 0.0
    kvc = jnp.pad(kvc, ((0, 0), (0, 0), (0, 0), (0, n_cmp_pad - kvc.shape[3]), (0, 0)))
    efullt = (np.arange(T)[:, None] // SLC_BLOCK == np.arange(n_slc)[None, :]).astype(np.float32)

    key_aug = jnp.broadcast_to(_pos_pieces(np.arange(T)), (G, B, T, LANES))
    cmp_aug = jnp.broadcast_to(_pos_pieces(np.arange(n_cmp_pad) * CMP_STRIDE + CMP_BLOCK - 1),
                               (B, G, n_cmp_pad, LANES))
    kc_aug = jnp.concatenate([kvc[0], cmp_aug], axis=-1)
    vct = _blocked_transpose(kvc[1])
    ks_aug = jnp.concatenate([zz[2 * G:3 * G], key_aug], axis=-1)
    kw_aug = jnp.concatenate([zz[4 * G:5 * G], key_aug], axis=-1)
    vst = _blocked_transpose(zz[3 * G:4 * G])
    vwt = _blocked_transpose(zz[5 * G:6 * G])

    def k_spec():
        return pl.BlockSpec((1, 1, T, 2 * Dh), lambda b, g, i: (g, b, 0, 0))

    def vt_spec():
        return pl.BlockSpec((1, 1, T // LANES, Dh, LANES), lambda b, g, i: (g, b, 0, 0, 0))

    n_rb = HG * Q_BLOCK // NSA_ROW_BLOCK
    return pl.pallas_call(
        _nsa_attn_body,
        grid=(B, G, T // Q_BLOCK),
        in_specs=[pl.BlockSpec((1, Q_BLOCK, HG * Dh), lambda b, g, i: (b, i, g)),
                  pl.BlockSpec((1, HG * Q_BLOCK, LANES), lambda b, g, i: (g, 0, 0)),
                  pl.BlockSpec((1, Q_BLOCK, LANES), lambda b, g, i: (b, i, g)),
                  pl.BlockSpec((1, 1, n_cmp_pad, 2 * Dh), lambda b, g, i: (b, g, 0, 0)),
                  pl.BlockSpec((1, 1, n_cmp_pad // LANES, Dh, LANES), lambda b, g, i: (b, g, 0, 0, 0)),
                  k_spec(), vt_spec(), k_spec(), vt_spec(),
                  pl.BlockSpec((n_slc, n_cmp_pad), lambda b, g, i: (0, 0)),
                  pl.BlockSpec((T, n_slc), lambda b, g, i: (0, 0))],
        out_specs=pl.BlockSpec((1, Q_BLOCK, HG * Dh), lambda b, g, i: (b, i, g)),
        out_shape=jax.ShapeDtypeStruct((B, T, H * Dh), BF16),
        scratch_shapes=[pltpu.VMEM((HG * Q_BLOCK, 2 * Dh), BF16),
                        pltpu.VMEM((Dh, HG * Q_BLOCK), F32),
                        pltpu.VMEM((n_rb, 1, NSA_ROW_BLOCK), F32),
                        pltpu.VMEM((n_rb, 1, NSA_ROW_BLOCK), F32),
                        pltpu.VMEM((Dh, HG * Q_BLOCK), F32),
                        pltpu.VMEM((n_rb, SLC_KEY_CHUNK, NSA_ROW_BLOCK), F32),
                        pltpu.VMEM((n_rb, n_cmp_pad, NSA_ROW_BLOCK), F32),
                        pltpu.VMEM((n_rb, WINDOW + Q_BLOCK, NSA_ROW_BLOCK), F32)],
        compiler_params=_params("arbitrary", "arbitrary", "arbitrary"),
        name="nsa_attention",
    )(q, _slope_pieces(), glog, kc_aug, vct, ks_aug, vst, kw_aug, vwt,
      jnp.asarray(mselt, BF16), jnp.asarray(efullt, BF16))


def _pack_rows(y, o_ref):
    m, d = y.shape
    bits = lax.bitcast_convert_type(y.astype(BF16).astype(F32), U32)
    for c in range(d // (2 * LANES)):
        lo = lax.shift_right_logical(bits[:, c * LANES:(c + 1) * LANES], jnp.uint32(16))
        hi = bits[:, d // 2 + c * LANES:d // 2 + (c + 1) * LANES]
        o_ref[pl.ds(c, m, stride=d // (2 * LANES)), :] = lo | hi


def _unpack_chunk(words):
    lo = lax.bitcast_convert_type(lax.shift_left(words, jnp.uint32(16)), F32)
    hi = lax.bitcast_convert_type(words & jnp.uint32(0xFFFF0000), F32)
    return lo, hi


def _moe_router_body(h_ref, g_ref, w_ref, b_ref, xp_ref, idx_ref, wgt_ref):
    x = h_ref[...]
    ms = jnp.mean(x * x, axis=-1, keepdims=True)
    xn = x * lax.rsqrt(ms + NORM_EPS) * g_ref[...]
    _pack_rows(xn, xp_ref)
    logits = jnp.dot(xn.astype(BF16), w_ref[...], preferred_element_type=F32) + b_ref[...]
    lane = lax.broadcasted_iota(I32, logits.shape, 1)
    logits = jnp.where(lane < N_EXPERTS, logits, NEG_INF)
    idx_out = jnp.zeros(logits.shape, I32)
    val_out = jnp.full(logits.shape, NEG_INF, F32)
    for k in range(TOP_K):
        m = jnp.max(logits, axis=-1, keepdims=True)
        first = jnp.min(jnp.where(logits == m, lane, LANES), axis=-1, keepdims=True)
        idx_out = jnp.where(lane == k, first, idx_out)
        val_out = jnp.where(lane == k, m, val_out)
        logits = jnp.where(lane == first, NEG_INF, logits)
    e = jnp.exp(val_out - jnp.max(val_out, axis=-1, keepdims=True))
    idx_ref[...] = idx_out
    wgt_ref[...] = e * (1.0 / jnp.sum(e, axis=-1, keepdims=True))


def moe_router(h, gain, w_router, b_router, tm=256):
    n, d = h.shape
    tm = min(tm, n)
    pr = d // (2 * LANES)
    w = _pad_cols(w_router).astype(BF16)
    b = _pad_cols(b_router.reshape(1, -1)).astype(F32)
    return pl.pallas_call(
        _moe_router_body,
        grid=(n // tm,),
        in_specs=[pl.BlockSpec((tm, d), lambda i: (i, 0)),
                  pl.BlockSpec((1, d), lambda i: (0, 0)),
                  pl.BlockSpec((d, LANES), lambda i: (0, 0)),
                  pl.BlockSpec((1, LANES), lambda i: (0, 0))],
        out_specs=[pl.BlockSpec((tm * pr, LANES), lambda i: (i, 0)),
                   pl.BlockSpec((tm, LANES), lambda i: (i, 0)),
                   pl.BlockSpec((tm, LANES), lambda i: (i, 0))],
        out_shape=[jax.ShapeDtypeStruct((n * pr, LANES), U32),
                   jax.ShapeDtypeStruct((n, LANES), I32), jax.ShapeDtypeStruct((n, LANES), F32)],
        compiler_params=_params("arbitrary"),
        name="moe_router",
    )(h, gain.reshape(1, d).astype(F32), w, b)


def _row_gather_copy(src_hbm, src_row, dst_buf, slot, r, sem, pr):
    return pltpu.make_async_copy(src_hbm.at[pl.ds(pl.multiple_of(src_row, pr), pr)],
                                 dst_buf.at[slot, pl.ds(pl.multiple_of(r * pr, pr), pr)], sem.at[slot])


def _start_row_gather(ids_ref, src_hbm, dst_buf, slot, sem, n_rows, pr):
    def body(r, carry):
        _row_gather_copy(src_hbm, ids_ref[0, 0, r], dst_buf, slot, r, sem, pr).start()
        return carry
    lax.fori_loop(0, n_rows, body, 0, unroll=8)


def _wait_row_gather(src_hbm, dst_buf, slot, sem):
    pltpu.make_async_copy(src_hbm.at[pl.ds(0, dst_buf.shape[1])], dst_buf.at[slot], sem.at[slot]).wait()


def _moe_expert_body(blk_e_ref, n_used_ref, ids0_ref, idsn_ref, x_hbm, wgu_ref, bgu_ref, wd_ref, bd_ref,
                     y_ref, xbuf, xs, sem):
    i = pl.program_id(0)
    n_used = n_used_ref[0]
    rows, d = xs.shape
    pr = d // (2 * LANES)
    ff = wd_ref.shape[2]

    @pl.when(i == 0)
    def _():
        _start_row_gather(ids0_ref, x_hbm, xbuf, 0, sem, rows, pr)

    @pl.when(i + 1 < n_used)
    def _():
        _start_row_gather(idsn_ref, x_hbm, xbuf, (i + 1) % 2, sem, rows, pr)

    @pl.when(i < n_used)
    def _():
        slot = i % 2
        _wait_row_gather(x_hbm, xbuf, slot, sem)
        sub = rows // MOE_SUB_BLOCKS
        for sb in range(MOE_SUB_BLOCKS):
            r0 = sb * sub
            for c in range(pr):
                lo, hi = _unpack_chunk(xbuf[slot, pl.ds(r0 * pr + c, sub, stride=pr), :])
                xs[r0:r0 + sub, c * LANES:(c + 1) * LANES] = lo.astype(BF16)
                xs[r0:r0 + sub, d // 2 + c * LANES:d // 2 + (c + 1) * LANES] = hi.astype(BF16)
            gu = jnp.dot(xs[r0:r0 + sub, :], wgu_ref[0, 0], preferred_element_type=F32) + bgu_ref[0]
            gate = jnp.minimum(gu[:, :ff], SWIGLU_LIMIT)
            up = jnp.clip(gu[:, ff:], -SWIGLU_LIMIT, SWIGLU_LIMIT)
            act = (up + 1.0) * (gate * jax.nn.sigmoid(SWIGLU_ALPHA * gate))
            y = jnp.dot(act.astype(BF16), wd_ref[0, 0], preferred_element_type=F32) + bd_ref[0]
            _pack_rows(y, y_ref.at[pl.ds(r0 * pr, sub * pr), :])

    @pl.when(i >= n_used)
    def _():
        y_ref[...] = jnp.zeros(y_ref.shape, y_ref.dtype)


def moe_experts(xp, tok_rows, blk_e, n_used, w_gate_up, b_gate_up, w_down, b_down, l):
    _, E, d, ff2 = w_gate_up.shape
    ff = ff2 // 2
    pr = d // (2 * LANES)
    rows = MOE_BLOCK_ROWS
    n_blocks = tok_rows.shape[0] // rows
    ids = tok_rows.reshape(n_blocks, 1, rows)

    def used(i, nu):
        return jnp.minimum(i, nu[0] - 1)

    grid_spec = pltpu.PrefetchScalarGridSpec(
        num_scalar_prefetch=2,
        grid=(n_blocks,),
        in_specs=[pl.BlockSpec((1, 1, rows), lambda i, be, nu: (0, 0, 0), memory_space=pltpu.SMEM),
                  pl.BlockSpec((1, 1, rows), lambda i, be, nu: (jnp.minimum(i + 1, n_blocks - 1), 0, 0),
                               memory_space=pltpu.SMEM),
                  pl.BlockSpec(memory_space=pl.ANY),
                  pl.BlockSpec((1, 1, d, ff2), lambda i, be, nu: (l, be[used(i, nu)], 0, 0)),
                  pl.BlockSpec((1, 1, ff2), lambda i, be, nu: (be[used(i, nu)], 0, 0)),
                  pl.BlockSpec((1, 1, ff, d), lambda i, be, nu: (l, be[used(i, nu)], 0, 0)),
                  pl.BlockSpec((1, 1, d), lambda i, be, nu: (be[used(i, nu)], 0, 0))],
        out_specs=pl.BlockSpec((rows * pr, LANES), lambda i, be, nu: (i, 0)),
        scratch_shapes=[pltpu.VMEM((2, rows * pr, LANES), U32), pltpu.VMEM((rows, d), BF16),
                        pltpu.SemaphoreType.DMA((2,))],
    )
    return pl.pallas_call(
        _moe_expert_body,
        grid_spec=grid_spec,
        out_shape=jax.ShapeDtypeStruct((n_blocks * rows * pr, LANES), U32),
        compiler_params=_params("arbitrary"),
        name="moe_experts",
    )(blk_e, n_used, ids, ids, xp, w_gate_up, b_gate_up.reshape(E, 1, ff2).astype(F32),
      w_down, b_down.reshape(E, 1, d).astype(F32))


def _moe_combine_body(ids0_ref, idsn_ref, y_hbm, h_ref, w_ref, g_ref, o_ref, xn_ref, ybuf, sem):
    i = pl.program_id(0)
    n_steps = pl.num_programs(0)
    tt, d = h_ref.shape
    pr = d // (2 * LANES)
    rows = TOP_K * tt

    @pl.when(i == 0)
    def _():
        _start_row_gather(ids0_ref, y_hbm, ybuf, 0, sem, rows, pr)

    @pl.when(i + 1 < n_steps)
    def _():
        _start_row_gather(idsn_ref, y_hbm, ybuf, (i + 1) % 2, sem, rows, pr)

    slot = i % 2
    _wait_row_gather(y_hbm, ybuf, slot, sem)
    w = w_ref[...]
    wk = [jnp.broadcast_to(w[:, k:k + 1], (tt, LANES)) for k in range(TOP_K)]
    ssq = jnp.zeros((tt, LANES), F32)
    for c in range(pr):
        lo_cols = slice(c * LANES, (c + 1) * LANES)
        hi_cols = slice(d // 2 + c * LANES, d // 2 + (c + 1) * LANES)
        acc_lo = h_ref[:, lo_cols]
        acc_hi = h_ref[:, hi_cols]
        for k in range(TOP_K):
            lo, hi = _unpack_chunk(ybuf[slot, pl.ds(k * tt * pr + c, tt, stride=pr), :])
            acc_lo = acc_lo + wk[k] * lo
            acc_hi = acc_hi + wk[k] * hi
        o_ref[:, lo_cols] = acc_lo
        o_ref[:, hi_cols] = acc_hi
        ssq = ssq + acc_lo * acc_lo + acc_hi * acc_hi
    inv = lax.rsqrt(jnp.sum(ssq, axis=-1, keepdims=True) * (1.0 / d) + NORM_EPS)
    xn_ref[...] = (o_ref[...] * inv * g_ref[...]).astype(xn_ref.dtype)


def moe_combine(yp, slot_rows, weights, h, next_gain):
    n, d = h.shape
    pr = d // (2 * LANES)
    tt = min(MOE_COMBINE_TOKENS, n)
    n_steps = n // tt
    ids = slot_rows.reshape(n_steps, tt, TOP_K).transpose(0, 2, 1).reshape(n_steps, 1, TOP_K * tt)
    return pl.pallas_call(
        _moe_combine_body,
        grid=(n_steps,),
        in_specs=[pl.BlockSpec((1, 1, TOP_K * tt), lambda i: (0, 0, 0), memory_space=pltpu.SMEM),
                  pl.BlockSpec((1, 1, TOP_K * tt), lambda i: (jnp.minimum(i + 1, n_steps - 1), 0, 0),
                               memory_space=pltpu.SMEM),
                  pl.BlockSpec(memory_space=pl.ANY),
                  pl.BlockSpec((tt, d), lambda i: (i, 0)),
                  pl.BlockSpec((tt, LANES), lambda i: (i, 0)),
                  pl.BlockSpec((1, d), lambda i: (0, 0))],
        out_specs=[pl.BlockSpec((tt, d), lambda i: (i, 0)),
                   pl.BlockSpec((tt, d), lambda i: (i, 0))],
        out_shape=[jax.ShapeDtypeStruct((n, d), F32), jax.ShapeDtypeStruct((n, d), BF16)],
        scratch_shapes=[pltpu.VMEM((2, TOP_K * tt * pr, LANES), U32), pltpu.SemaphoreType.DMA((2,))],
        compiler_params=_params("arbitrary"),
        name="moe_combine",
    )(ids, ids, yp, h, weights, next_gain.reshape(1, d).astype(F32))


def _moe_plan(top_idx, n_rows):
    E, rows = N_EXPERTS, MOE_BLOCK_ROWS
    flat_e = top_idx.reshape(-1)
    nk = flat_e.shape[0]
    onehot = (flat_e[:, None] == jnp.arange(E, dtype=I32)[None, :]).astype(I32)
    csum = jnp.cumsum(onehot, axis=0)
    rank = jnp.sum(onehot * csum, axis=1) - 1
    sizes = csum[-1]
    nblk = (sizes + rows - 1) // rows
    blk_end = jnp.cumsum(nblk)
    pad_start = (blk_end - nblk) * rows
    slot_dest = pad_start[flat_e] + rank
    tok_buf = jnp.zeros((n_rows,), I32).at[slot_dest].set(jnp.arange(nk, dtype=I32) // TOP_K)
    n_blocks = n_rows // rows
    blk_e = jnp.minimum(jnp.searchsorted(blk_end, jnp.arange(n_blocks, dtype=I32), side='right'), E - 1).astype(I32)
    n_used = blk_end[-1:].astype(I32)
    return slot_dest, tok_buf, blk_e, n_used


def moe_layer(h, gain, w_router, b_router, w_gate_up, b_gate_up, w_down, b_down, l, next_gain):
    n, d = h.shape
    pr = d // (2 * LANES)
    rows = MOE_BLOCK_ROWS
    xp, top_idx, weights = moe_router(h, gain, w_router, b_router)
    n_rows = n * TOP_K + N_EXPERTS * rows
    slot_dest, tok_buf, blk_e, n_used = _moe_plan(top_idx[:, :TOP_K], n_rows)
    yp = moe_experts(xp, tok_buf * pr, blk_e, n_used, w_gate_up, b_gate_up, w_down, b_down, l)
    return moe_combine(yp, slot_dest.reshape(n, TOP_K) * pr, weights, h, next_gain)


def _gla_mixer(h, xn, w_in, w_gate2, b_gate2, out_gain, w_out, l, B, T):
    n_main = w_in.shape[2] - GLA_GATE_RANK
    proj = matmul(xn, w_in, layer=l, n=n_main, out_dtype=BF16)
    a_pad = matmul(xn, _pad_cols(w_in[l, :, n_main:])[None])
    o = gla_core(proj, a_pad, w_gate2, b_gate2, out_gain, B, T)
    return matmul(o, w_out, layer=l, residual=h)


def _nsa_shared_kv(h, kv_gain, w_kv, k_gain, cmp_pe, cmp_w1, cmp_w2, B, T):
    G = NSA_GROUPS
    Dh = LANES
    xn = rmsnorm(h, kv_gain)
    ones = jnp.ones((G * Dh,), F32)
    gain_cols = jnp.concatenate([ones, ones, jnp.tile(k_gain[1], G), ones, jnp.tile(k_gain[2], G), ones])
    norm_groups = [False] * (2 * G) + [True] * G + [False] * G + [True] * G + [False] * G
    zz = matmul_groupnorm(xn, w_kv.astype(BF16)[None], gain_cols, norm_groups, split_out=True, tn=6 * G * Dh)
    kvc = nsa_compress(zz, cmp_pe, cmp_w1, cmp_w2, k_gain[0], B, T)
    return kvc, zz.reshape(6 * G, B, T, Dh)


def _nsa_mixer(h, xn, w_in, q_gain, w_out, kvc, zz, j, B, T):
    H, G, HG, Dh = NSA_HEADS, NSA_GROUPS, NSA_HG, LANES
    q = matmul_groupnorm(xn, w_in, jnp.tile(q_gain, H), [True] * 4, layer=j, n=H * Dh,
                         scale=Dh ** -0.5 * LOG2E, tm=1024, tn=512)
    wg = w_in[j, :, H * Dh:].reshape(-1, G, HG, 3).transpose(0, 1, 3, 2).reshape(-1, G, 3 * HG)
    wg = jnp.pad(wg, ((0, 0), (0, 0), (0, LANES - 3 * HG))).reshape(1, -1, G * LANES)
    glog = matmul(xn, wg)
    o = nsa_attention(q.reshape(B, T, H * Dh), glog.reshape(B, T, G * LANES), kvc, zz, B, T)
    return matmul(o.reshape(B * T, H * Dh), w_out, layer=j, residual=h)


def kernel(x, ln_mix, ln_ffn, a_w_in, a_w_gate2, a_b_gate2, a_out_gain, a_w_out, kv_gain, w_kv, k_gain,
           cmp_pe, cmp_w1, cmp_w2, b_w_in, b_q_gain, b_w_out, w_router, b_router, w_gate_up, b_gate_up,
           w_down, b_down):
    B, T, D = x.shape
    depth = ln_mix.shape[0]
    n_a = a_w_in.shape[0]
    h = x.reshape(B * T, D)
    shared = None
    wgu_bf = w_gate_up.astype(BF16)
    wd_bf = w_down.astype(BF16)
    xn = rmsnorm(h, ln_mix[0])
    for l in range(depth):
        if l < n_a:
            h = _gla_mixer(h, xn, a_w_in, a_w_gate2[l], a_b_gate2[l], a_out_gain[l], a_w_out, l, B, T)
        else:
            if l == n_a:
                shared = _nsa_shared_kv(h, kv_gain, w_kv, k_gain, cmp_pe, cmp_w1, cmp_w2, B, T)
            j = l - n_a
            h = _nsa_mixer(h, xn, b_w_in, b_q_gain[j], b_w_out, *shared, j, B, T)
        h, xn = moe_layer(h, ln_ffn[l], w_router[l], b_router[l], wgu_bf, b_gate_up[l], wd_bf, b_down[l], l,
                          ln_mix[min(l + 1, depth - 1)])
    return h.reshape(B, T, D)
```

```python
import functools
import math

import jax
import jax.numpy as jnp
from jax import lax
import numpy as np
from jax.experimental import pallas as pl
from jax.experimental.pallas import tpu as pltpu

F32 = jnp.float32
BF16 = jnp.bfloat16
I32 = jnp.int32
U32 = jnp.uint32

NORM_EPS = 1e-5
GLA_HEADS = 8
GLA_GATE_RANK = 16
GLA_GATE_TAU = 16.0
NSA_HEADS = 32
NSA_GROUPS = 2
NSA_HG = NSA_HEADS // NSA_GROUPS
CMP_BLOCK = 32
CMP_STRIDE = 16
SLC_BLOCK = 64
SLC_TOPK = 8
WINDOW = 512
Q_BLOCK = 128
N_EXPERTS = 32
TOP_K = 4
SWIGLU_ALPHA = 1.702
SWIGLU_LIMIT = 7.0

LANES = 128
MXU_DIM = 256
VMEM_LIMIT_BYTES = 56 * 1024 * 1024
MOE_BLOCK_ROWS = 512
MOE_SUB_BLOCKS = 4
MOE_COMBINE_TOKENS = 128
SLC_KEY_CHUNK = 512
NSA_ROW_BLOCK = 256
GLA_BLOCK = 128
GLA_STEP_TOKENS = 512
GLA_HEADS_PER_STEP = 2
NEG_INF = float("-inf")
LOG2E = math.log2(math.e)


def _params(*sem):
    return pltpu.CompilerParams(dimension_semantics=sem, vmem_limit_bytes=VMEM_LIMIT_BYTES)


def _split3(x):
    hi = x.astype(BF16)
    r1 = x - hi.astype(F32)
    mid = r1.astype(BF16)
    lo = (r1 - mid.astype(F32)).astype(BF16)
    return hi, mid, lo


def _rmsnorm_body(x_ref, g_ref, o_ref):
    x = x_ref[...].astype(F32)
    ms = jnp.mean(x * x, axis=-1, keepdims=True)
    o_ref[...] = (x * lax.rsqrt(ms + NORM_EPS) * g_ref[...].astype(F32)).astype(o_ref.dtype)


def rmsnorm(x, gain, out_dtype=BF16, tm=512):
    m, d = x.shape
    tm = min(tm, m)
    return pl.pallas_call(
        _rmsnorm_body,
        grid=(m // tm,),
        in_specs=[pl.BlockSpec((tm, d), lambda i: (i, 0)),
                  pl.BlockSpec((1, d), lambda i: (0, 0))],
        out_specs=pl.BlockSpec((tm, d), lambda i: (i, 0)),
        out_shape=jax.ShapeDtypeStruct((m, d), out_dtype),
        compiler_params=_params("arbitrary"),
        name="rmsnorm",
    )(x, gain.reshape(1, d))


def _tile_dot(a_ref, w_ref, w_transposed):
    w = w_ref[0].astype(BF16)
    dims = (((1,), (1,)), ((), ())) if w_transposed else (((1,), (0,)), ((), ()))
    return lax.dot_general(a_ref[...], w, dims, preferred_element_type=F32)


def _w_spec(k, tn, layer, w_transposed):
    if w_transposed:
        return pl.BlockSpec((1, tn, k), lambda i, j: (layer, j, 0))
    return pl.BlockSpec((1, k, tn), lambda i, j: (layer, 0, j))


def _matmul_body(a_ref, w_ref, o_ref, *, w_transposed):
    o_ref[...] = _tile_dot(a_ref, w_ref, w_transposed).astype(o_ref.dtype)


def _matmul_res_body(a_ref, w_ref, r_ref, o_ref, *, w_transposed):
    o_ref[...] = (r_ref[...].astype(F32) + _tile_dot(a_ref, w_ref, w_transposed)).astype(o_ref.dtype)


def matmul(a, w, layer=0, n=None, residual=None, w_transposed=False, out_dtype=F32, tm=1024, tn=512):
    m, k = a.shape
    n = w.shape[1 if w_transposed else 2] if n is None else n
    tm = min(tm, m)
    tn = min(tn, n)
    assert m % tm == 0 and n % tn == 0, (m, n, tm, tn)
    in_specs = [pl.BlockSpec((tm, k), lambda i, j: (i, 0)), _w_spec(k, tn, layer, w_transposed)]
    args = [a, w]
    body = functools.partial(_matmul_body, w_transposed=w_transposed)
    if residual is not None:
        in_specs.append(pl.BlockSpec((tm, tn), lambda i, j: (i, j)))
        args.append(residual)
        body = functools.partial(_matmul_res_body, w_transposed=w_transposed)
    return pl.pallas_call(
        body,
        grid=(m // tm, n // tn),
        in_specs=in_specs,
        out_specs=pl.BlockSpec((tm, tn), lambda i, j: (i, j)),
        out_shape=jax.ShapeDtypeStruct((m, n), out_dtype),
        compiler_params=_params("arbitrary", "arbitrary"),
        name="matmul",
    )(*args)


def _matmul_groupnorm_body(a_ref, w_ref, g_ref, o_ref, *, norm_groups, scale, split_out, w_transposed):
    acc = _tile_dot(a_ref, w_ref, w_transposed)
    for c, do_norm in enumerate(norm_groups):
        seg = acc[:, c * LANES:(c + 1) * LANES]
        if do_norm:
            ms = jnp.mean(seg * seg, axis=-1, keepdims=True)
            seg = seg * lax.rsqrt(ms + NORM_EPS) * g_ref[:, c * LANES:(c + 1) * LANES] * scale
        if split_out:
            o_ref[c] = seg.astype(o_ref.dtype)
        else:
            o_ref[:, c * LANES:(c + 1) * LANES] = seg.astype(o_ref.dtype)


def matmul_groupnorm(a, w, gain_cols, norm_groups, layer=0, n=None, w_transposed=False, scale=1.0, split_out=False,
                     out_dtype=BF16, tm=512, tn=512):
    m, k = a.shape
    n = w.shape[1 if w_transposed else 2] if n is None else n
    tm = min(tm, m)
    tn = min(tn, n)
    assert m % tm == 0 and n % tn == 0 and len(norm_groups) == tn // LANES
    if split_out:
        out_shape = jax.ShapeDtypeStruct((n // LANES, m, LANES), out_dtype)
        out_spec = pl.BlockSpec((tn // LANES, tm, LANES), lambda i, j: (j, i, 0))
    else:
        out_shape = jax.ShapeDtypeStruct((m, n), out_dtype)
        out_spec = pl.BlockSpec((tm, tn), lambda i, j: (i, j))
    body = functools.partial(_matmul_groupnorm_body, norm_groups=tuple(norm_groups), scale=scale,
                             split_out=split_out, w_transposed=w_transposed)
    return pl.pallas_call(
        body,
        grid=(m // tm, n // tn),
        in_specs=[pl.BlockSpec((tm, k), lambda i, j: (i, 0)),
                  _w_spec(k, tn, layer, w_transposed),
                  pl.BlockSpec((1, tn), lambda i, j: (0, j))],
        out_specs=out_spec,
        out_shape=out_shape,
        compiler_params=_params("arbitrary", "arbitrary"),
        name="matmul_groupnorm",
    )(a, w, gain_cols.reshape(1, n).astype(F32))


def _pad_cols(w, mult=LANES):
    pad = (-w.shape[-1]) % mult
    return jnp.pad(w, ((0, 0), (0, pad))) if pad else w


def _gla_body(q_ref, k_ref, v_ref, g_ref, a_ref, wg_ref, bg_ref, og_ref, tri_ref, o_ref, s_ref):
    dk, dv = s_ref.shape[1], s_ref.shape[2]
    C = GLA_BLOCK

    @pl.when(pl.program_id(2) == 0)
    def _():
        s_ref[...] = jnp.zeros(s_ref.shape, F32)

    tri = tri_ref[...]
    row = lax.broadcasted_iota(I32, (C, C), 0)
    col = lax.broadcasted_iota(I32, (C, C), 1)
    for c in range(q_ref.shape[0] // C):
        rows = slice(c * C, (c + 1) * C)
        a = a_ref[rows, :].astype(BF16)
        for hh in range(s_ref.shape[0]):
            kc = slice(hh * dk, (hh + 1) * dk)
            vc = slice(hh * dv, (hh + 1) * dv)
            gate_in = jnp.dot(a, wg_ref[:, kc], preferred_element_type=F32) + bg_ref[:, kc]
            log_a = jax.nn.log_sigmoid(gate_in) * (1.0 / GLA_GATE_TAU)
            hi, mid, lo = _split3(log_a)
            bcum = (jnp.dot(tri, hi, preferred_element_type=F32) + jnp.dot(tri, mid, preferred_element_type=F32)
                    + jnp.dot(tri, lo, preferred_element_type=F32))
            b_mid = bcum[C // 2 - 1:C // 2, :]
            q = q_ref[rows, kc].astype(F32) * dk ** -0.5
            k = k_ref[rows, kc].astype(F32)
            v = v_ref[rows, vc]
            att = lax.dot_general((q * jnp.exp(bcum - b_mid)).astype(BF16),
                                  (k * jnp.exp(b_mid - bcum)).astype(BF16),
                                  (((1,), (1,)), ((), ())), preferred_element_type=F32)
            att = jnp.where(col <= row, att, 0.0)
            o = jnp.dot(att.astype(BF16), v, preferred_element_type=F32)
            o = o + jnp.dot((q * jnp.exp(bcum)).astype(BF16), s_ref[hh].astype(BF16), preferred_element_type=F32)
            bcum_t = bcum.T
            b_last = bcum_t[:, C - 1:C]
            k_t = (k.T * jnp.exp(b_last - bcum_t)).astype(BF16)
            s_ref[hh] = s_ref[hh] * jnp.exp(b_last) + jnp.dot(k_t, v, preferred_element_type=F32)
            ms = jnp.mean(o * o, axis=-1, keepdims=True)
            o = o * lax.rsqrt(ms + NORM_EPS) * og_ref[...]
            o_ref[rows, vc] = (o * jax.nn.silu(g_ref[rows, vc].astype(F32))).astype(o_ref.dtype)


def gla_core(proj, a_pad, w_gate2, b_gate2, out_gain, B, T):
    H = GLA_HEADS
    n = proj.shape[0]
    dv = out_gain.shape[0]
    dk = w_gate2.shape[1] // H
    tb = min(GLA_STEP_TOKENS, T)
    nt = T // tb
    wg = jnp.pad(w_gate2, ((0, LANES - w_gate2.shape[0]), (0, 0))).astype(BF16)
    tri = jnp.asarray(np.tril(np.ones((GLA_BLOCK, GLA_BLOCK), np.float32)), BF16)
    hp = GLA_HEADS_PER_STEP
    hs = H // hp
    v0 = 2 * H * dk // (hp * dv)
    return pl.pallas_call(
        _gla_body,
        grid=(B, hs, nt),
        in_specs=[pl.BlockSpec((tb, hp * dk), lambda b, h, i: (b * nt + i, h)),
                  pl.BlockSpec((tb, hp * dk), lambda b, h, i: (b * nt + i, hs + h)),
                  pl.BlockSpec((tb, hp * dv), lambda b, h, i: (b * nt + i, v0 + h)),
                  pl.BlockSpec((tb, hp * dv), lambda b, h, i: (b * nt + i, v0 + hs + h)),
                  pl.BlockSpec((tb, LANES), lambda b, h, i: (b * nt + i, 0)),
                  pl.BlockSpec((LANES, hp * dk), lambda b, h, i: (0, h)),
                  pl.BlockSpec((1, hp * dk), lambda b, h, i: (0, h)),
                  pl.BlockSpec((1, dv), lambda b, h, i: (0, 0)),
                  pl.BlockSpec((GLA_BLOCK, GLA_BLOCK), lambda b, h, i: (0, 0))],
        out_specs=pl.BlockSpec((tb, hp * dv), lambda b, h, i: (b * nt + i, h)),
        out_shape=jax.ShapeDtypeStruct((n, H * dv), BF16),
        scratch_shapes=[pltpu.VMEM((hp, dk, dv), F32)],
        compiler_params=_params("arbitrary", "arbitrary", "arbitrary"),
        name="gla_core",
    )(proj, proj, proj, proj, a_pad, wg, b_gate2.reshape(1, -1).astype(F32), out_gain.reshape(1, dv).astype(F32), tri)


def _nsa_compress_body(u_ref, w1_ref, w2_ref, pe_ref, kg_ref, o_ref):
    kv = pl.program_id(0)
    u = u_ref[0, 0]
    half = u.shape[1]
    w1 = w1_ref[0]
    a = jnp.dot(u, w1[:half], preferred_element_type=F32)
    b = jnp.dot(u, w1[half:], preferred_element_type=F32)
    pe_term = jnp.dot(pe_ref[0], w1, preferred_element_type=F32)[0:1]
    n_chunks = u.shape[0]
    hid = jax.nn.gelu(a + pltpu.roll(b, n_chunks - 1, 0) + pe_term, approximate=True)
    out = jnp.dot(hid.astype(BF16), w2_ref[0], preferred_element_type=F32)
    ms = jnp.mean(out * out, axis=-1, keepdims=True)
    normed = out * lax.rsqrt(ms + NORM_EPS) * kg_ref[...]
    res = jnp.where(kv == 0, normed, out)
    row = lax.broadcasted_iota(I32, res.shape, 0)
    o_ref[0, 0, 0] = jnp.where(row < n_chunks - 1, res, 0.0).astype(o_ref.dtype)


def nsa_compress(zz, cmp_pe, cmp_w1, cmp_w2, kc_gain, B, T):
    G = NSA_GROUPS
    Dh = zz.shape[-1]
    n_chunks = T // CMP_STRIDE
    u = zz.reshape(zz.shape[0], B, n_chunks, CMP_STRIDE * Dh)
    w1 = cmp_w1.reshape(2, CMP_BLOCK * Dh, Dh).astype(BF16)
    pe = jnp.broadcast_to(cmp_pe.reshape(2, 1, CMP_BLOCK * Dh), (2, 8, CMP_BLOCK * Dh)).astype(BF16)
    return pl.pallas_call(
        _nsa_compress_body,
        grid=(2, B, G),
        in_specs=[pl.BlockSpec((1, 1, n_chunks, CMP_STRIDE * Dh), lambda kv, b, g: (kv * G + g, b, 0, 0)),
                  pl.BlockSpec((1, CMP_BLOCK * Dh, Dh), lambda kv, b, g: (kv, 0, 0)),
                  pl.BlockSpec((1, Dh, Dh), lambda kv, b, g: (kv, 0, 0)),
                  pl.BlockSpec((1, 8, CMP_BLOCK * Dh), lambda kv, b, g: (kv, 0, 0)),
                  pl.BlockSpec((1, Dh), lambda kv, b, g: (0, 0))],
        out_specs=pl.BlockSpec((1, 1, 1, n_chunks, Dh), lambda kv, b, g: (kv, b, g, 0, 0)),
        out_shape=jax.ShapeDtypeStruct((2, B, G, n_chunks, Dh), BF16),
        compiler_params=_params("arbitrary", "arbitrary", "arbitrary"),
        name="nsa_compress",
    )(u, w1, cmp_w2.astype(BF16), pe, kc_gain.reshape(1, Dh).astype(F32))


def _dot_nt(a, b):
    return lax.dot_general(a, b, (((1,), (1,)), ((), ())), preferred_element_type=F32)


def _col_softmax_terms(s):
    m = jnp.max(s, axis=0, keepdims=True)
    m = jnp.where(m == NEG_INF, 0.0, m)
    e = jnp.exp2(s - m)
    return e, 1.0 / jnp.maximum(jnp.sum(e, axis=0, keepdims=True), 1e-30)


def _tile_lanes(x, n):
    return jnp.concatenate([x] * n, axis=1)


def _nsa_attn_body(q_ref, qaug_ref, glog_ref, kc_ref, vct_ref, ks_ref, vst_ref, kw_ref, vwt_ref,
                   mselt_ref, efullt_ref, o_ref, q2_ref, acct_ref, m_ref, l_ref, oacct_ref, s_ref, sc_ref, sw_ref,
                   sel_ref):
    qb = pl.program_id(2)
    Dh = LANES
    RB = NSA_ROW_BLOCK
    hpb = RB // Q_BLOCK
    n_rb = NSA_HG // hpb
    n_cmp_pad = kc_ref.shape[2]
    n_slc = mselt_ref.shape[0]
    win_keys = WINDOW + Q_BLOCK
    s0 = qb * Q_BLOCK
    t_q = s0 + lax.broadcasted_iota(I32, (1, Q_BLOCK), 1)

    gates_t = jax.nn.sigmoid(glog_ref[0]).T

    for hg in range(NSA_HG):
        q2_ref[hg * Q_BLOCK:(hg + 1) * Q_BLOCK, :Dh] = q_ref[0, :, hg * Dh:(hg + 1) * Dh]
    q2_ref[:, Dh:] = qaug_ref[0]

    cmp_end = lax.broadcasted_iota(I32, (n_cmp_pad, 1), 0) * CMP_STRIDE + (CMP_BLOCK - 1)
    cmp_mask = _tile_lanes(jnp.where(cmp_end <= t_q, 0.0, NEG_INF), hpb)

    ws = pl.multiple_of(jnp.maximum(s0 - WINDOW, 0), Q_BLOCK)
    wdist = t_q - (ws + lax.broadcasted_iota(I32, (win_keys, 1), 0))
    win_mask = _tile_lanes(jnp.where((wdist >= 0) & (wdist < WINDOW), 0.0, NEG_INF), hpb)

    kc = kc_ref[0, 0]
    vct = jnp.concatenate([vct_ref[0, 0, j] for j in range(n_cmp_pad // Dh)], axis=1)
    kwin = kw_ref[0, 0, pl.ds(ws, win_keys), :]
    wblk = ws // Dh
    vwt = jnp.concatenate([vwt_ref[0, 0, wblk + j] for j in range(win_keys // Dh)], axis=1)

    imp_t = jnp.zeros((n_cmp_pad, Q_BLOCK), F32)
    for rb in range(n_rb):
        q_rb = q2_ref[rb * RB:(rb + 1) * RB, :]
        sc_ref[rb] = _dot_nt(kc, q_rb) + cmp_mask
        sw_ref[rb] = _dot_nt(kwin, q_rb) + win_mask
    for rb in range(n_rb):
        e, inv = _col_softmax_terms(sc_ref[rb])
        p = e * inv
        for j in range(hpb):
            imp_t = imp_t + p[:, j * Q_BLOCK:(j + 1) * Q_BLOCK]
        o_c = jnp.dot(vct, p.astype(BF16), preferred_element_type=F32)
        e, inv = _col_softmax_terms(sw_ref[rb])
        o_w = jnp.dot(vwt, e.astype(BF16), preferred_element_type=F32) * inv
        for j in range(hpb):
            hg = rb * hpb + j
            sub = slice(j * Q_BLOCK, (j + 1) * Q_BLOCK)
            acct_ref[:, hg * Q_BLOCK:(hg + 1) * Q_BLOCK] = (
                gates_t[hg:hg + 1, :] * o_c[:, sub] + gates_t[2 * NSA_HG + hg:2 * NSA_HG + hg + 1, :] * o_w[:, sub])

    hi, mid, lo = _split3(imp_t)
    mselt = mselt_ref[...]
    slc_imp = (jnp.dot(mselt, hi, preferred_element_type=F32) + jnp.dot(mselt, mid, preferred_element_type=F32)
               + jnp.dot(mselt, lo, preferred_element_type=F32))

    blk = lax.broadcasted_iota(I32, (n_slc, 1), 0)
    cur = lax.shift_right_logical(t_q, 6)
    forced = (blk == 0) | (blk == cur) | (blk == cur - 1)
    score = jnp.where(forced, jnp.inf, jnp.where(blk <= cur, slc_imp, NEG_INF))
    sel = jnp.zeros((n_slc, Q_BLOCK), F32)
    for _ in range(SLC_TOPK):
        m = jnp.max(score, axis=0, keepdims=True)
        cand = (score == m) & (m > NEG_INF)
        first = jnp.min(jnp.where(cand, blk, n_slc), axis=0, keepdims=True)
        one = blk == first
        sel = jnp.where(one, 1.0, sel)
        score = jnp.where(one, NEG_INF, score)
    sel_b = sel.astype(BF16)
    sel_ref[...] = sel

    m_ref[...] = jnp.full(m_ref.shape, NEG_INF, F32)
    l_ref[...] = jnp.zeros(l_ref.shape, F32)
    oacct_ref[...] = jnp.zeros(oacct_ref.shape, F32)
    KC = SLC_KEY_CHUNK
    key_iota = lax.broadcasted_iota(I32, (KC, 1), 0)

    def chunk_step(c, carry):
        k0 = pl.multiple_of(c * KC, KC)
        bpc = KC // SLC_BLOCK
        picked = sel_ref[pl.ds(pl.multiple_of(c * bpc, bpc), bpc), :]
        any_picked = jnp.max(jnp.max(picked, axis=0, keepdims=True), axis=1, keepdims=True)[0, 0]

        @pl.when(any_picked > 0.5)
        def _():
            kch = ks_ref[0, 0, pl.ds(k0, KC), :]
            vt = jnp.concatenate([vst_ref[0, 0, c * (KC // Dh) + j] for j in range(KC // Dh)], axis=1)
            selk = jnp.dot(efullt_ref[pl.ds(k0, KC), :], sel_b, preferred_element_type=F32)
            mask = _tile_lanes(jnp.where((selk > 0.5) & (k0 + key_iota <= t_q), 0.0, NEG_INF), hpb)
            for rb in range(n_rb):
                s_ref[rb] = _dot_nt(kch, q2_ref[rb * RB:(rb + 1) * RB, :]) + mask
            for rb in range(n_rb):
                cols = slice(rb * RB, (rb + 1) * RB)
                s = s_ref[rb]
                m_old = m_ref[rb]
                m_new = jnp.maximum(m_old, jnp.max(s, axis=0, keepdims=True))
                m_safe = jnp.where(m_new == NEG_INF, 0.0, m_new)
                alpha = jnp.exp2(m_old - m_safe)
                p = jnp.exp2(s - m_safe)
                l_ref[rb] = alpha * l_ref[rb] + jnp.sum(p, axis=0, keepdims=True)
                oacct_ref[:, cols] = (alpha * oacct_ref[:, cols]
                                      + jnp.dot(vt, p.astype(BF16), preferred_element_type=F32))
                m_ref[rb] = m_new
        return carry

    n_chunks = (s0 + Q_BLOCK + KC - 1) // KC
    lax.fori_loop(0, n_chunks, chunk_step, 0)
    for hg in range(NSA_HG):
        rb, j = divmod(hg, hpb)
        cols = slice(hg * Q_BLOCK, (hg + 1) * Q_BLOCK)
        inv = 1.0 / jnp.maximum(l_ref[rb][:, j * Q_BLOCK:(j + 1) * Q_BLOCK], 1e-30)
        o_t = acct_ref[:, cols] + gates_t[NSA_HG + hg:NSA_HG + hg + 1, :] * (oacct_ref[:, cols] * inv)
        o_ref[0, :, hg * Dh:(hg + 1) * Dh] = o_t.T.astype(o_ref.dtype)


def _pos_pieces(pos):
    pos = np.asarray(pos)
    out = np.zeros((pos.shape[0], LANES), np.float32)
    for i in range(3):
        out[:, 2 * i] = 64 * (pos // 64)
        out[:, 2 * i + 1] = pos % 64
    return jnp.asarray(out, BF16)


def _slope_pieces():
    H, G, HG = NSA_HEADS, NSA_GROUPS, NSA_HG
    slopes = jnp.asarray(LOG2E * 2.0 ** (-8.0 * np.arange(1, H + 1, dtype=np.float64) / H), F32)
    pieces = jnp.stack(_split3(slopes), axis=-1)
    cols = jnp.repeat(pieces, 2, axis=-1)
    cols = jnp.pad(cols, ((0, 0), (0, LANES - cols.shape[-1])))
    return jnp.repeat(cols.reshape(G, HG, 1, LANES), Q_BLOCK, axis=2).reshape(G, HG * Q_BLOCK, LANES)


def _blocked_transpose(v):
    lead, (t, dh) = v.shape[:-2], v.shape[-2:]
    return jnp.swapaxes(v.reshape(*lead, t // LANES, LANES, dh), -1, -2)


def nsa_attention(q, glog, kvc, zz, B, T):
    G, HG, Dh = NSA_GROUPS, NSA_HG, LANES
    H = NSA_HEADS
    n_cmp = T // CMP_STRIDE - CMP_BLOCK // CMP_STRIDE + 1
    n_cmp_pad = -(-T // CMP_STRIDE // LANES) * LANES
    n_slc = T // SLC_BLOCK
    per = SLC_BLOCK // CMP_STRIDE
    n = np.arange(n_cmp_pad)[None, :]
    j = np.arange(n_slc)[:, None]
    mselt = ((n // per == j).astype(np.float32) + ((n + 1) // per == j).astype(np.float32))
    mselt[:, n_cmp:] = 0.0
    kvc = jnp.pad(kvc, ((0, 0), (0, 0), (0, 0), (0, n_cmp_pad - kvc.shape[3]), (0, 0)))
    efullt = (np.arange(T)[:, None] // SLC_BLOCK == np.arange(n_slc)[None, :]).astype(np.float32)

    key_aug = jnp.broadcast_to(_pos_pieces(np.arange(T)), (G, B, T, LANES))
    cmp_aug = jnp.broadcast_to(_pos_pieces(np.arange(n_cmp_pad) * CMP_STRIDE + CMP_BLOCK - 1),
                               (B, G, n_cmp_pad, LANES))
    kc_aug = jnp.concatenate([kvc[0], cmp_aug], axis=-1)
    vct = _blocked_transpose(kvc[1])
    ks_aug = jnp.concatenate([zz[2 * G:3 * G], key_aug], axis=-1)
    kw_aug = jnp.concatenate([zz[4 * G:5 * G], key_aug], axis=-1)
    vst = _blocked_transpose(zz[3 * G:4 * G])
    vwt = _blocked_transpose(zz[5 * G:6 * G])

    def k_spec():
        return pl.BlockSpec((1, 1, T, 2 * Dh), lambda b, g, i: (g, b, 0, 0))

    def vt_spec():
        return pl.BlockSpec((1, 1, T // LANES, Dh, LANES), lambda b, g, i: (g, b, 0, 0, 0))

    n_rb = HG * Q_BLOCK // NSA_ROW_BLOCK
    return pl.pallas_call(
        _nsa_attn_body,
        grid=(B, G, T // Q_BLOCK),
        in_specs=[pl.BlockSpec((1, Q_BLOCK, HG * Dh), lambda b, g, i: (b, i, g)),
                  pl.BlockSpec((1, HG * Q_BLOCK, LANES), lambda b, g, i: (g, 0, 0)),
                  pl.BlockSpec((1, Q_BLOCK, LANES), lambda b, g, i: (b, i, g)),
                  pl.BlockSpec((1, 1, n_cmp_pad, 2 * Dh), lambda b, g, i: (b, g, 0, 0)),
                  pl.BlockSpec((1, 1, n_cmp_pad // LANES, Dh, LANES), lambda b, g, i: (b, g, 0, 0, 0)),
                  k_spec(), vt_spec(), k_spec(), vt_spec(),
                  pl.BlockSpec((n_slc, n_cmp_pad), lambda b, g, i: (0, 0)),
                  pl.BlockSpec((T, n_slc), lambda b, g, i: (0, 0))],
        out_specs=pl.BlockSpec((1, Q_BLOCK, HG * Dh), lambda b, g, i: (b, i, g)),
        out_shape=jax.ShapeDtypeStruct((B, T, H * Dh), BF16),
        scratch_shapes=[pltpu.VMEM((HG * Q_BLOCK, 2 * Dh), BF16),
                        pltpu.VMEM((Dh, HG * Q_BLOCK), F32),
                        pltpu.VMEM((n_rb, 1, NSA_ROW_BLOCK), F32),
                        pltpu.VMEM((n_rb, 1, NSA_ROW_BLOCK), F32),
                        pltpu.VMEM((Dh, HG * Q_BLOCK), F32),
                        pltpu.VMEM((n_rb, SLC_KEY_CHUNK, NSA_ROW_BLOCK), F32),
                        pltpu.VMEM((n_rb, n_cmp_pad, NSA_ROW_BLOCK), F32),
                        pltpu.VMEM((n_rb, WINDOW + Q_BLOCK, NSA_ROW_BLOCK), F32),
                        pltpu.VMEM((n_slc, Q_BLOCK), F32)],
        compiler_params=_params("arbitrary", "arbitrary", "arbitrary"),
        name="nsa_attention",
    )(q, _slope_pieces(), glog, kc_aug, vct, ks_aug, vst, kw_aug, vwt,
      jnp.asarray(mselt, BF16), jnp.asarray(efullt, BF16))


def _pack_rows(y, o_ref):
    m, d = y.shape
    bits = lax.bitcast_convert_type(y.astype(BF16).astype(F32), U32)
    for c in range(d // (2 * LANES)):
        lo = lax.shift_right_logical(bits[:, c * LANES:(c + 1) * LANES], jnp.uint32(16))
        hi = bits[:, d // 2 + c * LANES:d // 2 + (c + 1) * LANES]
        o_ref[pl.ds(c, m, stride=d // (2 * LANES)), :] = lo | hi


def _unpack_chunk(words):
    lo = lax.bitcast_convert_type(lax.shift_left(words, jnp.uint32(16)), F32)
    hi = lax.bitcast_convert_type(words & jnp.uint32(0xFFFF0000), F32)
    return lo, hi


def _moe_router_body(h_ref, g_ref, w_ref, b_ref, xp_ref, idx_ref, wgt_ref):
    x = h_ref[...]
    ms = jnp.mean(x * x, axis=-1, keepdims=True)
    xn = x * lax.rsqrt(ms + NORM_EPS) * g_ref[...]
    _pack_rows(xn, xp_ref)
    logits = jnp.dot(xn.astype(BF16), w_ref[...], preferred_element_type=F32) + b_ref[...]
    lane = lax.broadcasted_iota(I32, logits.shape, 1)
    logits = jnp.where(lane < N_EXPERTS, logits, NEG_INF)
    idx_out = jnp.zeros(logits.shape, I32)
    val_out = jnp.full(logits.shape, NEG_INF, F32)
    for k in range(TOP_K):
        m = jnp.max(logits, axis=-1, keepdims=True)
        first = jnp.min(jnp.where(logits == m, lane, LANES), axis=-1, keepdims=True)
        idx_out = jnp.where(lane == k, first, idx_out)
        val_out = jnp.where(lane == k, m, val_out)
        logits = jnp.where(lane == first, NEG_INF, logits)
    e = jnp.exp(val_out - jnp.max(val_out, axis=-1, keepdims=True))
    idx_ref[...] = idx_out
    wgt_ref[...] = e * (1.0 / jnp.sum(e, axis=-1, keepdims=True))


def moe_router(h, gain, w_router, b_router, tm=256):
    n, d = h.shape
    tm = min(tm, n)
    pr = d // (2 * LANES)
    w = _pad_cols(w_router).astype(BF16)
    b = _pad_cols(b_router.reshape(1, -1)).astype(F32)
    return pl.pallas_call(
        _moe_router_body,
        grid=(n // tm,),
        in_specs=[pl.BlockSpec((tm, d), lambda i: (i, 0)),
                  pl.BlockSpec((1, d), lambda i: (0, 0)),
                  pl.BlockSpec((d, LANES), lambda i: (0, 0)),
                  pl.BlockSpec((1, LANES), lambda i: (0, 0))],
        out_specs=[pl.BlockSpec((tm * pr, LANES), lambda i: (i, 0)),
                   pl.BlockSpec((tm, LANES), lambda i: (i, 0)),
                   pl.BlockSpec((tm, LANES), lambda i: (i, 0))],
        out_shape=[jax.ShapeDtypeStruct((n * pr, LANES), U32),
                   jax.ShapeDtypeStruct((n, LANES), I32), jax.ShapeDtypeStruct((n, LANES), F32)],
        compiler_params=_params("arbitrary"),
        name="moe_router",
    )(h, gain.reshape(1, d).astype(F32), w, b)


def _row_gather_copy(src_hbm, src_row, dst_buf, slot, r, sem, pr):
    return pltpu.make_async_copy(src_hbm.at[pl.ds(pl.multiple_of(src_row, pr), pr)],
                                 dst_buf.at[slot, pl.ds(pl.multiple_of(r * pr, pr), pr)], sem.at[slot])


def _start_row_gather(ids_ref, src_hbm, dst_buf, slot, sem, n_rows, pr):
    def body(r, carry):
        _row_gather_copy(src_hbm, ids_ref[0, 0, r], dst_buf, slot, r, sem, pr).start()
        return carry
    lax.fori_loop(0, n_rows, body, 0, unroll=8)


def _wait_row_gather(src_hbm, dst_buf, slot, sem):
    pltpu.make_async_copy(src_hbm.at[pl.ds(0, dst_buf.shape[1])], dst_buf.at[slot], sem.at[slot]).wait()


def _moe_expert_body(blk_e_ref, n_used_ref, ids0_ref, idsn_ref, x_hbm, wgu_ref, bgu_ref, wd_ref, bd_ref,
                     y_ref, xbuf, xs, sem):
    i = pl.program_id(0)
    n_used = n_used_ref[0]
    rows, d = xs.shape
    pr = d // (2 * LANES)
    ff = wd_ref.shape[2]

    @pl.when(i == 0)
    def _():
        _start_row_gather(ids0_ref, x_hbm, xbuf, 0, sem, rows, pr)

    @pl.when(i + 1 < n_used)
    def _():
        _start_row_gather(idsn_ref, x_hbm, xbuf, (i + 1) % 2, sem, rows, pr)

    @pl.when(i < n_used)
    def _():
        slot = i % 2
        _wait_row_gather(x_hbm, xbuf, slot, sem)
        sub = rows // MOE_SUB_BLOCKS
        for sb in range(MOE_SUB_BLOCKS):
            r0 = sb * sub
            for c in range(pr):
                lo, hi = _unpack_chunk(xbuf[slot, pl.ds(r0 * pr + c, sub, stride=pr), :])
                xs[r0:r0 + sub, c * LANES:(c + 1) * LANES] = lo.astype(BF16)
                xs[r0:r0 + sub, d // 2 + c * LANES:d // 2 + (c + 1) * LANES] = hi.astype(BF16)
            gu = jnp.dot(xs[r0:r0 + sub, :], wgu_ref[0, 0], preferred_element_type=F32) + bgu_ref[0]
            gate = jnp.minimum(gu[:, :ff], SWIGLU_LIMIT)
            up = jnp.clip(gu[:, ff:], -SWIGLU_LIMIT, SWIGLU_LIMIT)
            act = (up + 1.0) * (gate * jax.nn.sigmoid(SWIGLU_ALPHA * gate))
            y = jnp.dot(act.astype(BF16), wd_ref[0, 0], preferred_element_type=F32) + bd_ref[0]
            _pack_rows(y, y_ref.at[pl.ds(r0 * pr, sub * pr), :])

    @pl.when(i >= n_used)
    def _():
        y_ref[...] = jnp.zeros(y_ref.shape, y_ref.dtype)


def moe_experts(xp, tok_rows, blk_e, n_used, w_gate_up, b_gate_up, w_down, b_down, l):
    _, E, d, ff2 = w_gate_up.shape
    ff = ff2 // 2
    pr = d // (2 * LANES)
    rows = MOE_BLOCK_ROWS
    n_blocks = tok_rows.shape[0] // rows
    ids = tok_rows.reshape(n_blocks, 1, rows)

    def used(i, nu):
        return jnp.minimum(i, nu[0] - 1)

    grid_spec = pltpu.PrefetchScalarGridSpec(
        num_scalar_prefetch=2,
        grid=(n_blocks,),
        in_specs=[pl.BlockSpec((1, 1, rows), lambda i, be, nu: (0, 0, 0), memory_space=pltpu.SMEM),
                  pl.BlockSpec((1, 1, rows), lambda i, be, nu: (jnp.minimum(i + 1, n_blocks - 1), 0, 0),
                               memory_space=pltpu.SMEM),
                  pl.BlockSpec(memory_space=pl.ANY),
                  pl.BlockSpec((1, 1, d, ff2), lambda i, be, nu: (l, be[used(i, nu)], 0, 0)),
                  pl.BlockSpec((1, 1, ff2), lambda i, be, nu: (be[used(i, nu)], 0, 0)),
                  pl.BlockSpec((1, 1, ff, d), lambda i, be, nu: (l, be[used(i, nu)], 0, 0)),
                  pl.BlockSpec((1, 1, d), lambda i, be, nu: (be[used(i, nu)], 0, 0))],
        out_specs=pl.BlockSpec((rows * pr, LANES), lambda i, be, nu: (i, 0)),
        scratch_shapes=[pltpu.VMEM((2, rows * pr, LANES), U32), pltpu.VMEM((rows, d), BF16),
                        pltpu.SemaphoreType.DMA((2,))],
    )
    return pl.pallas_call(
        _moe_expert_body,
        grid_spec=grid_spec,
        out_shape=jax.ShapeDtypeStruct((n_blocks * rows * pr, LANES), U32),
        compiler_params=_params("arbitrary"),
        name="moe_experts",
    )(blk_e, n_used, ids, ids, xp, w_gate_up, b_gate_up.reshape(E, 1, ff2).astype(F32),
      w_down, b_down.reshape(E, 1, d).astype(F32))


def _moe_combine_body(ids0_ref, idsn_ref, y_hbm, h_ref, w_ref, g_ref, o_ref, xn_ref, ybuf, sem):
    i = pl.program_id(0)
    n_steps = pl.num_programs(0)
    tt, d = h_ref.shape
    pr = d // (2 * LANES)
    rows = TOP_K * tt

    @pl.when(i == 0)
    def _():
        _start_row_gather(ids0_ref, y_hbm, ybuf, 0, sem, rows, pr)

    @pl.when(i + 1 < n_steps)
    def _():
        _start_row_gather(idsn_ref, y_hbm, ybuf, (i + 1) % 2, sem, rows, pr)

    slot = i % 2
    _wait_row_gather(y_hbm, ybuf, slot, sem)
    w = w_ref[...]
    wk = [jnp.broadcast_to(w[:, k:k + 1], (tt, LANES)) for k in range(TOP_K)]
    ssq = jnp.zeros((tt, LANES), F32)
    for c in range(pr):
        lo_cols = slice(c * LANES, (c + 1) * LANES)
        hi_cols = slice(d // 2 + c * LANES, d // 2 + (c + 1) * LANES)
        acc_lo = h_ref[:, lo_cols]
        acc_hi = h_ref[:, hi_cols]
        for k in range(TOP_K):
            lo, hi = _unpack_chunk(ybuf[slot, pl.ds(k * tt * pr + c, tt, stride=pr), :])
            acc_lo = acc_lo + wk[k] * lo
            acc_hi = acc_hi + wk[k] * hi
        o_ref[:, lo_cols] = acc_lo
        o_ref[:, hi_cols] = acc_hi
        ssq = ssq + acc_lo * acc_lo + acc_hi * acc_hi
    inv = lax.rsqrt(jnp.sum(ssq, axis=-1, keepdims=True) * (1.0 / d) + NORM_EPS)
    xn_ref[...] = (o_ref[...] * inv * g_ref[...]).astype(xn_ref.dtype)


def moe_combine(yp, slot_rows, weights, h, next_gain):
    n, d = h.shape
    pr = d // (2 * LANES)
    tt = min(MOE_COMBINE_TOKENS, n)
    n_steps = n // tt
    ids = slot_rows.reshape(n_steps, tt, TOP_K).transpose(0, 2, 1).reshape(n_steps, 1, TOP_K * tt)
    return pl.pallas_call(
        _moe_combine_body,
        grid=(n_steps,),
        in_specs=[pl.BlockSpec((1, 1, TOP_K * tt), lambda i: (0, 0, 0), memory_space=pltpu.SMEM),
                  pl.BlockSpec((1, 1, TOP_K * tt), lambda i: (jnp.minimum(i + 1, n_steps - 1), 0, 0),
                               memory_space=pltpu.SMEM),
                  pl.BlockSpec(memory_space=pl.ANY),
                  pl.BlockSpec((tt, d), lambda i: (i, 0)),
                  pl.BlockSpec((tt, LANES), lambda i: (i, 0)),
                  pl.BlockSpec((1, d), lambda i: (0, 0))],
        out_specs=[pl.BlockSpec((tt, d), lambda i: (i, 0)),
                   pl.BlockSpec((tt, d), lambda i: (i, 0))],
        out_shape=[jax.ShapeDtypeStruct((n, d), F32), jax.ShapeDtypeStruct((n, d), BF16)],
        scratch_shapes=[pltpu.VMEM((2, TOP_K * tt * pr, LANES), U32), pltpu.SemaphoreType.DMA((2,))],
        compiler_params=_params("arbitrary"),
        name="moe_combine",
    )(ids, ids, yp, h, weights, next_gain.reshape(1, d).astype(F32))


def _moe_plan(top_idx, n_rows):
    E, rows = N_EXPERTS, MOE_BLOCK_ROWS
    flat_e = top_idx.reshape(-1)
    nk = flat_e.shape[0]
    onehot = (flat_e[:, None] == jnp.arange(E, dtype=I32)[None, :]).astype(I32)
    csum = jnp.cumsum(onehot, axis=0)
    rank = jnp.sum(onehot * csum, axis=1) - 1
    sizes = csum[-1]
    nblk = (sizes + rows - 1) // rows
    blk_end = jnp.cumsum(nblk)
    pad_start = (blk_end - nblk) * rows
    slot_dest = pad_start[flat_e] + rank
    tok_buf = jnp.zeros((n_rows,), I32).at[slot_dest].set(jnp.arange(nk, dtype=I32) // TOP_K)
    n_blocks = n_rows // rows
    blk_e = jnp.minimum(jnp.searchsorted(blk_end, jnp.arange(n_blocks, dtype=I32), side='right'), E - 1).astype(I32)
    n_used = blk_end[-1:].astype(I32)
    return slot_dest, tok_buf, blk_e, n_used


def moe_layer(h, gain, w_router, b_router, w_gate_up, b_gate_up, w_down, b_down, l, next_gain):
    n, d = h.shape
    pr = d // (2 * LANES)
    rows = MOE_BLOCK_ROWS
    xp, top_idx, weights = moe_router(h, gain, w_router, b_router)
    n_rows = n * TOP_K + N_EXPERTS * rows
    slot_dest, tok_buf, blk_e, n_used = _moe_plan(top_idx[:, :TOP_K], n_rows)
    yp = moe_experts(xp, tok_buf * pr, blk_e, n_used, w_gate_up, b_gate_up, w_down, b_down, l)
    return moe_combine(yp, slot_dest.reshape(n, TOP_K) * pr, weights, h, next_gain)


def _gla_mixer(h, xn, w_in, w_gate2, b_gate2, out_gain, w_out, l, B, T):
    n_main = w_in.shape[2] - GLA_GATE_RANK
    w_in_t = jnp.swapaxes(w_in, 1, 2)
    proj = matmul(xn, w_in_t, layer=l, n=n_main, w_transposed=True, out_dtype=BF16)
    wa_t = jnp.pad(w_in_t[l, n_main:], ((0, LANES - GLA_GATE_RANK), (0, 0)))
    a_pad = matmul(xn, wa_t[None], w_transposed=True)
    o = gla_core(proj, a_pad, w_gate2, b_gate2, out_gain, B, T)
    return matmul(o, w_out, layer=l, residual=h)


def _nsa_shared_kv(h, kv_gain, w_kv, k_gain, cmp_pe, cmp_w1, cmp_w2, B, T):
    G = NSA_GROUPS
    Dh = LANES
    xn = rmsnorm(h, kv_gain)
    ones = jnp.ones((G * Dh,), F32)
    gain_cols = jnp.concatenate([ones, ones, jnp.tile(k_gain[1], G), ones, jnp.tile(k_gain[2], G), ones])
    norm_groups = [False] * (2 * G) + [True] * G + [False] * G + [True] * G + [False] * G
    zz = matmul_groupnorm(xn, w_kv.astype(BF16)[None], gain_cols, norm_groups, split_out=True, tn=6 * G * Dh)
    kvc = nsa_compress(zz, cmp_pe, cmp_w1, cmp_w2, k_gain[0], B, T)
    return kvc, zz.reshape(6 * G, B, T, Dh)


def _nsa_mixer(h, xn, w_in, q_gain, w_out, kvc, zz, j, B, T):
    H, G, HG, Dh = NSA_HEADS, NSA_GROUPS, NSA_HG, LANES
    w_in_t = jnp.swapaxes(w_in, 1, 2)
    q = matmul_groupnorm(xn, w_in_t, jnp.tile(q_gain, H), [True] * 4, layer=j, n=H * Dh, w_transposed=True,
                         scale=Dh ** -0.5 * LOG2E, tm=1024, tn=512)
    wg = w_in_t[j, H * Dh:].reshape(G, HG, 3, -1).transpose(0, 2, 1, 3).reshape(G, 3 * HG, -1)
    wg = jnp.pad(wg, ((0, 0), (0, LANES - 3 * HG), (0, 0))).reshape(1, G * LANES, -1)
    glog = matmul(xn, wg, w_transposed=True)
    o = nsa_attention(q.reshape(B, T, H * Dh), glog.reshape(B, T, G * LANES), kvc, zz, B, T)
    return matmul(o.reshape(B * T, H * Dh), w_out, layer=j, residual=h)


def kernel(x, ln_mix, ln_ffn, a_w_in, a_w_gate2, a_b_gate2, a_out_gain, a_w_out, kv_gain, w_kv, k_gain,
           cmp_pe, cmp_w1, cmp_w2, b_w_in, b_q_gain, b_w_out, w_router, b_router, w_gate_up, b_gate_up,
           w_down, b_down):
    B, T, D = x.shape
    depth = ln_mix.shape[0]
    n_a = a_w_in.shape[0]
    h = x.reshape(B * T, D)
    shared = None
    wgu_bf = w_gate_up.astype(BF16)
    wd_bf = w_down.astype(BF16)
    xn = rmsnorm(h, ln_mix[0])
    for l in range(depth):
        if l < n_a:
            h = _gla_mixer(h, xn, a_w_in, a_w_gate2[l], a_b_gate2[l], a_out_gain[l], a_w_out, l, B, T)
        else:
            if l == n_a:
                shared = _nsa_shared_kv(h, kv_gain, w_kv, k_gain, cmp_pe, cmp_w1, cmp_w2, B, T)
            j = l - n_a
            h = _nsa_mixer(h, xn, b_w_in, b_q_gain[j], b_w_out, *shared, j, B, T)
        h, xn = moe_layer(h, ln_ffn[l], w_router[l], b_router[l], wgu_bf, b_gate_up[l], wd_bf, b_down[l], l,
                          ln_mix[min(l + 1, depth - 1)])
    return h.reshape(B, T, D)
```

```python
import functools
import math

import jax
import jax.numpy as jnp
from jax import lax
import numpy as np
from jax.experimental import pallas as pl
from jax.experimental.pallas import tpu as pltpu

F32 = jnp.float32
BF16 = jnp.bfloat16
I32 = jnp.int32
U32 = jnp.uint32

NORM_EPS = 1e-5
GLA_HEADS = 8
GLA_GATE_RANK = 16
GLA_GATE_TAU = 16.0
NSA_HEADS = 32
NSA_GROUPS = 2
NSA_HG = NSA_HEADS // NSA_GROUPS
CMP_BLOCK = 32
CMP_STRIDE = 16
SLC_BLOCK = 64
SLC_TOPK = 8
WINDOW = 512
Q_BLOCK = 128
N_EXPERTS = 32
TOP_K = 4
SWIGLU_ALPHA = 1.702
SWIGLU_LIMIT = 7.0

LANES = 128
MXU_DIM = 256
VMEM_LIMIT_BYTES = 56 * 1024 * 1024
MOE_BLOCK_ROWS = 512
MOE_SUB_BLOCKS = 1
MOE_COMBINE_TOKENS = 128
SLC_KEY_CHUNK = 512
NSA_ROW_BLOCK = 256
GLA_BLOCK = 128
GLA_STEP_TOKENS = 512
GLA_HEADS_PER_STEP = 2
NEG_INF = float("-inf")
LOG2E = math.log2(math.e)


def _params(*sem):
    return pltpu.CompilerParams(dimension_semantics=sem, vmem_limit_bytes=VMEM_LIMIT_BYTES)


def _split3(x):
    hi = x.astype(BF16)
    r1 = x - hi.astype(F32)
    mid = r1.astype(BF16)
    lo = (r1 - mid.astype(F32)).astype(BF16)
    return hi, mid, lo


def _rmsnorm_body(x_ref, g_ref, o_ref):
    x = x_ref[...].astype(F32)
    ms = jnp.mean(x * x, axis=-1, keepdims=True)
    o_ref[...] = (x * lax.rsqrt(ms + NORM_EPS) * g_ref[...].astype(F32)).astype(o_ref.dtype)


def rmsnorm(x, gain, out_dtype=BF16, tm=512):
    m, d = x.shape
    tm = min(tm, m)
    return pl.pallas_call(
        _rmsnorm_body,
        grid=(m // tm,),
        in_specs=[pl.BlockSpec((tm, d), lambda i: (i, 0)),
                  pl.BlockSpec((1, d), lambda i: (0, 0))],
        out_specs=pl.BlockSpec((tm, d), lambda i: (i, 0)),
        out_shape=jax.ShapeDtypeStruct((m, d), out_dtype),
        compiler_params=_params("arbitrary"),
        name="rmsnorm",
    )(x, gain.reshape(1, d))


def _tile_dot(a_ref, w_ref, w_transposed):
    w = w_ref[0].astype(BF16)
    dims = (((1,), (1,)), ((), ())) if w_transposed else (((1,), (0,)), ((), ()))
    return lax.dot_general(a_ref[...], w, dims, preferred_element_type=F32)


def _w_spec(k, tn, layer, w_transposed):
    if w_transposed:
        return pl.BlockSpec((1, tn, k), lambda i, j: (layer, j, 0))
    return pl.BlockSpec((1, k, tn), lambda i, j: (layer, 0, j))


def _matmul_body(a_ref, w_ref, o_ref, *, w_transposed):
    o_ref[...] = _tile_dot(a_ref, w_ref, w_transposed).astype(o_ref.dtype)


def _matmul_res_body(a_ref, w_ref, r_ref, o_ref, *, w_transposed):
    o_ref[...] = (r_ref[...].astype(F32) + _tile_dot(a_ref, w_ref, w_transposed)).astype(o_ref.dtype)


def matmul(a, w, layer=0, n=None, residual=None, w_transposed=False, out_dtype=F32, tm=1024, tn=512):
    m, k = a.shape
    n = w.shape[1 if w_transposed else 2] if n is None else n
    tm = min(tm, m)
    tn = min(tn, n)
    assert m % tm == 0 and n % tn == 0, (m, n, tm, tn)
    in_specs = [pl.BlockSpec((tm, k), lambda i, j: (i, 0)), _w_spec(k, tn, layer, w_transposed)]
    args = [a, w]
    body = functools.partial(_matmul_body, w_transposed=w_transposed)
    if residual is not None:
        in_specs.append(pl.BlockSpec((tm, tn), lambda i, j: (i, j)))
        args.append(residual)
        body = functools.partial(_matmul_res_body, w_transposed=w_transposed)
    return pl.pallas_call(
        body,
        grid=(m // tm, n // tn),
        in_specs=in_specs,
        out_specs=pl.BlockSpec((tm, tn), lambda i, j: (i, j)),
        out_shape=jax.ShapeDtypeStruct((m, n), out_dtype),
        compiler_params=_params("arbitrary", "arbitrary"),
        name="matmul",
    )(*args)


def _matmul_groupnorm_body(a_ref, w_ref, g_ref, o_ref, *, norm_groups, scale, split_out, w_transposed):
    acc = _tile_dot(a_ref, w_ref, w_transposed)
    for c, do_norm in enumerate(norm_groups):
        seg = acc[:, c * LANES:(c + 1) * LANES]
        if do_norm:
            ms = jnp.mean(seg * seg, axis=-1, keepdims=True)
            seg = seg * lax.rsqrt(ms + NORM_EPS) * g_ref[:, c * LANES:(c + 1) * LANES] * scale
        if split_out:
            o_ref[c] = seg.astype(o_ref.dtype)
        else:
            o_ref[:, c * LANES:(c + 1) * LANES] = seg.astype(o_ref.dtype)


def matmul_groupnorm(a, w, gain_cols, norm_groups, layer=0, n=None, w_transposed=False, scale=1.0, split_out=False,
                     out_dtype=BF16, tm=512, tn=512):
    m, k = a.shape
    n = w.shape[1 if w_transposed else 2] if n is None else n
    tm = min(tm, m)
    tn = min(tn, n)
    assert m % tm == 0 and n % tn == 0 and len(norm_groups) == tn // LANES
    if split_out:
        out_shape = jax.ShapeDtypeStruct((n // LANES, m, LANES), out_dtype)
        out_spec = pl.BlockSpec((tn // LANES, tm, LANES), lambda i, j: (j, i, 0))
    else:
        out_shape = jax.ShapeDtypeStruct((m, n), out_dtype)
        out_spec = pl.BlockSpec((tm, tn), lambda i, j: (i, j))
    body = functools.partial(_matmul_groupnorm_body, norm_groups=tuple(norm_groups), scale=scale,
                             split_out=split_out, w_transposed=w_transposed)
    return pl.pallas_call(
        body,
        grid=(m // tm, n // tn),
        in_specs=[pl.BlockSpec((tm, k), lambda i, j: (i, 0)),
                  _w_spec(k, tn, layer, w_transposed),
                  pl.BlockSpec((1, tn), lambda i, j: (0, j))],
        out_specs=out_spec,
        out_shape=out_shape,
        compiler_params=_params("arbitrary", "arbitrary"),
        name="matmul_groupnorm",
    )(a, w, gain_cols.reshape(1, n).astype(F32))


def _pad_cols(w, mult=LANES):
    pad = (-w.shape[-1]) % mult
    return jnp.pad(w, ((0, 0), (0, pad))) if pad else w


def _gla_body(q_ref, k_ref, v_ref, g_ref, a_ref, wg_ref, bg_ref, og_ref, tri_ref, o_ref, s_ref):
    dk, dv = s_ref.shape[1], s_ref.shape[2]
    C = GLA_BLOCK

    @pl.when(pl.program_id(2) == 0)
    def _():
        s_ref[...] = jnp.zeros(s_ref.shape, F32)

    tri = tri_ref[...]
    row = lax.broadcasted_iota(I32, (C, C), 0)
    col = lax.broadcasted_iota(I32, (C, C), 1)
    for c in range(q_ref.shape[0] // C):
        rows = slice(c * C, (c + 1) * C)
        a = a_ref[rows, :].astype(BF16)
        for hh in range(s_ref.shape[0]):
            kc = slice(hh * dk, (hh + 1) * dk)
            vc = slice(hh * dv, (hh + 1) * dv)
            gate_in = jnp.dot(a, wg_ref[:, kc], preferred_element_type=F32) + bg_ref[:, kc]
            log_a = jax.nn.log_sigmoid(gate_in) * (1.0 / GLA_GATE_TAU)
            hi, mid, lo = _split3(log_a)
            bcum = (jnp.dot(tri, hi, preferred_element_type=F32) + jnp.dot(tri, mid, preferred_element_type=F32)
                    + jnp.dot(tri, lo, preferred_element_type=F32))
            b_mid = bcum[C // 2 - 1:C // 2, :]
            q = q_ref[rows, kc].astype(F32) * dk ** -0.5
            k = k_ref[rows, kc].astype(F32)
            v = v_ref[rows, vc]
            att = lax.dot_general((q * jnp.exp(bcum - b_mid)).astype(BF16),
                                  (k * jnp.exp(b_mid - bcum)).astype(BF16),
                                  (((1,), (1,)), ((), ())), preferred_element_type=F32)
            att = jnp.where(col <= row, att, 0.0)
            o = jnp.dot(att.astype(BF16), v, preferred_element_type=F32)
            o = o + jnp.dot((q * jnp.exp(bcum)).astype(BF16), s_ref[hh].astype(BF16), preferred_element_type=F32)
            bcum_t = bcum.T
            b_last = bcum_t[:, C - 1:C]
            k_t = (k.T * jnp.exp(b_last - bcum_t)).astype(BF16)
            s_ref[hh] = s_ref[hh] * jnp.exp(b_last) + jnp.dot(k_t, v, preferred_element_type=F32)
            ms = jnp.mean(o * o, axis=-1, keepdims=True)
            o = o * lax.rsqrt(ms + NORM_EPS) * og_ref[...]
            o_ref[rows, vc] = (o * jax.nn.silu(g_ref[rows, vc].astype(F32))).astype(o_ref.dtype)


def gla_core(proj, a_pad, w_gate2, b_gate2, out_gain, B, T):
    H = GLA_HEADS
    n = proj.shape[0]
    dv = out_gain.shape[0]
    dk = w_gate2.shape[1] // H
    tb = min(GLA_STEP_TOKENS, T)
    nt = T // tb
    wg = jnp.pad(w_gate2, ((0, LANES - w_gate2.shape[0]), (0, 0))).astype(BF16)
    tri = jnp.asarray(np.tril(np.ones((GLA_BLOCK, GLA_BLOCK), np.float32)), BF16)
    hp = GLA_HEADS_PER_STEP
    hs = H // hp
    v0 = 2 * H * dk // (hp * dv)
    return pl.pallas_call(
        _gla_body,
        grid=(B, hs, nt),
        in_specs=[pl.BlockSpec((tb, hp * dk), lambda b, h, i: (b * nt + i, h)),
                  pl.BlockSpec((tb, hp * dk), lambda b, h, i: (b * nt + i, hs + h)),
                  pl.BlockSpec((tb, hp * dv), lambda b, h, i: (b * nt + i, v0 + h)),
                  pl.BlockSpec((tb, hp * dv), lambda b, h, i: (b * nt + i, v0 + hs + h)),
                  pl.BlockSpec((tb, LANES), lambda b, h, i: (b * nt + i, 0)),
                  pl.BlockSpec((LANES, hp * dk), lambda b, h, i: (0, h)),
                  pl.BlockSpec((1, hp * dk), lambda b, h, i: (0, h)),
                  pl.BlockSpec((1, dv), lambda b, h, i: (0, 0)),
                  pl.BlockSpec((GLA_BLOCK, GLA_BLOCK), lambda b, h, i: (0, 0))],
        out_specs=pl.BlockSpec((tb, hp * dv), lambda b, h, i: (b * nt + i, h)),
        out_shape=jax.ShapeDtypeStruct((n, H * dv), BF16),
        scratch_shapes=[pltpu.VMEM((hp, dk, dv), F32)],
        compiler_params=_params("arbitrary", "arbitrary", "arbitrary"),
        name="gla_core",
    )(proj, proj, proj, proj, a_pad, wg, b_gate2.reshape(1, -1).astype(F32), out_gain.reshape(1, dv).astype(F32), tri)


def _nsa_compress_body(u_ref, w1_ref, w2_ref, pe_ref, kg_ref, o_ref):
    kv = pl.program_id(0)
    u = u_ref[0, 0]
    half = u.shape[1]
    w1 = w1_ref[0]
    a = jnp.dot(u, w1[:half], preferred_element_type=F32)
    b = jnp.dot(u, w1[half:], preferred_element_type=F32)
    pe_term = jnp.dot(pe_ref[0], w1, preferred_element_type=F32)[0:1]
    n_chunks = u.shape[0]
    hid = jax.nn.gelu(a + pltpu.roll(b, n_chunks - 1, 0) + pe_term, approximate=True)
    out = jnp.dot(hid.astype(BF16), w2_ref[0], preferred_element_type=F32)
    ms = jnp.mean(out * out, axis=-1, keepdims=True)
    normed = out * lax.rsqrt(ms + NORM_EPS) * kg_ref[...]
    res = jnp.where(kv == 0, normed, out)
    row = lax.broadcasted_iota(I32, res.shape, 0)
    o_ref[0, 0, 0] = jnp.where(row < n_chunks - 1, res, 0.0).astype(o_ref.dtype)


def nsa_compress(zz, cmp_pe, cmp_w1, cmp_w2, kc_gain, B, T):
    G = NSA_GROUPS
    Dh = zz.shape[-1]
    n_chunks = T // CMP_STRIDE
    u = zz.reshape(zz.shape[0], B, n_chunks, CMP_STRIDE * Dh)
    w1 = cmp_w1.reshape(2, CMP_BLOCK * Dh, Dh).astype(BF16)
    pe = jnp.broadcast_to(cmp_pe.reshape(2, 1, CMP_BLOCK * Dh), (2, 8, CMP_BLOCK * Dh)).astype(BF16)
    return pl.pallas_call(
        _nsa_compress_body,
        grid=(2, B, G),
        in_specs=[pl.BlockSpec((1, 1, n_chunks, CMP_STRIDE * Dh), lambda kv, b, g: (kv * G + g, b, 0, 0)),
                  pl.BlockSpec((1, CMP_BLOCK * Dh, Dh), lambda kv, b, g: (kv, 0, 0)),
                  pl.BlockSpec((1, Dh, Dh), lambda kv, b, g: (kv, 0, 0)),
                  pl.BlockSpec((1, 8, CMP_BLOCK * Dh), lambda kv, b, g: (kv, 0, 0)),
                  pl.BlockSpec((1, Dh), lambda kv, b, g: (0, 0))],
        out_specs=pl.BlockSpec((1, 1, 1, n_chunks, Dh), lambda kv, b, g: (kv, b, g, 0, 0)),
        out_shape=jax.ShapeDtypeStruct((2, B, G, n_chunks, Dh), BF16),
        compiler_params=_params("arbitrary", "arbitrary", "arbitrary"),
        name="nsa_compress",
    )(u, w1, cmp_w2.astype(BF16), pe, kc_gain.reshape(1, Dh).astype(F32))


def _dot_nt(a, b):
    return lax.dot_general(a, b, (((1,), (1,)), ((), ())), preferred_element_type=F32)


def _col_softmax_terms(s):
    m = jnp.max(s, axis=0, keepdims=True)
    m = jnp.where(m == NEG_INF, 0.0, m)
    e = jnp.exp2(s - m)
    return e, 1.0 / jnp.maximum(jnp.sum(e, axis=0, keepdims=True), 1e-30)


def _tile_lanes(x, n):
    return jnp.concatenate([x] * n, axis=1)


def _nsa_attn_body(q_ref, qaug_ref, glog_ref, kc_ref, vct_ref, ks_ref, vst_ref, kw_ref, vwt_ref,
                   mselt_ref, efullt_ref, o_ref, q2_ref, acct_ref, m_ref, l_ref, oacct_ref, s_ref, sc_ref, sw_ref,
                   sel_ref):
    qb = pl.program_id(2)
    Dh = LANES
    RB = NSA_ROW_BLOCK
    hpb = RB // Q_BLOCK
    n_rb = NSA_HG // hpb
    n_cmp_pad = kc_ref.shape[2]
    n_slc = mselt_ref.shape[0]
    win_keys = WINDOW + Q_BLOCK
    s0 = qb * Q_BLOCK
    t_q = s0 + lax.broadcasted_iota(I32, (1, Q_BLOCK), 1)

    gates_t = jax.nn.sigmoid(glog_ref[0]).T

    for hg in range(NSA_HG):
        q2_ref[hg * Q_BLOCK:(hg + 1) * Q_BLOCK, :Dh] = q_ref[0, :, hg * Dh:(hg + 1) * Dh]
    q2_ref[:, Dh:] = qaug_ref[0]

    cmp_end = lax.broadcasted_iota(I32, (n_cmp_pad, 1), 0) * CMP_STRIDE + (CMP_BLOCK - 1)
    cmp_mask = _tile_lanes(jnp.where(cmp_end <= t_q, 0.0, NEG_INF), hpb)

    ws = pl.multiple_of(jnp.maximum(s0 - WINDOW, 0), Q_BLOCK)
    wdist = t_q - (ws + lax.broadcasted_iota(I32, (win_keys, 1), 0))
    win_mask = _tile_lanes(jnp.where((wdist >= 0) & (wdist < WINDOW), 0.0, NEG_INF), hpb)

    kc = kc_ref[0, 0]
    vct = jnp.concatenate([vct_ref[0, 0, j] for j in range(n_cmp_pad // Dh)], axis=1)
    kwin = kw_ref[0, 0, pl.ds(ws, win_keys), :]
    wblk = ws // Dh
    vwt = jnp.concatenate([vwt_ref[0, 0, wblk + j] for j in range(win_keys // Dh)], axis=1)

    imp_t = jnp.zeros((n_cmp_pad, Q_BLOCK), F32)
    for rb in range(n_rb):
        q_rb = q2_ref[rb * RB:(rb + 1) * RB, :]
        sc_ref[rb] = _dot_nt(kc, q_rb) + cmp_mask
        sw_ref[rb] = _dot_nt(kwin, q_rb) + win_mask
    for rb in range(n_rb):
        e, inv = _col_softmax_terms(sc_ref[rb])
        p = e * inv
        for j in range(hpb):
            imp_t = imp_t + p[:, j * Q_BLOCK:(j + 1) * Q_BLOCK]
        o_c = jnp.dot(vct, p.astype(BF16), preferred_element_type=F32)
        e, inv = _col_softmax_terms(sw_ref[rb])
        o_w = jnp.dot(vwt, e.astype(BF16), preferred_element_type=F32) * inv
        for j in range(hpb):
            hg = rb * hpb + j
            sub = slice(j * Q_BLOCK, (j + 1) * Q_BLOCK)
            acct_ref[:, hg * Q_BLOCK:(hg + 1) * Q_BLOCK] = (
                gates_t[hg:hg + 1, :] * o_c[:, sub] + gates_t[2 * NSA_HG + hg:2 * NSA_HG + hg + 1, :] * o_w[:, sub])

    hi, mid, lo = _split3(imp_t)
    mselt = mselt_ref[...]
    slc_imp = (jnp.dot(mselt, hi, preferred_element_type=F32) + jnp.dot(mselt, mid, preferred_element_type=F32)
               + jnp.dot(mselt, lo, preferred_element_type=F32))

    blk = lax.broadcasted_iota(I32, (n_slc, 1), 0)
    cur = lax.shift_right_logical(t_q, 6)
    forced = (blk == 0) | (blk == cur) | (blk == cur - 1)
    score = jnp.where(forced, jnp.inf, jnp.where(blk <= cur, slc_imp, NEG_INF))
    sel = jnp.zeros((n_slc, Q_BLOCK), F32)
    for _ in range(SLC_TOPK):
        m = jnp.max(score, axis=0, keepdims=True)
        cand = (score == m) & (m > NEG_INF)
        first = jnp.min(jnp.where(cand, blk, n_slc), axis=0, keepdims=True)
        one = blk == first
        sel = jnp.where(one, 1.0, sel)
        score = jnp.where(one, NEG_INF, score)
    sel_b = sel.astype(BF16)
    sel_ref[...] = sel

    m_ref[...] = jnp.full(m_ref.shape, NEG_INF, F32)
    l_ref[...] = jnp.zeros(l_ref.shape, F32)
    oacct_ref[...] = jnp.zeros(oacct_ref.shape, F32)
    KC = SLC_KEY_CHUNK
    key_iota = lax.broadcasted_iota(I32, (KC, 1), 0)

    def chunk_step(c, carry):
        k0 = pl.multiple_of(c * KC, KC)
        bpc = KC // SLC_BLOCK
        picked = sel_ref[pl.ds(pl.multiple_of(c * bpc, bpc), bpc), :]
        any_picked = jnp.max(jnp.max(picked, axis=0, keepdims=True), axis=1, keepdims=True)[0, 0]

        @pl.when(any_picked > 0.5)
        def _():
            kch = ks_ref[0, 0, pl.ds(k0, KC), :]
            vt = jnp.concatenate([vst_ref[0, 0, c * (KC // Dh) + j] for j in range(KC // Dh)], axis=1)
            selk = jnp.dot(efullt_ref[pl.ds(k0, KC), :], sel_b, preferred_element_type=F32)
            mask = _tile_lanes(jnp.where((selk > 0.5) & (k0 + key_iota <= t_q), 0.0, NEG_INF), hpb)
            for rb in range(n_rb):
                s_ref[rb] = _dot_nt(kch, q2_ref[rb * RB:(rb + 1) * RB, :]) + mask
            for rb in range(n_rb):
                cols = slice(rb * RB, (rb + 1) * RB)
                s = s_ref[rb]
                m_old = m_ref[rb]
                m_new = jnp.maximum(m_old, jnp.max(s, axis=0, keepdims=True))
                m_safe = jnp.where(m_new == NEG_INF, 0.0, m_new)
                alpha = jnp.exp2(m_old - m_safe)
                p = jnp.exp2(s - m_safe)
                l_ref[rb] = alpha * l_ref[rb] + jnp.sum(p, axis=0, keepdims=True)
                oacct_ref[:, cols] = (alpha * oacct_ref[:, cols]
                                      + jnp.dot(vt, p.astype(BF16), preferred_element_type=F32))
                m_ref[rb] = m_new
        return carry

    n_chunks = (s0 + Q_BLOCK + KC - 1) // KC
    lax.fori_loop(0, n_chunks, chunk_step, 0)
    for hg in range(NSA_HG):
        rb, j = divmod(hg, hpb)
        cols = slice(hg * Q_BLOCK, (hg + 1) * Q_BLOCK)
        inv = 1.0 / jnp.maximum(l_ref[rb][:, j * Q_BLOCK:(j + 1) * Q_BLOCK], 1e-30)
        o_t = acct_ref[:, cols] + gates_t[NSA_HG + hg:NSA_HG + hg + 1, :] * (oacct_ref[:, cols] * inv)
        o_ref[0, :, hg * Dh:(hg + 1) * Dh] = o_t.T.astype(o_ref.dtype)


def _pos_pieces(pos):
    pos = np.asarray(pos)
    out = np.zeros((pos.shape[0], LANES), np.float32)
    for i in range(3):
        out[:, 2 * i] = 64 * (pos // 64)
        out[:, 2 * i + 1] = pos % 64
    return jnp.asarray(out, BF16)


def _slope_pieces():
    H, G, HG = NSA_HEADS, NSA_GROUPS, NSA_HG
    slopes = jnp.asarray(LOG2E * 2.0 ** (-8.0 * np.arange(1, H + 1, dtype=np.float64) / H), F32)
    pieces = jnp.stack(_split3(slopes), axis=-1)
    cols = jnp.repeat(pieces, 2, axis=-1)
    cols = jnp.pad(cols, ((0, 0), (0, LANES - cols.shape[-1])))
    return jnp.repeat(cols.reshape(G, HG, 1, LANES), Q_BLOCK, axis=2).reshape(G, HG * Q_BLOCK, LANES)


def _blocked_transpose(v):
    lead, (t, dh) = v.shape[:-2], v.shape[-2:]
    return jnp.swapaxes(v.reshape(*lead, t // LANES, LANES, dh), -1, -2)


def nsa_attention(q, glog, kvc, zz, B, T):
    G, HG, Dh = NSA_GROUPS, NSA_HG, LANES
    H = NSA_HEADS
    n_cmp = T // CMP_STRIDE - CMP_BLOCK // CMP_STRIDE + 1
    n_cmp_pad = -(-T // CMP_STRIDE // LANES) * LANES
    n_slc = T // SLC_BLOCK
    per = SLC_BLOCK // CMP_STRIDE
    n = np.arange(n_cmp_pad)[None, :]
    j = np.arange(n_slc)[:, None]
    mselt = ((n // per == j).astype(np.float32) + ((n + 1) // per == j).astype(np.float32))
    mselt[:, n_cmp:] = 0.0
    kvc = jnp.pad(kvc, ((0, 0), (0, 0), (0, 0), (0, n_cmp_pad - kvc.shape[3]), (0, 0)))
    efullt = (np.arange(T)[:, None] // SLC_BLOCK == np.arange(n_slc)[None, :]).astype(np.float32)

    key_aug = jnp.broadcast_to(_pos_pieces(np.arange(T)), (G, B, T, LANES))
    cmp_aug = jnp.broadcast_to(_pos_pieces(np.arange(n_cmp_pad) * CMP_STRIDE + CMP_BLOCK - 1),
                               (B, G, n_cmp_pad, LANES))
    kc_aug = jnp.concatenate([kvc[0], cmp_aug], axis=-1)
    vct = _blocked_transpose(kvc[1])
    ks_aug = jnp.concatenate([zz[2 * G:3 * G], key_aug], axis=-1)
    kw_aug = jnp.concatenate([zz[4 * G:5 * G], key_aug], axis=-1)
    vst = _blocked_transpose(zz[3 * G:4 * G])
    vwt = _blocked_transpose(zz[5 * G:6 * G])

    def k_spec():
        return pl.BlockSpec((1, 1, T, 2 * Dh), lambda b, g, i: (g, b, 0, 0))

    def vt_spec():
        return pl.BlockSpec((1, 1, T // LANES, Dh, LANES), lambda b, g, i: (g, b, 0, 0, 0))

    n_rb = HG * Q_BLOCK // NSA_ROW_BLOCK
    return pl.pallas_call(
        _nsa_attn_body,
        grid=(B, G, T // Q_BLOCK),
        in_specs=[pl.BlockSpec((1, Q_BLOCK, HG * Dh), lambda b, g, i: (b, i, g)),
                  pl.BlockSpec((1, HG * Q_BLOCK, LANES), lambda b, g, i: (g, 0, 0)),
                  pl.BlockSpec((1, Q_BLOCK, LANES), lambda b, g, i: (b, i, g)),
                  pl.BlockSpec((1, 1, n_cmp_pad, 2 * Dh), lambda b, g, i: (b, g, 0, 0)),
                  pl.BlockSpec((1, 1, n_cmp_pad // LANES, Dh, LANES), lambda b, g, i: (b, g, 0, 0, 0)),
                  k_spec(), vt_spec(), k_spec(), vt_spec(),
                  pl.BlockSpec((n_slc, n_cmp_pad), lambda b, g, i: (0, 0)),
                  pl.BlockSpec((T, n_slc), lambda b, g, i: (0, 0))],
        out_specs=pl.BlockSpec((1, Q_BLOCK, HG * Dh), lambda b, g, i: (b, i, g)),
        out_shape=jax.ShapeDtypeStruct((B, T, H * Dh), BF16),
        scratch_shapes=[pltpu.VMEM((HG * Q_BLOCK, 2 * Dh), BF16),
                        pltpu.VMEM((Dh, HG * Q_BLOCK), F32),
                        pltpu.VMEM((n_rb, 1, NSA_ROW_BLOCK), F32),
                        pltpu.VMEM((n_rb, 1, NSA_ROW_BLOCK), F32),
                        pltpu.VMEM((Dh, HG * Q_BLOCK), F32),
                        pltpu.VMEM((n_rb, SLC_KEY_CHUNK, NSA_ROW_BLOCK), F32),
                        pltpu.VMEM((n_rb, n_cmp_pad, NSA_ROW_BLOCK), F32),
                        pltpu.VMEM((n_rb, WINDOW + Q_BLOCK, NSA_ROW_BLOCK), F32),
                        pltpu.VMEM((n_slc, Q_BLOCK), F32)],
        compiler_params=_params("arbitrary", "arbitrary", "arbitrary"),
        name="nsa_attention",
    )(q, _slope_pieces(), glog, kc_aug, vct, ks_aug, vst, kw_aug, vwt,
      jnp.asarray(mselt, BF16), jnp.asarray(efullt, BF16))


def _pack_rows(y, o_ref):
    m, d = y.shape
    bits = lax.bitcast_convert_type(y.astype(BF16).astype(F32), U32)
    for c in range(d // (2 * LANES)):
        lo = lax.shift_right_logical(bits[:, c * LANES:(c + 1) * LANES], jnp.uint32(16))
        hi = bits[:, d // 2 + c * LANES:d // 2 + (c + 1) * LANES]
        o_ref[pl.ds(c, m, stride=d // (2 * LANES)), :] = lo | hi


def _unpack_chunk(words):
    lo = lax.bitcast_convert_type(lax.shift_left(words, jnp.uint32(16)), F32)
    hi = lax.bitcast_convert_type(words & jnp.uint32(0xFFFF0000), F32)
    return lo, hi


def _moe_router_body(h_ref, g_ref, w_ref, b_ref, xp_ref, idx_ref, wgt_ref):
    x = h_ref[...]
    ms = jnp.mean(x * x, axis=-1, keepdims=True)
    xn = x * lax.rsqrt(ms + NORM_EPS) * g_ref[...]
    _pack_rows(xn, xp_ref)
    logits = jnp.dot(xn.astype(BF16), w_ref[...], preferred_element_type=F32) + b_ref[...]
    lane = lax.broadcasted_iota(I32, logits.shape, 1)
    logits = jnp.where(lane < N_EXPERTS, logits, NEG_INF)
    idx_out = jnp.zeros(logits.shape, I32)
    val_out = jnp.full(logits.shape, NEG_INF, F32)
    for k in range(TOP_K):
        m = jnp.max(logits, axis=-1, keepdims=True)
        first = jnp.min(jnp.where(logits == m, lane, LANES), axis=-1, keepdims=True)
        idx_out = jnp.where(lane == k, first, idx_out)
        val_out = jnp.where(lane == k, m, val_out)
        logits = jnp.where(lane == first, NEG_INF, logits)
    e = jnp.exp(val_out - jnp.max(val_out, axis=-1, keepdims=True))
    idx_ref[...] = idx_out
    wgt_ref[...] = e * (1.0 / jnp.sum(e, axis=-1, keepdims=True))


def moe_router(h, gain, w_router, b_router, tm=256):
    n, d = h.shape
    tm = min(tm, n)
    pr = d // (2 * LANES)
    w = _pad_cols(w_router).astype(BF16)
    b = _pad_cols(b_router.reshape(1, -1)).astype(F32)
    return pl.pallas_call(
        _moe_router_body,
        grid=(n // tm,),
        in_specs=[pl.BlockSpec((tm, d), lambda i: (i, 0)),
                  pl.BlockSpec((1, d), lambda i: (0, 0)),
                  pl.BlockSpec((d, LANES), lambda i: (0, 0)),
                  pl.BlockSpec((1, LANES), lambda i: (0, 0))],
        out_specs=[pl.BlockSpec((tm * pr, LANES), lambda i: (i, 0)),
                   pl.BlockSpec((tm, LANES), lambda i: (i, 0)),
                   pl.BlockSpec((tm, LANES), lambda i: (i, 0))],
        out_shape=[jax.ShapeDtypeStruct((n * pr, LANES), U32),
                   jax.ShapeDtypeStruct((n, LANES), I32), jax.ShapeDtypeStruct((n, LANES), F32)],
        compiler_params=_params("arbitrary"),
        name="moe_router",
    )(h, gain.reshape(1, d).astype(F32), w, b)


def _row_gather_copy(src_hbm, src_row, dst_buf, slot, r, sem, pr):
    return pltpu.make_async_copy(src_hbm.at[pl.ds(pl.multiple_of(src_row, pr), pr)],
                                 dst_buf.at[slot, pl.ds(pl.multiple_of(r * pr, pr), pr)], sem.at[slot])


def _start_row_gather(ids_ref, src_hbm, dst_buf, slot, sem, n_rows, pr):
    def body(r, carry):
        _row_gather_copy(src_hbm, ids_ref[0, 0, r], dst_buf, slot, r, sem, pr).start()
        return carry
    lax.fori_loop(0, n_rows, body, 0, unroll=8)


def _wait_row_gather(src_hbm, dst_buf, slot, sem):
    pltpu.make_async_copy(src_hbm.at[pl.ds(0, dst_buf.shape[1])], dst_buf.at[slot], sem.at[slot]).wait()


def _moe_expert_body(blk_e_ref, n_used_ref, ids0_ref, idsn_ref, x_hbm, wgu_ref, bgu_ref, wd_ref, bd_ref,
                     y_ref, xbuf, xs, sem):
    i = pl.program_id(0)
    n_used = n_used_ref[0]
    rows, d = xs.shape
    pr = d // (2 * LANES)
    ff = wd_ref.shape[3]

    @pl.when(i == 0)
    def _():
        _start_row_gather(ids0_ref, x_hbm, xbuf, 0, sem, rows, pr)

    @pl.when(i + 1 < n_used)
    def _():
        _start_row_gather(idsn_ref, x_hbm, xbuf, (i + 1) % 2, sem, rows, pr)

    @pl.when(i < n_used)
    def _():
        slot = i % 2
        _wait_row_gather(x_hbm, xbuf, slot, sem)
        sub = rows // MOE_SUB_BLOCKS
        for sb in range(MOE_SUB_BLOCKS):
            r0 = sb * sub
            for c in range(pr):
                lo, hi = _unpack_chunk(xbuf[slot, pl.ds(r0 * pr + c, sub, stride=pr), :])
                xs[r0:r0 + sub, c * LANES:(c + 1) * LANES] = lo.astype(BF16)
                xs[r0:r0 + sub, d // 2 + c * LANES:d // 2 + (c + 1) * LANES] = hi.astype(BF16)
            gu_t = lax.dot_general(wgu_ref[0, 0], xs[r0:r0 + sub, :], (((1,), (1,)), ((), ())),
                                   preferred_element_type=F32)
            gu_t = gu_t + jnp.concatenate([bgu_ref[0]] * (sub // LANES), axis=1)
            gate = jnp.minimum(gu_t[:ff], SWIGLU_LIMIT)
            up = jnp.clip(gu_t[ff:], -SWIGLU_LIMIT, SWIGLU_LIMIT)
            act_t = (up + 1.0) * (gate * jax.nn.sigmoid(SWIGLU_ALPHA * gate))
            y_t = jnp.dot(wd_ref[0, 0], act_t.astype(BF16), preferred_element_type=F32)
            _pack_rows(y_t.T + bd_ref[0], y_ref.at[pl.ds(r0 * pr, sub * pr), :])

    @pl.when(i >= n_used)
    def _():
        y_ref[...] = jnp.zeros(y_ref.shape, y_ref.dtype)


def moe_experts(xp, tok_rows, blk_e, n_used, w_gate_up, b_gate_up, w_down, b_down, l):
    _, E, ff2, d = w_gate_up.shape
    ff = ff2 // 2
    pr = d // (2 * LANES)
    rows = MOE_BLOCK_ROWS
    n_blocks = tok_rows.shape[0] // rows
    ids = tok_rows.reshape(n_blocks, 1, rows)

    def used(i, nu):
        return jnp.minimum(i, nu[0] - 1)

    grid_spec = pltpu.PrefetchScalarGridSpec(
        num_scalar_prefetch=2,
        grid=(n_blocks,),
        in_specs=[pl.BlockSpec((1, 1, rows), lambda i, be, nu: (0, 0, 0), memory_space=pltpu.SMEM),
                  pl.BlockSpec((1, 1, rows), lambda i, be, nu: (jnp.minimum(i + 1, n_blocks - 1), 0, 0),
                               memory_space=pltpu.SMEM),
                  pl.BlockSpec(memory_space=pl.ANY),
                  pl.BlockSpec((1, 1, ff2, d), lambda i, be, nu: (l, be[used(i, nu)], 0, 0)),
                  pl.BlockSpec((1, ff2, LANES), lambda i, be, nu: (be[used(i, nu)], 0, 0)),
                  pl.BlockSpec((1, 1, d, ff), lambda i, be, nu: (l, be[used(i, nu)], 0, 0)),
                  pl.BlockSpec((1, 1, d), lambda i, be, nu: (be[used(i, nu)], 0, 0))],
        out_specs=pl.BlockSpec((rows * pr, LANES), lambda i, be, nu: (i, 0)),
        scratch_shapes=[pltpu.VMEM((2, rows * pr, LANES), U32), pltpu.VMEM((rows, d), BF16),
                        pltpu.SemaphoreType.DMA((2,))],
    )
    return pl.pallas_call(
        _moe_expert_body,
        grid_spec=grid_spec,
        out_shape=jax.ShapeDtypeStruct((n_blocks * rows * pr, LANES), U32),
        compiler_params=_params("arbitrary"),
        name="moe_experts",
    )(blk_e, n_used, ids, ids, xp, w_gate_up,
      jnp.broadcast_to(b_gate_up.astype(F32)[:, :, None], (E, ff2, LANES)),
      w_down, b_down.reshape(E, 1, d).astype(F32))


def _moe_combine_body(ids0_ref, idsn_ref, y_hbm, h_ref, w_ref, g_ref, o_ref, xn_ref, ybuf, sem):
    i = pl.program_id(0)
    n_steps = pl.num_programs(0)
    tt, d = h_ref.shape
    pr = d // (2 * LANES)
    rows = TOP_K * tt

    @pl.when(i == 0)
    def _():
        _start_row_gather(ids0_ref, y_hbm, ybuf, 0, sem, rows, pr)

    @pl.when(i + 1 < n_steps)
    def _():
        _start_row_gather(idsn_ref, y_hbm, ybuf, (i + 1) % 2, sem, rows, pr)

    slot = i % 2
    _wait_row_gather(y_hbm, ybuf, slot, sem)
    w = w_ref[...]
    wk = [jnp.broadcast_to(w[:, k:k + 1], (tt, LANES)) for k in range(TOP_K)]
    ssq = jnp.zeros((tt, LANES), F32)
    for c in range(pr):
        lo_cols = slice(c * LANES, (c + 1) * LANES)
        hi_cols = slice(d // 2 + c * LANES, d // 2 + (c + 1) * LANES)
        acc_lo = h_ref[:, lo_cols]
        acc_hi = h_ref[:, hi_cols]
        for k in range(TOP_K):
            lo, hi = _unpack_chunk(ybuf[slot, pl.ds(k * tt * pr + c, tt, stride=pr), :])
            acc_lo = acc_lo + wk[k] * lo
            acc_hi = acc_hi + wk[k] * hi
        o_ref[:, lo_cols] = acc_lo
        o_ref[:, hi_cols] = acc_hi
        ssq = ssq + acc_lo * acc_lo + acc_hi * acc_hi
    inv = lax.rsqrt(jnp.sum(ssq, axis=-1, keepdims=True) * (1.0 / d) + NORM_EPS)
    xn_ref[...] = (o_ref[...] * inv * g_ref[...]).astype(xn_ref.dtype)


def moe_combine(yp, slot_rows, weights, h, next_gain):
    n, d = h.shape
    pr = d // (2 * LANES)
    tt = min(MOE_COMBINE_TOKENS, n)
    n_steps = n // tt
    ids = slot_rows.reshape(n_steps, tt, TOP_K).transpose(0, 2, 1).reshape(n_steps, 1, TOP_K * tt)
    return pl.pallas_call(
        _moe_combine_body,
        grid=(n_steps,),
        in_specs=[pl.BlockSpec((1, 1, TOP_K * tt), lambda i: (0, 0, 0), memory_space=pltpu.SMEM),
                  pl.BlockSpec((1, 1, TOP_K * tt), lambda i: (jnp.minimum(i + 1, n_steps - 1), 0, 0),
                               memory_space=pltpu.SMEM),
                  pl.BlockSpec(memory_space=pl.ANY),
                  pl.BlockSpec((tt, d), lambda i: (i, 0)),
                  pl.BlockSpec((tt, LANES), lambda i: (i, 0)),
                  pl.BlockSpec((1, d), lambda i: (0, 0))],
        out_specs=[pl.BlockSpec((tt, d), lambda i: (i, 0)),
                   pl.BlockSpec((tt, d), lambda i: (i, 0))],
        out_shape=[jax.ShapeDtypeStruct((n, d), F32), jax.ShapeDtypeStruct((n, d), BF16)],
        scratch_shapes=[pltpu.VMEM((2, TOP_K * tt * pr, LANES), U32), pltpu.SemaphoreType.DMA((2,))],
        compiler_params=_params("arbitrary"),
        name="moe_combine",
    )(ids, ids, yp, h, weights, next_gain.reshape(1, d).astype(F32))


def _moe_plan(top_idx, n_rows):
    E, rows = N_EXPERTS, MOE_BLOCK_ROWS
    flat_e = top_idx.reshape(-1)
    nk = flat_e.shape[0]
    onehot = (flat_e[:, None] == jnp.arange(E, dtype=I32)[None, :]).astype(I32)
    csum = jnp.cumsum(onehot, axis=0)
    rank = jnp.sum(onehot * csum, axis=1) - 1
    sizes = csum[-1]
    nblk = (sizes + rows - 1) // rows
    blk_end = jnp.cumsum(nblk)
    pad_start = (blk_end - nblk) * rows
    slot_dest = pad_start[flat_e] + rank
    tok_buf = jnp.zeros((n_rows,), I32).at[slot_dest].set(jnp.arange(nk, dtype=I32) // TOP_K)
    n_blocks = n_rows // rows
    blk_e = jnp.minimum(jnp.searchsorted(blk_end, jnp.arange(n_blocks, dtype=I32), side='right'), E - 1).astype(I32)
    n_used = blk_end[-1:].astype(I32)
    return slot_dest, tok_buf, blk_e, n_used


def moe_layer(h, gain, w_router, b_router, w_gate_up, b_gate_up, w_down, b_down, l, next_gain):
    n, d = h.shape
    pr = d // (2 * LANES)
    rows = MOE_BLOCK_ROWS
    xp, top_idx, weights = moe_router(h, gain, w_router, b_router)
    n_rows = n * TOP_K + N_EXPERTS * rows
    slot_dest, tok_buf, blk_e, n_used = _moe_plan(top_idx[:, :TOP_K], n_rows)
    yp = moe_experts(xp, tok_buf * pr, blk_e, n_used, w_gate_up, b_gate_up, w_down, b_down, l)
    return moe_combine(yp, slot_dest.reshape(n, TOP_K) * pr, weights, h, next_gain)


def _gla_mixer(h, xn, w_in, w_gate2, b_gate2, out_gain, w_out, l, B, T):
    n_main = w_in.shape[2] - GLA_GATE_RANK
    w_in_t = jnp.swapaxes(w_in, 1, 2)
    proj = matmul(xn, w_in_t, layer=l, n=n_main, w_transposed=True, out_dtype=BF16)
    wa_t = jnp.pad(w_in_t[l, n_main:], ((0, LANES - GLA_GATE_RANK), (0, 0)))
    a_pad = matmul(xn, wa_t[None], w_transposed=True)
    o = gla_core(proj, a_pad, w_gate2, b_gate2, out_gain, B, T)
    return matmul(o, w_out, layer=l, residual=h)


def _nsa_shared_kv(h, kv_gain, w_kv, k_gain, cmp_pe, cmp_w1, cmp_w2, B, T):
    G = NSA_GROUPS
    Dh = LANES
    xn = rmsnorm(h, kv_gain)
    ones = jnp.ones((G * Dh,), F32)
    gain_cols = jnp.concatenate([ones, ones, jnp.tile(k_gain[1], G), ones, jnp.tile(k_gain[2], G), ones])
    norm_groups = [False] * (2 * G) + [True] * G + [False] * G + [True] * G + [False] * G
    zz = matmul_groupnorm(xn, w_kv.astype(BF16)[None], gain_cols, norm_groups, split_out=True, tn=6 * G * Dh)
    kvc = nsa_compress(zz, cmp_pe, cmp_w1, cmp_w2, k_gain[0], B, T)
    return kvc, zz.reshape(6 * G, B, T, Dh)


def _nsa_mixer(h, xn, w_in, q_gain, w_out, kvc, zz, j, B, T):
    H, G, HG, Dh = NSA_HEADS, NSA_GROUPS, NSA_HG, LANES
    w_in_t = jnp.swapaxes(w_in, 1, 2)
    q = matmul_groupnorm(xn, w_in_t, jnp.tile(q_gain, H), [True] * 4, layer=j, n=H * Dh, w_transposed=True,
                         scale=Dh ** -0.5 * LOG2E, tm=1024, tn=512)
    wg = w_in_t[j, H * Dh:].reshape(G, HG, 3, -1).transpose(0, 2, 1, 3).reshape(G, 3 * HG, -1)
    wg = jnp.pad(wg, ((0, 0), (0, LANES - 3 * HG), (0, 0))).reshape(1, G * LANES, -1)
    glog = matmul(xn, wg, w_transposed=True)
    o = nsa_attention(q.reshape(B, T, H * Dh), glog.reshape(B, T, G * LANES), kvc, zz, B, T)
    return matmul(o.reshape(B * T, H * Dh), w_out, layer=j, residual=h)


def kernel(x, ln_mix, ln_ffn, a_w_in, a_w_gate2, a_b_gate2, a_out_gain, a_w_out, kv_gain, w_kv, k_gain,
           cmp_pe, cmp_w1, cmp_w2, b_w_in, b_q_gain, b_w_out, w_router, b_router, w_gate_up, b_gate_up,
           w_down, b_down):
    B, T, D = x.shape
    depth = ln_mix.shape[0]
    n_a = a_w_in.shape[0]
    h = x.reshape(B * T, D)
    shared = None
    wgu_bf = jnp.swapaxes(w_gate_up, 2, 3).astype(BF16)
    wd_bf = jnp.swapaxes(w_down, 2, 3).astype(BF16)
    xn = rmsnorm(h, ln_mix[0])
    for l in range(depth):
        if l < n_a:
            h = _gla_mixer(h, xn, a_w_in, a_w_gate2[l], a_b_gate2[l], a_out_gain[l], a_w_out, l, B, T)
        else:
            if l == n_a:
                shared = _nsa_shared_kv(h, kv_gain, w_kv, k_gain, cmp_pe, cmp_w1, cmp_w2, B, T)
            j = l - n_a
            h = _nsa_mixer(h, xn, b_w_in, b_q_gain[j], b_w_out, *shared, j, B, T)
        h, xn = moe_layer(h, ln_ffn[l], w_router[l], b_router[l], wgu_bf, b_gate_up[l], wd_bf, b_down[l], l,
                          ln_mix[min(l + 1, depth - 1)])
    return h.reshape(B, T, D)
```

```python
import functools
import math

import jax
import jax.numpy as jnp
from jax import lax
import numpy as np
from jax.experimental import pallas as pl
from jax.experimental.pallas import tpu as pltpu

F32 = jnp.float32
BF16 = jnp.bfloat16
I32 = jnp.int32
U32 = jnp.uint32

NORM_EPS = 1e-5
GLA_HEADS = 8
GLA_GATE_RANK = 16
GLA_GATE_TAU = 16.0
NSA_HEADS = 32
NSA_GROUPS = 2
NSA_HG = NSA_HEADS // NSA_GROUPS
CMP_BLOCK = 32
CMP_STRIDE = 16
SLC_BLOCK = 64
SLC_TOPK = 8
WINDOW = 512
Q_BLOCK = 128
N_EXPERTS = 32
TOP_K = 4
SWIGLU_ALPHA = 1.702
SWIGLU_LIMIT = 7.0

LANES = 128
MXU_DIM = 256
VMEM_LIMIT_BYTES = 56 * 1024 * 1024
MOE_BLOCK_ROWS = 512
MOE_SUB_BLOCKS = 4
MOE_COMBINE_TOKENS = 128
SLC_KEY_CHUNK = 512
NSA_ROW_BLOCK = 256
GLA_BLOCK = 128
GLA_STEP_TOKENS = 512
GLA_HEADS_PER_STEP = 2
NEG_INF = float("-inf")
LOG2E = math.log2(math.e)


def _params(*sem):
    return pltpu.CompilerParams(dimension_semantics=sem, vmem_limit_bytes=VMEM_LIMIT_BYTES)


def _split3(x):
    hi = x.astype(BF16)
    r1 = x - hi.astype(F32)
    mid = r1.astype(BF16)
    lo = (r1 - mid.astype(F32)).astype(BF16)
    return hi, mid, lo


def _rmsnorm_body(x_ref, g_ref, o_ref):
    x = x_ref[...].astype(F32)
    ms = jnp.mean(x * x, axis=-1, keepdims=True)
    o_ref[...] = (x * lax.rsqrt(ms + NORM_EPS) * g_ref[...].astype(F32)).astype(o_ref.dtype)


def rmsnorm(x, gain, out_dtype=BF16, tm=512):
    m, d = x.shape
    tm = min(tm, m)
    return pl.pallas_call(
        _rmsnorm_body,
        grid=(m // tm,),
        in_specs=[pl.BlockSpec((tm, d), lambda i: (i, 0)),
                  pl.BlockSpec((1, d), lambda i: (0, 0))],
        out_specs=pl.BlockSpec((tm, d), lambda i: (i, 0)),
        out_shape=jax.ShapeDtypeStruct((m, d), out_dtype),
        compiler_params=_params("arbitrary"),
        name="rmsnorm",
    )(x, gain.reshape(1, d))


def _tile_dot(a_ref, w_ref, w_transposed):
    w = w_ref[0].astype(BF16)
    dims = (((1,), (1,)), ((), ())) if w_transposed else (((1,), (0,)), ((), ()))
    return lax.dot_general(a_ref[...], w, dims, preferred_element_type=F32)


def _w_spec(k, tn, layer, w_transposed):
    if w_transposed:
        return pl.BlockSpec((1, tn, k), lambda i, j: (layer, j, 0))
    return pl.BlockSpec((1, k, tn), lambda i, j: (layer, 0, j))


def _matmul_body(a_ref, w_ref, o_ref, *, w_transposed):
    o_ref[...] = _tile_dot(a_ref, w_ref, w_transposed).astype(o_ref.dtype)


def _matmul_res_body(a_ref, w_ref, r_ref, o_ref, *, w_transposed):
    o_ref[...] = (r_ref[...].astype(F32) + _tile_dot(a_ref, w_ref, w_transposed)).astype(o_ref.dtype)


def matmul(a, w, layer=0, n=None, residual=None, w_transposed=False, out_dtype=F32, tm=1024, tn=512):
    m, k = a.shape
    n = w.shape[1 if w_transposed else 2] if n is None else n
    tm = min(tm, m)
    tn = min(tn, n)
    assert m % tm == 0 and n % tn == 0, (m, n, tm, tn)
    in_specs = [pl.BlockSpec((tm, k), lambda i, j: (i, 0)), _w_spec(k, tn, layer, w_transposed)]
    args = [a, w]
    body = functools.partial(_matmul_body, w_transposed=w_transposed)
    if residual is not None:
        in_specs.append(pl.BlockSpec((tm, tn), lambda i, j: (i, j)))
        args.append(residual)
        body = functools.partial(_matmul_res_body, w_transposed=w_transposed)
    return pl.pallas_call(
        body,
        grid=(m // tm, n // tn),
        in_specs=in_specs,
        out_specs=pl.BlockSpec((tm, tn), lambda i, j: (i, j)),
        out_shape=jax.ShapeDtypeStruct((m, n), out_dtype),
        compiler_params=_params("arbitrary", "arbitrary"),
        name="matmul",
    )(*args)


def _matmul_groupnorm_body(a_ref, w_ref, g_ref, o_ref, *, norm_groups, scale, split_out, w_transposed):
    acc = _tile_dot(a_ref, w_ref, w_transposed)
    for c, do_norm in enumerate(norm_groups):
        seg = acc[:, c * LANES:(c + 1) * LANES]
        if do_norm:
            ms = jnp.mean(seg * seg, axis=-1, keepdims=True)
            seg = seg * lax.rsqrt(ms + NORM_EPS) * g_ref[:, c * LANES:(c + 1) * LANES] * scale
        if split_out:
            o_ref[c] = seg.astype(o_ref.dtype)
        else:
            o_ref[:, c * LANES:(c + 1) * LANES] = seg.astype(o_ref.dtype)


def matmul_groupnorm(a, w, gain_cols, norm_groups, layer=0, n=None, w_transposed=False, scale=1.0, split_out=False,
                     out_dtype=BF16, tm=512, tn=512):
    m, k = a.shape
    n = w.shape[1 if w_transposed else 2] if n is None else n
    tm = min(tm, m)
    tn = min(tn, n)
    assert m % tm == 0 and n % tn == 0 and len(norm_groups) == tn // LANES
    if split_out:
        out_shape = jax.ShapeDtypeStruct((n // LANES, m, LANES), out_dtype)
        out_spec = pl.BlockSpec((tn // LANES, tm, LANES), lambda i, j: (j, i, 0))
    else:
        out_shape = jax.ShapeDtypeStruct((m, n), out_dtype)
        out_spec = pl.BlockSpec((tm, tn), lambda i, j: (i, j))
    body = functools.partial(_matmul_groupnorm_body, norm_groups=tuple(norm_groups), scale=scale,
                             split_out=split_out, w_transposed=w_transposed)
    return pl.pallas_call(
        body,
        grid=(m // tm, n // tn),
        in_specs=[pl.BlockSpec((tm, k), lambda i, j: (i, 0)),
                  _w_spec(k, tn, layer, w_transposed),
                  pl.BlockSpec((1, tn), lambda i, j: (0, j))],
        out_specs=out_spec,
        out_shape=out_shape,
        compiler_params=_params("arbitrary", "arbitrary"),
        name="matmul_groupnorm",
    )(a, w, gain_cols.reshape(1, n).astype(F32))


def _pad_cols(w, mult=LANES):
    pad = (-w.shape[-1]) % mult
    return jnp.pad(w, ((0, 0), (0, pad))) if pad else w


def _gla_body(q_ref, k_ref, v_ref, g_ref, a_ref, wg_ref, bg_ref, og_ref, tri_ref, o_ref, s_ref):
    dk, dv = s_ref.shape[1], s_ref.shape[2]
    C = GLA_BLOCK

    @pl.when(pl.program_id(2) == 0)
    def _():
        s_ref[...] = jnp.zeros(s_ref.shape, F32)

    tri = tri_ref[...]
    row = lax.broadcasted_iota(I32, (C, C), 0)
    col = lax.broadcasted_iota(I32, (C, C), 1)
    for c in range(q_ref.shape[0] // C):
        rows = slice(c * C, (c + 1) * C)
        a = a_ref[rows, :].astype(BF16)
        for hh in range(s_ref.shape[0]):
            kc = slice(hh * dk, (hh + 1) * dk)
            vc = slice(hh * dv, (hh + 1) * dv)
            gate_in = jnp.dot(a, wg_ref[:, kc], preferred_element_type=F32) + bg_ref[:, kc]
            log_a = jax.nn.log_sigmoid(gate_in) * (1.0 / GLA_GATE_TAU)
            hi, mid, lo = _split3(log_a)
            bcum = (jnp.dot(tri, hi, preferred_element_type=F32) + jnp.dot(tri, mid, preferred_element_type=F32)
                    + jnp.dot(tri, lo, preferred_element_type=F32))
            b_mid = bcum[C // 2 - 1:C // 2, :]
            q = q_ref[rows, kc].astype(F32) * dk ** -0.5
            k = k_ref[rows, kc].astype(F32)
            v = v_ref[rows, vc]
            att = lax.dot_general((q * jnp.exp(bcum - b_mid)).astype(BF16),
                                  (k * jnp.exp(b_mid - bcum)).astype(BF16),
                                  (((1,), (1,)), ((), ())), preferred_element_type=F32)
            att = jnp.where(col <= row, att, 0.0)
            o = jnp.dot(att.astype(BF16), v, preferred_element_type=F32)
            o = o + jnp.dot((q * jnp.exp(bcum)).astype(BF16), s_ref[hh].astype(BF16), preferred_element_type=F32)
            bcum_t = bcum.T
            b_last = bcum_t[:, C - 1:C]
            k_t = (k.T * jnp.exp(b_last - bcum_t)).astype(BF16)
            s_ref[hh] = s_ref[hh] * jnp.exp(b_last) + jnp.dot(k_t, v, preferred_element_type=F32)
            ms = jnp.mean(o * o, axis=-1, keepdims=True)
            o = o * lax.rsqrt(ms + NORM_EPS) * og_ref[...]
            o_ref[rows, vc] = (o * jax.nn.silu(g_ref[rows, vc].astype(F32))).astype(o_ref.dtype)


def gla_core(proj, a_pad, w_gate2, b_gate2, out_gain, B, T):
    H = GLA_HEADS
    n = proj.shape[0]
    dv = out_gain.shape[0]
    dk = w_gate2.shape[1] // H
    tb = min(GLA_STEP_TOKENS, T)
    nt = T // tb
    wg = jnp.pad(w_gate2, ((0, LANES - w_gate2.shape[0]), (0, 0))).astype(BF16)
    tri = jnp.asarray(np.tril(np.ones((GLA_BLOCK, GLA_BLOCK), np.float32)), BF16)
    hp = GLA_HEADS_PER_STEP
    hs = H // hp
    v0 = 2 * H * dk // (hp * dv)
    return pl.pallas_call(
        _gla_body,
        grid=(B, hs, nt),
        in_specs=[pl.BlockSpec((tb, hp * dk), lambda b, h, i: (b * nt + i, h)),
                  pl.BlockSpec((tb, hp * dk), lambda b, h, i: (b * nt + i, hs + h)),
                  pl.BlockSpec((tb, hp * dv), lambda b, h, i: (b * nt + i, v0 + h)),
                  pl.BlockSpec((tb, hp * dv), lambda b, h, i: (b * nt + i, v0 + hs + h)),
                  pl.BlockSpec((tb, LANES), lambda b, h, i: (b * nt + i, 0)),
                  pl.BlockSpec((LANES, hp * dk), lambda b, h, i: (0, h)),
                  pl.BlockSpec((1, hp * dk), lambda b, h, i: (0, h)),
                  pl.BlockSpec((1, dv), lambda b, h, i: (0, 0)),
                  pl.BlockSpec((GLA_BLOCK, GLA_BLOCK), lambda b, h, i: (0, 0))],
        out_specs=pl.BlockSpec((tb, hp * dv), lambda b, h, i: (b * nt + i, h)),
        out_shape=jax.ShapeDtypeStruct((n, H * dv), BF16),
        scratch_shapes=[pltpu.VMEM((hp, dk, dv), F32)],
        compiler_params=_params("arbitrary", "arbitrary", "arbitrary"),
        name="gla_core",
    )(proj, proj, proj, proj, a_pad, wg, b_gate2.reshape(1, -1).astype(F32), out_gain.reshape(1, dv).astype(F32), tri)


def _nsa_compress_body(u_ref, w1_ref, w2_ref, pe_ref, kg_ref, o_ref):
    kv = pl.program_id(0)
    u = u_ref[0, 0]
    half = u.shape[1]
    w1 = w1_ref[0]
    a = jnp.dot(u, w1[:half], preferred_element_type=F32)
    b = jnp.dot(u, w1[half:], preferred_element_type=F32)
    pe_term = jnp.dot(pe_ref[0], w1, preferred_element_type=F32)[0:1]
    n_chunks = u.shape[0]
    hid = jax.nn.gelu(a + pltpu.roll(b, n_chunks - 1, 0) + pe_term, approximate=True)
    out = jnp.dot(hid.astype(BF16), w2_ref[0], preferred_element_type=F32)
    ms = jnp.mean(out * out, axis=-1, keepdims=True)
    normed = out * lax.rsqrt(ms + NORM_EPS) * kg_ref[...]
    res = jnp.where(kv == 0, normed, out)
    row = lax.broadcasted_iota(I32, res.shape, 0)
    o_ref[0, 0, 0] = jnp.where(row < n_chunks - 1, res, 0.0).astype(o_ref.dtype)


def nsa_compress(zz, cmp_pe, cmp_w1, cmp_w2, kc_gain, B, T):
    G = NSA_GROUPS
    Dh = zz.shape[-1]
    n_chunks = T // CMP_STRIDE
    u = zz.reshape(zz.shape[0], B, n_chunks, CMP_STRIDE * Dh)
    w1 = cmp_w1.reshape(2, CMP_BLOCK * Dh, Dh).astype(BF16)
    pe = jnp.broadcast_to(cmp_pe.reshape(2, 1, CMP_BLOCK * Dh), (2, 8, CMP_BLOCK * Dh)).astype(BF16)
    return pl.pallas_call(
        _nsa_compress_body,
        grid=(2, B, G),
        in_specs=[pl.BlockSpec((1, 1, n_chunks, CMP_STRIDE * Dh), lambda kv, b, g: (kv * G + g, b, 0, 0)),
                  pl.BlockSpec((1, CMP_BLOCK * Dh, Dh), lambda kv, b, g: (kv, 0, 0)),
                  pl.BlockSpec((1, Dh, Dh), lambda kv, b, g: (kv, 0, 0)),
                  pl.BlockSpec((1, 8, CMP_BLOCK * Dh), lambda kv, b, g: (kv, 0, 0)),
                  pl.BlockSpec((1, Dh), lambda kv, b, g: (0, 0))],
        out_specs=pl.BlockSpec((1, 1, 1, n_chunks, Dh), lambda kv, b, g: (kv, b, g, 0, 0)),
        out_shape=jax.ShapeDtypeStruct((2, B, G, n_chunks, Dh), BF16),
        compiler_params=_params("arbitrary", "arbitrary", "arbitrary"),
        name="nsa_compress",
    )(u, w1, cmp_w2.astype(BF16), pe, kc_gain.reshape(1, Dh).astype(F32))


def _dot_nt(a, b):
    return lax.dot_general(a, b, (((1,), (1,)), ((), ())), preferred_element_type=F32)


def _col_softmax_terms(s):
    m = jnp.max(s, axis=0, keepdims=True)
    m = jnp.where(m == NEG_INF, 0.0, m)
    e = jnp.exp2(s - m)
    return e, 1.0 / jnp.maximum(jnp.sum(e, axis=0, keepdims=True), 1e-30)


def _tile_lanes(x, n):
    return jnp.concatenate([x] * n, axis=1)


def _nsa_attn_body(q_ref, qaug_ref, glog_ref, kc_ref, vct_ref, ks_ref, vst_ref, kw_ref, vwt_ref,
                   mselt_ref, efullt_ref, o_ref, q2_ref, acct_ref, m_ref, l_ref, oacct_ref, s_ref, sc_ref, sw_ref,
                   sel_ref):
    qb = pl.program_id(2)
    Dh = LANES
    RB = NSA_ROW_BLOCK
    hpb = RB // Q_BLOCK
    n_rb = NSA_HG // hpb
    n_cmp_pad = kc_ref.shape[2]
    n_slc = mselt_ref.shape[0]
    win_keys = WINDOW + Q_BLOCK
    s0 = qb * Q_BLOCK
    t_q = s0 + lax.broadcasted_iota(I32, (1, Q_BLOCK), 1)

    gates_t = jax.nn.sigmoid(glog_ref[0]).T

    for hg in range(NSA_HG):
        q2_ref[hg * Q_BLOCK:(hg + 1) * Q_BLOCK, :Dh] = q_ref[0, :, hg * Dh:(hg + 1) * Dh]
    q2_ref[:, Dh:] = qaug_ref[0]

    cmp_end = lax.broadcasted_iota(I32, (n_cmp_pad, 1), 0) * CMP_STRIDE + (CMP_BLOCK - 1)
    cmp_mask = _tile_lanes(jnp.where(cmp_end <= t_q, 0.0, NEG_INF), hpb)

    ws = pl.multiple_of(jnp.maximum(s0 - WINDOW, 0), Q_BLOCK)
    wdist = t_q - (ws + lax.broadcasted_iota(I32, (win_keys, 1), 0))
    win_mask = _tile_lanes(jnp.where((wdist >= 0) & (wdist < WINDOW), 0.0, NEG_INF), hpb)

    kc = kc_ref[0, 0]
    vct = jnp.concatenate([vct_ref[0, 0, j] for j in range(n_cmp_pad // Dh)], axis=1)
    kwin = kw_ref[0, 0, pl.ds(ws, win_keys), :]
    wblk = ws // Dh
    vwt = jnp.concatenate([vwt_ref[0, 0, wblk + j] for j in range(win_keys // Dh)], axis=1)

    imp_t = jnp.zeros((n_cmp_pad, Q_BLOCK), F32)
    for rb in range(n_rb):
        q_rb = q2_ref[rb * RB:(rb + 1) * RB, :]
        sc_ref[rb] = _dot_nt(kc, q_rb) + cmp_mask
        sw_ref[rb] = _dot_nt(kwin, q_rb) + win_mask
    for rb in range(n_rb):
        e, inv = _col_softmax_terms(sc_ref[rb])
        p = e * inv
        for j in range(hpb):
            imp_t = imp_t + p[:, j * Q_BLOCK:(j + 1) * Q_BLOCK]
        o_c = jnp.dot(vct, p.astype(BF16), preferred_element_type=F32)
        e, inv = _col_softmax_terms(sw_ref[rb])
        o_w = jnp.dot(vwt, e.astype(BF16), preferred_element_type=F32) * inv
        for j in range(hpb):
            hg = rb * hpb + j
            sub = slice(j * Q_BLOCK, (j + 1) * Q_BLOCK)
            acct_ref[:, hg * Q_BLOCK:(hg + 1) * Q_BLOCK] = (
                gates_t[hg:hg + 1, :] * o_c[:, sub] + gates_t[2 * NSA_HG + hg:2 * NSA_HG + hg + 1, :] * o_w[:, sub])

    hi, mid, lo = _split3(imp_t)
    mselt = mselt_ref[...]
    slc_imp = (jnp.dot(mselt, hi, preferred_element_type=F32) + jnp.dot(mselt, mid, preferred_element_type=F32)
               + jnp.dot(mselt, lo, preferred_element_type=F32))

    blk = lax.broadcasted_iota(I32, (n_slc, 1), 0)
    cur = lax.shift_right_logical(t_q, 6)
    forced = (blk == 0) | (blk == cur) | (blk == cur - 1)
    score = jnp.where(forced, jnp.inf, jnp.where(blk <= cur, slc_imp, NEG_INF))
    sel = jnp.zeros((n_slc, Q_BLOCK), F32)
    for _ in range(SLC_TOPK):
        m = jnp.max(score, axis=0, keepdims=True)
        cand = (score == m) & (m > NEG_INF)
        first = jnp.min(jnp.where(cand, blk, n_slc), axis=0, keepdims=True)
        one = blk == first
        sel = jnp.where(one, 1.0, sel)
        score = jnp.where(one, NEG_INF, score)
    sel_b = sel.astype(BF16)
    sel_ref[...] = sel

    m_ref[...] = jnp.full(m_ref.shape, NEG_INF, F32)
    l_ref[...] = jnp.zeros(l_ref.shape, F32)
    oacct_ref[...] = jnp.zeros(oacct_ref.shape, F32)
    KC = SLC_KEY_CHUNK
    key_iota = lax.broadcasted_iota(I32, (KC, 1), 0)

    def chunk_step(c, carry):
        k0 = pl.multiple_of(c * KC, KC)
        bpc = KC // SLC_BLOCK
        picked = sel_ref[pl.ds(pl.multiple_of(c * bpc, bpc), bpc), :]
        any_picked = jnp.max(jnp.max(picked, axis=0, keepdims=True), axis=1, keepdims=True)[0, 0]

        @pl.when(any_picked > 0.5)
        def _():
            kch = ks_ref[0, 0, pl.ds(k0, KC), :]
            vt = jnp.concatenate([vst_ref[0, 0, c * (KC // Dh) + j] for j in range(KC // Dh)], axis=1)
            selk = jnp.dot(efullt_ref[pl.ds(k0, KC), :], sel_b, preferred_element_type=F32)
            mask = _tile_lanes(jnp.where((selk > 0.5) & (k0 + key_iota <= t_q), 0.0, NEG_INF), hpb)
            for rb in range(n_rb):
                s_ref[rb] = _dot_nt(kch, q2_ref[rb * RB:(rb + 1) * RB, :]) + mask
            for rb in range(n_rb):
                cols = slice(rb * RB, (rb + 1) * RB)
                s = s_ref[rb]
                m_old = m_ref[rb]
                m_new = jnp.maximum(m_old, jnp.max(s, axis=0, keepdims=True))
                m_safe = jnp.where(m_new == NEG_INF, 0.0, m_new)
                alpha = jnp.exp2(m_old - m_safe)
                p = jnp.exp2(s - m_safe)
                l_ref[rb] = alpha * l_ref[rb] + jnp.sum(p, axis=0, keepdims=True)
                oacct_ref[:, cols] = (alpha * oacct_ref[:, cols]
                                      + jnp.dot(vt, p.astype(BF16), preferred_element_type=F32))
                m_ref[rb] = m_new
        return carry

    n_chunks = (s0 + Q_BLOCK + KC - 1) // KC
    lax.fori_loop(0, n_chunks, chunk_step, 0)
    for hg in range(NSA_HG):
        rb, j = divmod(hg, hpb)
        cols = slice(hg * Q_BLOCK, (hg + 1) * Q_BLOCK)
        inv = 1.0 / jnp.maximum(l_ref[rb][:, j * Q_BLOCK:(j + 1) * Q_BLOCK], 1e-30)
        o_t = acct_ref[:, cols] + gates_t[NSA_HG + hg:NSA_HG + hg + 1, :] * (oacct_ref[:, cols] * inv)
        o_ref[0, :, hg * Dh:(hg + 1) * Dh] = o_t.T.astype(o_ref.dtype)


def _pos_pieces(pos):
    pos = np.asarray(pos)
    out = np.zeros((pos.shape[0], LANES), np.float32)
    for i in range(3):
        out[:, 2 * i] = 64 * (pos // 64)
        out[:, 2 * i + 1] = pos % 64
    return jnp.asarray(out, BF16)


def _slope_pieces():
    H, G, HG = NSA_HEADS, NSA_GROUPS, NSA_HG
    slopes = jnp.asarray(LOG2E * 2.0 ** (-8.0 * np.arange(1, H + 1, dtype=np.float64) / H), F32)
    pieces = jnp.stack(_split3(slopes), axis=-1)
    cols = jnp.repeat(pieces, 2, axis=-1)
    cols = jnp.pad(cols, ((0, 0), (0, LANES - cols.shape[-1])))
    return jnp.repeat(cols.reshape(G, HG, 1, LANES), Q_BLOCK, axis=2).reshape(G, HG * Q_BLOCK, LANES)


def _blocked_transpose(v):
    lead, (t, dh) = v.shape[:-2], v.shape[-2:]
    return jnp.swapaxes(v.reshape(*lead, t // LANES, LANES, dh), -1, -2)


def nsa_attention(q, glog, kvc, zz, B, T):
    G, HG, Dh = NSA_GROUPS, NSA_HG, LANES
    H = NSA_HEADS
    n_cmp = T // CMP_STRIDE - CMP_BLOCK // CMP_STRIDE + 1
    n_cmp_pad = -(-T // CMP_STRIDE // LANES) * LANES
    n_slc = T // SLC_BLOCK
    per = SLC_BLOCK // CMP_STRIDE
    n = np.arange(n_cmp_pad)[None, :]
    j = np.arange(n_slc)[:, None]
    mselt = ((n // per == j).astype(np.float32) + ((n + 1) // per == j).astype(np.float32))
    mselt[:, n_cmp:] = 0.0
    kvc = jnp.pad(kvc, ((0, 0), (0, 0), (0, 0), (0, n_cmp_pad - kvc.shape[3]), (0, 0)))
    efullt = (np.arange(T)[:, None] // SLC_BLOCK == np.arange(n_slc)[None, :]).astype(np.float32)

    key_aug = jnp.broadcast_to(_pos_pieces(np.arange(T)), (G, B, T, LANES))
    cmp_aug = jnp.broadcast_to(_pos_pieces(np.arange(n_cmp_pad) * CMP_STRIDE + CMP_BLOCK - 1),
                               (B, G, n_cmp_pad, LANES))
    kc_aug = jnp.concatenate([kvc[0], cmp_aug], axis=-1)
    vct = _blocked_transpose(kvc[1])
    ks_aug = jnp.concatenate([zz[2 * G:3 * G], key_aug], axis=-1)
    kw_aug = jnp.concatenate([zz[4 * G:5 * G], key_aug], axis=-1)
    vst = _blocked_transpose(zz[3 * G:4 * G])
    vwt = _blocked_transpose(zz[5 * G:6 * G])

    def k_spec():
        return pl.BlockSpec((1, 1, T, 2 * Dh), lambda b, g, i: (g, b, 0, 0))

    def vt_spec():
        return pl.BlockSpec((1, 1, T // LANES, Dh, LANES), lambda b, g, i: (g, b, 0, 0, 0))

    n_rb = HG * Q_BLOCK // NSA_ROW_BLOCK
    return pl.pallas_call(
        _nsa_attn_body,
        grid=(B, G, T // Q_BLOCK),
        in_specs=[pl.BlockSpec((1, Q_BLOCK, HG * Dh), lambda b, g, i: (b, i, g)),
                  pl.BlockSpec((1, HG * Q_BLOCK, LANES), lambda b, g, i: (g, 0, 0)),
                  pl.BlockSpec((1, Q_BLOCK, LANES), lambda b, g, i: (b, i, g)),
                  pl.BlockSpec((1, 1, n_cmp_pad, 2 * Dh), lambda b, g, i: (b, g, 0, 0)),
                  pl.BlockSpec((1, 1, n_cmp_pad // LANES, Dh, LANES), lambda b, g, i: (b, g, 0, 0, 0)),
                  k_spec(), vt_spec(), k_spec(), vt_spec(),
                  pl.BlockSpec((n_slc, n_cmp_pad), lambda b, g, i: (0, 0)),
                  pl.BlockSpec((T, n_slc), lambda b, g, i: (0, 0))],
        out_specs=pl.BlockSpec((1, Q_BLOCK, HG * Dh), lambda b, g, i: (b, i, g)),
        out_shape=jax.ShapeDtypeStruct((B, T, H * Dh), BF16),
        scratch_shapes=[pltpu.VMEM((HG * Q_BLOCK, 2 * Dh), BF16),
                        pltpu.VMEM((Dh, HG * Q_BLOCK), F32),
                        pltpu.VMEM((n_rb, 1, NSA_ROW_BLOCK), F32),
                        pltpu.VMEM((n_rb, 1, NSA_ROW_BLOCK), F32),
                        pltpu.VMEM((Dh, HG * Q_BLOCK), F32),
                        pltpu.VMEM((n_rb, SLC_KEY_CHUNK, NSA_ROW_BLOCK), F32),
                        pltpu.VMEM((n_rb, n_cmp_pad, NSA_ROW_BLOCK), F32),
                        pltpu.VMEM((n_rb, WINDOW + Q_BLOCK, NSA_ROW_BLOCK), F32),
                        pltpu.VMEM((n_slc, Q_BLOCK), F32)],
        compiler_params=_params("arbitrary", "arbitrary", "arbitrary"),
        name="nsa_attention",
    )(q, _slope_pieces(), glog, kc_aug, vct, ks_aug, vst, kw_aug, vwt,
      jnp.asarray(mselt, BF16), jnp.asarray(efullt, BF16))


def _pack_rows(y, o_ref):
    m, d = y.shape
    bits = lax.bitcast_convert_type(y.astype(BF16).astype(F32), U32)
    for c in range(d // (2 * LANES)):
        lo = lax.shift_right_logical(bits[:, c * LANES:(c + 1) * LANES], jnp.uint32(16))
        hi = bits[:, d // 2 + c * LANES:d // 2 + (c + 1) * LANES]
        o_ref[pl.ds(c, m, stride=d // (2 * LANES)), :] = lo | hi


def _unpack_chunk(words):
    lo = lax.bitcast_convert_type(lax.shift_left(words, jnp.uint32(16)), F32)
    hi = lax.bitcast_convert_type(words & jnp.uint32(0xFFFF0000), F32)
    return lo, hi


def _moe_router_body(h_ref, g_ref, w_ref, b_ref, xp_ref, idx_ref, wgt_ref):
    x = h_ref[...]
    ms = jnp.mean(x * x, axis=-1, keepdims=True)
    xn = x * lax.rsqrt(ms + NORM_EPS) * g_ref[...]
    _pack_rows(xn, xp_ref)
    logits = jnp.dot(xn.astype(BF16), w_ref[...], preferred_element_type=F32) + b_ref[...]
    lane = lax.broadcasted_iota(I32, logits.shape, 1)
    logits = jnp.where(lane < N_EXPERTS, logits, NEG_INF)
    idx_out = jnp.zeros(logits.shape, I32)
    val_out = jnp.full(logits.shape, NEG_INF, F32)
    for k in range(TOP_K):
        m = jnp.max(logits, axis=-1, keepdims=True)
        first = jnp.min(jnp.where(logits == m, lane, LANES), axis=-1, keepdims=True)
        idx_out = jnp.where(lane == k, first, idx_out)
        val_out = jnp.where(lane == k, m, val_out)
        logits = jnp.where(lane == first, NEG_INF, logits)
    e = jnp.exp(val_out - jnp.max(val_out, axis=-1, keepdims=True))
    idx_ref[...] = idx_out
    wgt_ref[...] = e * (1.0 / jnp.sum(e, axis=-1, keepdims=True))


def moe_router(h, gain, w_router, b_router, tm=256):
    n, d = h.shape
    tm = min(tm, n)
    pr = d // (2 * LANES)
    w = _pad_cols(w_router).astype(BF16)
    b = _pad_cols(b_router.reshape(1, -1)).astype(F32)
    return pl.pallas_call(
        _moe_router_body,
        grid=(n // tm,),
        in_specs=[pl.BlockSpec((tm, d), lambda i: (i, 0)),
                  pl.BlockSpec((1, d), lambda i: (0, 0)),
                  pl.BlockSpec((d, LANES), lambda i: (0, 0)),
                  pl.BlockSpec((1, LANES), lambda i: (0, 0))],
        out_specs=[pl.BlockSpec((tm * pr, LANES), lambda i: (i, 0)),
                   pl.BlockSpec((tm, LANES), lambda i: (i, 0)),
                   pl.BlockSpec((tm, LANES), lambda i: (i, 0))],
        out_shape=[jax.ShapeDtypeStruct((n * pr, LANES), U32),
                   jax.ShapeDtypeStruct((n, LANES), I32), jax.ShapeDtypeStruct((n, LANES), F32)],
        compiler_params=_params("arbitrary"),
        name="moe_router",
    )(h, gain.reshape(1, d).astype(F32), w, b)


def _row_gather_copy(src_hbm, src_row, dst_buf, slot, r, sem, pr):
    return pltpu.make_async_copy(src_hbm.at[pl.ds(pl.multiple_of(src_row, pr), pr)],
                                 dst_buf.at[slot, pl.ds(pl.multiple_of(r * pr, pr), pr)], sem.at[slot])


def _start_row_gather(ids_ref, src_hbm, dst_buf, slot, sem, n_rows, pr):
    def body(r, carry):
        _row_gather_copy(src_hbm, ids_ref[0, 0, r], dst_buf, slot, r, sem, pr).start(priority=1)
        return carry
    lax.fori_loop(0, n_rows, body, 0, unroll=8)


def _wait_row_gather(src_hbm, dst_buf, slot, sem):
    pltpu.make_async_copy(src_hbm.at[pl.ds(0, dst_buf.shape[1])], dst_buf.at[slot], sem.at[slot]).wait()


def _moe_expert_body(blk_e_ref, n_used_ref, ids0_ref, idsn_ref, x_hbm, wgu_ref, bgu_ref, wd_ref, bd_ref,
                     y_ref, xbuf, xs, sem):
    i = pl.program_id(0)
    n_used = n_used_ref[0]
    rows, d = xs.shape
    pr = d // (2 * LANES)
    ff = wd_ref.shape[2]

    @pl.when(i == 0)
    def _():
        _start_row_gather(ids0_ref, x_hbm, xbuf, 0, sem, rows, pr)

    @pl.when(i + 1 < n_used)
    def _():
        _start_row_gather(idsn_ref, x_hbm, xbuf, (i + 1) % 2, sem, rows, pr)

    @pl.when(i < n_used)
    def _():
        slot = i % 2
        _wait_row_gather(x_hbm, xbuf, slot, sem)
        sub = rows // MOE_SUB_BLOCKS
        for sb in range(MOE_SUB_BLOCKS):
            r0 = sb * sub
            for c in range(pr):
                lo, hi = _unpack_chunk(xbuf[slot, pl.ds(r0 * pr + c, sub, stride=pr), :])
                xs[r0:r0 + sub, c * LANES:(c + 1) * LANES] = lo.astype(BF16)
                xs[r0:r0 + sub, d // 2 + c * LANES:d // 2 + (c + 1) * LANES] = hi.astype(BF16)
            gu = jnp.dot(xs[r0:r0 + sub, :], wgu_ref[0, 0], preferred_element_type=F32) + bgu_ref[0]
            gate = jnp.minimum(gu[:, :ff], SWIGLU_LIMIT)
            up = jnp.clip(gu[:, ff:], -SWIGLU_LIMIT, SWIGLU_LIMIT)
            act = (up + 1.0) * (gate * jax.nn.sigmoid(SWIGLU_ALPHA * gate))
            y = jnp.dot(act.astype(BF16), wd_ref[0, 0], preferred_element_type=F32) + bd_ref[0]
            _pack_rows(y, y_ref.at[pl.ds(r0 * pr, sub * pr), :])

    @pl.when(i >= n_used)
    def _():
        y_ref[...] = jnp.zeros(y_ref.shape, y_ref.dtype)


def moe_experts(xp, tok_rows, blk_e, n_used, w_gate_up, b_gate_up, w_down, b_down, l):
    _, E, d, ff2 = w_gate_up.shape
    ff = ff2 // 2
    pr = d // (2 * LANES)
    rows = MOE_BLOCK_ROWS
    n_blocks = tok_rows.shape[0] // rows
    ids = tok_rows.reshape(n_blocks, 1, rows)

    def used(i, nu):
        return jnp.minimum(i, nu[0] - 1)

    grid_spec = pltpu.PrefetchScalarGridSpec(
        num_scalar_prefetch=2,
        grid=(n_blocks,),
        in_specs=[pl.BlockSpec((1, 1, rows), lambda i, be, nu: (0, 0, 0), memory_space=pltpu.SMEM),
                  pl.BlockSpec((1, 1, rows), lambda i, be, nu: (jnp.minimum(i + 1, n_blocks - 1), 0, 0),
                               memory_space=pltpu.SMEM),
                  pl.BlockSpec(memory_space=pl.ANY),
                  pl.BlockSpec((1, 1, d, ff2), lambda i, be, nu: (l, be[used(i, nu)], 0, 0)),
                  pl.BlockSpec((1, 1, ff2), lambda i, be, nu: (be[used(i, nu)], 0, 0)),
                  pl.BlockSpec((1, 1, ff, d), lambda i, be, nu: (l, be[used(i, nu)], 0, 0)),
                  pl.BlockSpec((1, 1, d), lambda i, be, nu: (be[used(i, nu)], 0, 0))],
        out_specs=pl.BlockSpec((rows * pr, LANES), lambda i, be, nu: (i, 0)),
        scratch_shapes=[pltpu.VMEM((2, rows * pr, LANES), U32), pltpu.VMEM((rows, d), BF16),
                        pltpu.SemaphoreType.DMA((2,))],
    )
    return pl.pallas_call(
        _moe_expert_body,
        grid_spec=grid_spec,
        out_shape=jax.ShapeDtypeStruct((n_blocks * rows * pr, LANES), U32),
        compiler_params=_params("arbitrary"),
        name="moe_experts",
    )(blk_e, n_used, ids, ids, xp, w_gate_up, b_gate_up.reshape(E, 1, ff2).astype(F32),
      w_down, b_down.reshape(E, 1, d).astype(F32))


def _moe_combine_body(ids0_ref, idsn_ref, y_hbm, h_ref, w_ref, g_ref, o_ref, xn_ref, ybuf, sem):
    i = pl.program_id(0)
    n_steps = pl.num_programs(0)
    tt, d = h_ref.shape
    pr = d // (2 * LANES)
    rows = TOP_K * tt

    @pl.when(i == 0)
    def _():
        _start_row_gather(ids0_ref, y_hbm, ybuf, 0, sem, rows, pr)

    @pl.when(i + 1 < n_steps)
    def _():
        _start_row_gather(idsn_ref, y_hbm, ybuf, (i + 1) % 2, sem, rows, pr)

    slot = i % 2
    _wait_row_gather(y_hbm, ybuf, slot, sem)
    w = w_ref[...]
    wk = [jnp.broadcast_to(w[:, k:k + 1], (tt, LANES)) for k in range(TOP_K)]
    ssq = jnp.zeros((tt, LANES), F32)
    for c in range(pr):
        lo_cols = slice(c * LANES, (c + 1) * LANES)
        hi_cols = slice(d // 2 + c * LANES, d // 2 + (c + 1) * LANES)
        acc_lo = h_ref[:, lo_cols]
        acc_hi = h_ref[:, hi_cols]
        for k in range(TOP_K):
            lo, hi = _unpack_chunk(ybuf[slot, pl.ds(k * tt * pr + c, tt, stride=pr), :])
            acc_lo = acc_lo + wk[k] * lo
            acc_hi = acc_hi + wk[k] * hi
        o_ref[:, lo_cols] = acc_lo
        o_ref[:, hi_cols] = acc_hi
        ssq = ssq + acc_lo * acc_lo + acc_hi * acc_hi
    inv = lax.rsqrt(jnp.sum(ssq, axis=-1, keepdims=True) * (1.0 / d) + NORM_EPS)
    xn_ref[...] = (o_ref[...] * inv * g_ref[...]).astype(xn_ref.dtype)


def moe_combine(yp, slot_rows, weights, h, next_gain):
    n, d = h.shape
    pr = d // (2 * LANES)
    tt = min(MOE_COMBINE_TOKENS, n)
    n_steps = n // tt
    ids = slot_rows.reshape(n_steps, tt, TOP_K).transpose(0, 2, 1).reshape(n_steps, 1, TOP_K * tt)
    return pl.pallas_call(
        _moe_combine_body,
        grid=(n_steps,),
        in_specs=[pl.BlockSpec((1, 1, TOP_K * tt), lambda i: (0, 0, 0), memory_space=pltpu.SMEM),
                  pl.BlockSpec((1, 1, TOP_K * tt), lambda i: (jnp.minimum(i + 1, n_steps - 1), 0, 0),
                               memory_space=pltpu.SMEM),
                  pl.BlockSpec(memory_space=pl.ANY),
                  pl.BlockSpec((tt, d), lambda i: (i, 0)),
                  pl.BlockSpec((tt, LANES), lambda i: (i, 0)),
                  pl.BlockSpec((1, d), lambda i: (0, 0))],
        out_specs=[pl.BlockSpec((tt, d), lambda i: (i, 0)),
                   pl.BlockSpec((tt, d), lambda i: (i, 0))],
        out_shape=[jax.ShapeDtypeStruct((n, d), F32), jax.ShapeDtypeStruct((n, d), BF16)],
        scratch_shapes=[pltpu.VMEM((2, TOP_K * tt * pr, LANES), U32), pltpu.SemaphoreType.DMA((2,))],
        compiler_params=_params("arbitrary"),
        name="moe_combine",
    )(ids, ids, yp, h, weights, next_gain.reshape(1, d).astype(F32))


def _moe_plan(top_idx, n_rows):
    E, rows = N_EXPERTS, MOE_BLOCK_ROWS
    flat_e = top_idx.reshape(-1)
    nk = flat_e.shape[0]
    onehot = (flat_e[:, None] == jnp.arange(E, dtype=I32)[None, :]).astype(I32)
    csum = jnp.cumsum(onehot, axis=0)
    rank = jnp.sum(onehot * csum, axis=1) - 1
    sizes = csum[-1]
    nblk = (sizes + rows - 1) // rows
    blk_end = jnp.cumsum(nblk)
    pad_start = (blk_end - nblk) * rows
    slot_dest = pad_start[flat_e] + rank
    tok_buf = jnp.zeros((n_rows,), I32).at[slot_dest].set(jnp.arange(nk, dtype=I32) // TOP_K)
    n_blocks = n_rows // rows
    blk_e = jnp.minimum(jnp.searchsorted(blk_end, jnp.arange(n_blocks, dtype=I32), side='right'), E - 1).astype(I32)
    n_used = blk_end[-1:].astype(I32)
    return slot_dest, tok_buf, blk_e, n_used


def moe_layer(h, gain, w_router, b_router, w_gate_up, b_gate_up, w_down, b_down, l, next_gain):
    n, d = h.shape
    pr = d // (2 * LANES)
    rows = MOE_BLOCK_ROWS
    xp, top_idx, weights = moe_router(h, gain, w_router, b_router)
    n_rows = n * TOP_K + N_EXPERTS * rows
    slot_dest, tok_buf, blk_e, n_used = _moe_plan(top_idx[:, :TOP_K], n_rows)
    yp = moe_experts(xp, tok_buf * pr, blk_e, n_used, w_gate_up, b_gate_up, w_down, b_down, l)
    return moe_combine(yp, slot_dest.reshape(n, TOP_K) * pr, weights, h, next_gain)


def _gla_mixer(h, xn, w_in, w_gate2, b_gate2, out_gain, w_out, l, B, T):
    n_main = w_in.shape[2] - GLA_GATE_RANK
    w_in_t = jnp.swapaxes(w_in, 1, 2)
    proj = matmul(xn, w_in_t, layer=l, n=n_main, w_transposed=True, out_dtype=BF16)
    wa_t = jnp.pad(w_in_t[l, n_main:], ((0, LANES - GLA_GATE_RANK), (0, 0)))
    a_pad = matmul(xn, wa_t[None], w_transposed=True)
    o = gla_core(proj, a_pad, w_gate2, b_gate2, out_gain, B, T)
    return matmul(o, w_out, layer=l, residual=h)


def _nsa_shared_kv(h, kv_gain, w_kv, k_gain, cmp_pe, cmp_w1, cmp_w2, B, T):
    G = NSA_GROUPS
    Dh = LANES
    xn = rmsnorm(h, kv_gain)
    ones = jnp.ones((G * Dh,), F32)
    gain_cols = jnp.concatenate([ones, ones, jnp.tile(k_gain[1], G), ones, jnp.tile(k_gain[2], G), ones])
    norm_groups = [False] * (2 * G) + [True] * G + [False] * G + [True] * G + [False] * G
    zz = matmul_groupnorm(xn, w_kv.astype(BF16)[None], gain_cols, norm_groups, split_out=True, tn=6 * G * Dh)
    kvc = nsa_compress(zz, cmp_pe, cmp_w1, cmp_w2, k_gain[0], B, T)
    return kvc, zz.reshape(6 * G, B, T, Dh)


def _nsa_mixer(h, xn, w_in, q_gain, w_out, kvc, zz, j, B, T):
    H, G, HG, Dh = NSA_HEADS, NSA_GROUPS, NSA_HG, LANES
    w_in_t = jnp.swapaxes(w_in, 1, 2)
    q = matmul_groupnorm(xn, w_in_t, jnp.tile(q_gain, H), [True] * 4, layer=j, n=H * Dh, w_transposed=True,
                         scale=Dh ** -0.5 * LOG2E, tm=1024, tn=512)
    wg = w_in_t[j, H * Dh:].reshape(G, HG, 3, -1).transpose(0, 2, 1, 3).reshape(G, 3 * HG, -1)
    wg = jnp.pad(wg, ((0, 0), (0, LANES - 3 * HG), (0, 0))).reshape(1, G * LANES, -1)
    glog = matmul(xn, wg, w_transposed=True)
    o = nsa_attention(q.reshape(B, T, H * Dh), glog.reshape(B, T, G * LANES), kvc, zz, B, T)
    return matmul(o.reshape(B * T, H * Dh), w_out, layer=j, residual=h)


def kernel(x, ln_mix, ln_ffn, a_w_in, a_w_gate2, a_b_gate2, a_out_gain, a_w_out, kv_gain, w_kv, k_gain,
           cmp_pe, cmp_w1, cmp_w2, b_w_in, b_q_gain, b_w_out, w_router, b_router, w_gate_up, b_gate_up,
           w_down, b_down):
    B, T, D = x.shape
    depth = ln_mix.shape[0]
    n_a = a_w_in.shape[0]
    h = x.reshape(B * T, D)
    shared = None
    wgu_bf = w_gate_up.astype(BF16)
    wd_bf = w_down.astype(BF16)
    xn = rmsnorm(h, ln_mix[0])
    for l in range(depth):
        if l < n_a:
            h = _gla_mixer(h, xn, a_w_in, a_w_gate2[l], a_b_gate2[l], a_out_gain[l], a_w_out, l, B, T)
        else:
            if l == n_a:
                shared = _nsa_shared_kv(h, kv_gain, w_kv, k_gain, cmp_pe, cmp_w1, cmp_w2, B, T)
            j = l - n_a
            h = _nsa_mixer(h, xn, b_w_in, b_q_gain[j], b_w_out, *shared, j, B, T)
        h, xn = moe_layer(h, ln_ffn[l], w_router[l], b_router[l], wgu_bf, b_gate_up[l], wd_bf, b_down[l], l,
                          ln_mix[min(l + 1, depth - 1)])
    return h.reshape(B, T, D)
```

```python
import functools
import math

import jax
import jax.numpy as jnp
from jax import lax
import numpy as np
from jax.experimental import pallas as pl
from jax.experimental.pallas import tpu as pltpu

F32 = jnp.float32
BF16 = jnp.bfloat16
I32 = jnp.int32
U32 = jnp.uint32

NORM_EPS = 1e-5
GLA_HEADS = 8
GLA_GATE_RANK = 16
GLA_GATE_TAU = 16.0
NSA_HEADS = 32
NSA_GROUPS = 2
NSA_HG = NSA_HEADS // NSA_GROUPS
CMP_BLOCK = 32
CMP_STRIDE = 16
SLC_BLOCK = 64
SLC_TOPK = 8
WINDOW = 512
Q_BLOCK = 128
N_EXPERTS = 32
TOP_K = 4
SWIGLU_ALPHA = 1.702
SWIGLU_LIMIT = 7.0

LANES = 128
MXU_DIM = 256
VMEM_LIMIT_BYTES = 56 * 1024 * 1024
MOE_BLOCK_ROWS = 512
MOE_SUB_BLOCKS = 4
MOE_COMBINE_TOKENS = 128
SLC_KEY_CHUNK = 512
NSA_ROW_BLOCK = 256
GLA_BLOCK = 128
GLA_STEP_TOKENS = 512
GLA_HEADS_PER_STEP = 2
NEG_INF = float("-inf")
LOG2E = math.log2(math.e)


def _params(*sem):
    return pltpu.CompilerParams(dimension_semantics=sem, vmem_limit_bytes=VMEM_LIMIT_BYTES)


def _split3(x):
    hi = x.astype(BF16)
    r1 = x - hi.astype(F32)
    mid = r1.astype(BF16)
    lo = (r1 - mid.astype(F32)).astype(BF16)
    return hi, mid, lo


def _rmsnorm_body(x_ref, g_ref, o_ref):
    x = x_ref[...].astype(F32)
    ms = jnp.mean(x * x, axis=-1, keepdims=True)
    o_ref[...] = (x * lax.rsqrt(ms + NORM_EPS) * g_ref[...].astype(F32)).astype(o_ref.dtype)


def rmsnorm(x, gain, out_dtype=BF16, tm=512):
    m, d = x.shape
    tm = min(tm, m)
    return pl.pallas_call(
        _rmsnorm_body,
        grid=(m // tm,),
        in_specs=[pl.BlockSpec((tm, d), lambda i: (i, 0)),
                  pl.BlockSpec((1, d), lambda i: (0, 0))],
        out_specs=pl.BlockSpec((tm, d), lambda i: (i, 0)),
        out_shape=jax.ShapeDtypeStruct((m, d), out_dtype),
        compiler_params=_params("arbitrary"),
        name="rmsnorm",
    )(x, gain.reshape(1, d))


def _tile_dot(a_ref, w_ref, w_transposed):
    w = w_ref[0].astype(BF16)
    dims = (((1,), (1,)), ((), ())) if w_transposed else (((1,), (0,)), ((), ()))
    return lax.dot_general(a_ref[...], w, dims, preferred_element_type=F32)


def _w_spec(k, tn, layer, w_transposed):
    if w_transposed:
        return pl.BlockSpec((1, tn, k), lambda i, j: (layer, j, 0))
    return pl.BlockSpec((1, k, tn), lambda i, j: (layer, 0, j))


def _matmul_body(a_ref, w_ref, o_ref, *, w_transposed):
    o_ref[...] = _tile_dot(a_ref, w_ref, w_transposed).astype(o_ref.dtype)


def _matmul_res_body(a_ref, w_ref, r_ref, o_ref, *, w_transposed):
    o_ref[...] = (r_ref[...].astype(F32) + _tile_dot(a_ref, w_ref, w_transposed)).astype(o_ref.dtype)


def matmul(a, w, layer=0, n=None, residual=None, w_transposed=False, out_dtype=F32, tm=1024, tn=512):
    m, k = a.shape
    n = w.shape[1 if w_transposed else 2] if n is None else n
    tm = min(tm, m)
    tn = min(tn, n)
    assert m % tm == 0 and n % tn == 0, (m, n, tm, tn)
    in_specs = [pl.BlockSpec((tm, k), lambda i, j: (i, 0)), _w_spec(k, tn, layer, w_transposed)]
    args = [a, w]
    body = functools.partial(_matmul_body, w_transposed=w_transposed)
    if residual is not None:
        in_specs.append(pl.BlockSpec((tm, tn), lambda i, j: (i, j)))
        args.append(residual)
        body = functools.partial(_matmul_res_body, w_transposed=w_transposed)
    return pl.pallas_call(
        body,
        grid=(m // tm, n // tn),
        in_specs=in_specs,
        out_specs=pl.BlockSpec((tm, tn), lambda i, j: (i, j)),
        out_shape=jax.ShapeDtypeStruct((m, n), out_dtype),
        compiler_params=_params("arbitrary", "arbitrary"),
        name="matmul",
    )(*args)


def _matmul_groupnorm_body(a_ref, w_ref, g_ref, o_ref, *, norm_groups, scale, split_out, w_transposed):
    acc = _tile_dot(a_ref, w_ref, w_transposed)
    for c, do_norm in enumerate(norm_groups):
        seg = acc[:, c * LANES:(c + 1) * LANES]
        if do_norm:
            ms = jnp.mean(seg * seg, axis=-1, keepdims=True)
            seg = seg * lax.rsqrt(ms + NORM_EPS) * g_ref[:, c * LANES:(c + 1) * LANES] * scale
        if split_out:
            o_ref[c] = seg.astype(o_ref.dtype)
        else:
            o_ref[:, c * LANES:(c + 1) * LANES] = seg.astype(o_ref.dtype)


def matmul_groupnorm(a, w, gain_cols, norm_groups, layer=0, n=None, w_transposed=False, scale=1.0, split_out=False,
                     out_dtype=BF16, tm=512, tn=512):
    m, k = a.shape
    n = w.shape[1 if w_transposed else 2] if n is None else n
    tm = min(tm, m)
    tn = min(tn, n)
    assert m % tm == 0 and n % tn == 0 and len(norm_groups) == tn // LANES
    if split_out:
        out_shape = jax.ShapeDtypeStruct((n // LANES, m, LANES), out_dtype)
        out_spec = pl.BlockSpec((tn // LANES, tm, LANES), lambda i, j: (j, i, 0))
    else:
        out_shape = jax.ShapeDtypeStruct((m, n), out_dtype)
        out_spec = pl.BlockSpec((tm, tn), lambda i, j: (i, j))
    body = functools.partial(_matmul_groupnorm_body, norm_groups=tuple(norm_groups), scale=scale,
                             split_out=split_out, w_transposed=w_transposed)
    return pl.pallas_call(
        body,
        grid=(m // tm, n // tn),
        in_specs=[pl.BlockSpec((tm, k), lambda i, j: (i, 0)),
                  _w_spec(k, tn, layer, w_transposed),
                  pl.BlockSpec((1, tn), lambda i, j: (0, j))],
        out_specs=out_spec,
        out_shape=out_shape,
        compiler_params=_params("arbitrary", "arbitrary"),
        name="matmul_groupnorm",
    )(a, w, gain_cols.reshape(1, n).astype(F32))


def _pad_cols(w, mult=LANES):
    pad = (-w.shape[-1]) % mult
    return jnp.pad(w, ((0, 0), (0, pad))) if pad else w


def _gla_body(q_ref, k_ref, v_ref, g_ref, a_ref, wg_ref, bg_ref, og_ref, tri_ref, o_ref, s_ref):
    dk, dv = s_ref.shape[1], s_ref.shape[2]
    C = GLA_BLOCK

    @pl.when(pl.program_id(2) == 0)
    def _():
        s_ref[...] = jnp.zeros(s_ref.shape, F32)

    tri = tri_ref[...]
    row = lax.broadcasted_iota(I32, (C, C), 0)
    col = lax.broadcasted_iota(I32, (C, C), 1)
    for c in range(q_ref.shape[0] // C):
        rows = slice(c * C, (c + 1) * C)
        a = a_ref[rows, :].astype(BF16)
        for hh in range(s_ref.shape[0]):
            kc = slice(hh * dk, (hh + 1) * dk)
            vc = slice(hh * dv, (hh + 1) * dv)
            gate_in = jnp.dot(a, wg_ref[:, kc], preferred_element_type=F32) + bg_ref[:, kc]
            log_a = jax.nn.log_sigmoid(gate_in) * (1.0 / GLA_GATE_TAU)
            hi, mid, lo = _split3(log_a)
            bcum = (jnp.dot(tri, hi, preferred_element_type=F32) + jnp.dot(tri, mid, preferred_element_type=F32)
                    + jnp.dot(tri, lo, preferred_element_type=F32))
            b_mid = bcum[C // 2 - 1:C // 2, :]
            q = q_ref[rows, kc].astype(F32) * dk ** -0.5
            k = k_ref[rows, kc].astype(F32)
            v = v_ref[rows, vc]
            att = lax.dot_general((q * jnp.exp(bcum - b_mid)).astype(BF16),
                                  (k * jnp.exp(b_mid - bcum)).astype(BF16),
                                  (((1,), (1,)), ((), ())), preferred_element_type=F32)
            att = jnp.where(col <= row, att, 0.0)
            o = jnp.dot(att.astype(BF16), v, preferred_element_type=F32)
            o = o + jnp.dot((q * jnp.exp(bcum)).astype(BF16), s_ref[hh].astype(BF16), preferred_element_type=F32)
            bcum_t = bcum.T
            b_last = bcum_t[:, C - 1:C]
            k_t = (k.T * jnp.exp(b_last - bcum_t)).astype(BF16)
            s_ref[hh] = s_ref[hh] * jnp.exp(b_last) + jnp.dot(k_t, v, preferred_element_type=F32)
            ms = jnp.mean(o * o, axis=-1, keepdims=True)
            o = o * lax.rsqrt(ms + NORM_EPS) * og_ref[...]
            o_ref[rows, vc] = (o * jax.nn.silu(g_ref[rows, vc].astype(F32))).astype(o_ref.dtype)


def gla_core(proj, a_pad, w_gate2, b_gate2, out_gain, B, T):
    H = GLA_HEADS
    n = proj.shape[0]
    dv = out_gain.shape[0]
    dk = w_gate2.shape[1] // H
    tb = min(GLA_STEP_TOKENS, T)
    nt = T // tb
    wg = jnp.pad(w_gate2, ((0, LANES - w_gate2.shape[0]), (0, 0))).astype(BF16)
    tri = jnp.asarray(np.tril(np.ones((GLA_BLOCK, GLA_BLOCK), np.float32)), BF16)
    hp = GLA_HEADS_PER_STEP
    hs = H // hp
    v0 = 2 * H * dk // (hp * dv)
    return pl.pallas_call(
        _gla_body,
        grid=(B, hs, nt),
        in_specs=[pl.BlockSpec((tb, hp * dk), lambda b, h, i: (b * nt + i, h)),
                  pl.BlockSpec((tb, hp * dk), lambda b, h, i: (b * nt + i, hs + h)),
                  pl.BlockSpec((tb, hp * dv), lambda b, h, i: (b * nt + i, v0 + h)),
                  pl.BlockSpec((tb, hp * dv), lambda b, h, i: (b * nt + i, v0 + hs + h)),
                  pl.BlockSpec((tb, LANES), lambda b, h, i: (b * nt + i, 0)),
                  pl.BlockSpec((LANES, hp * dk), lambda b, h, i: (0, h)),
                  pl.BlockSpec((1, hp * dk), lambda b, h, i: (0, h)),
                  pl.BlockSpec((1, dv), lambda b, h, i: (0, 0)),
                  pl.BlockSpec((GLA_BLOCK, GLA_BLOCK), lambda b, h, i: (0, 0))],
        out_specs=pl.BlockSpec((tb, hp * dv), lambda b, h, i: (b * nt + i, h)),
        out_shape=jax.ShapeDtypeStruct((n, H * dv), BF16),
        scratch_shapes=[pltpu.VMEM((hp, dk, dv), F32)],
        compiler_params=_params("arbitrary", "arbitrary", "arbitrary"),
        name="gla_core",
    )(proj, proj, proj, proj, a_pad, wg, b_gate2.reshape(1, -1).astype(F32), out_gain.reshape(1, dv).astype(F32), tri)


def _nsa_compress_body(u_ref, w1_ref, w2_ref, pe_ref, kg_ref, o_ref):
    kv = pl.program_id(0)
    u = u_ref[0, 0]
    half = u.shape[1]
    w1 = w1_ref[0]
    a = jnp.dot(u, w1[:half], preferred_element_type=F32)
    b = jnp.dot(u, w1[half:], preferred_element_type=F32)
    pe_term = jnp.dot(pe_ref[0], w1, preferred_element_type=F32)[0:1]
    n_chunks = u.shape[0]
    hid = jax.nn.gelu(a + pltpu.roll(b, n_chunks - 1, 0) + pe_term, approximate=True)
    out = jnp.dot(hid.astype(BF16), w2_ref[0], preferred_element_type=F32)
    ms = jnp.mean(out * out, axis=-1, keepdims=True)
    normed = out * lax.rsqrt(ms + NORM_EPS) * kg_ref[...]
    res = jnp.where(kv == 0, normed, out)
    row = lax.broadcasted_iota(I32, res.shape, 0)
    o_ref[0, 0, 0] = jnp.where(row < n_chunks - 1, res, 0.0).astype(o_ref.dtype)


def nsa_compress(zz, cmp_pe, cmp_w1, cmp_w2, kc_gain, B, T):
    G = NSA_GROUPS
    Dh = zz.shape[-1]
    n_chunks = T // CMP_STRIDE
    u = zz.reshape(zz.shape[0], B, n_chunks, CMP_STRIDE * Dh)
    w1 = cmp_w1.reshape(2, CMP_BLOCK * Dh, Dh).astype(BF16)
    pe = jnp.broadcast_to(cmp_pe.reshape(2, 1, CMP_BLOCK * Dh), (2, 8, CMP_BLOCK * Dh)).astype(BF16)
    return pl.pallas_call(
        _nsa_compress_body,
        grid=(2, B, G),
        in_specs=[pl.BlockSpec((1, 1, n_chunks, CMP_STRIDE * Dh), lambda kv, b, g: (kv * G + g, b, 0, 0)),
                  pl.BlockSpec((1, CMP_BLOCK * Dh, Dh), lambda kv, b, g: (kv, 0, 0)),
                  pl.BlockSpec((1, Dh, Dh), lambda kv, b, g: (kv, 0, 0)),
                  pl.BlockSpec((1, 8, CMP_BLOCK * Dh), lambda kv, b, g: (kv, 0, 0)),
                  pl.BlockSpec((1, Dh), lambda kv, b, g: (0, 0))],
        out_specs=pl.BlockSpec((1, 1, 1, n_chunks, Dh), lambda kv, b, g: (kv, b, g, 0, 0)),
        out_shape=jax.ShapeDtypeStruct((2, B, G, n_chunks, Dh), BF16),
        compiler_params=_params("arbitrary", "arbitrary", "arbitrary"),
        name="nsa_compress",
    )(u, w1, cmp_w2.astype(BF16), pe, kc_gain.reshape(1, Dh).astype(F32))


def _dot_nt(a, b):
    return lax.dot_general(a, b, (((1,), (1,)), ((), ())), preferred_element_type=F32)


def _col_softmax_terms(s):
    m = jnp.max(s, axis=0, keepdims=True)
    m = jnp.where(m == NEG_INF, 0.0, m)
    e = jnp.exp2(s - m)
    return e, 1.0 / jnp.maximum(jnp.sum(e, axis=0, keepdims=True), 1e-30)


def _tile_lanes(x, n):
    return jnp.concatenate([x] * n, axis=1)


def _nsa_attn_body(q_ref, qaug_ref, glog_ref, kc_ref, vct_ref, ks_ref, vst_ref, kw_ref, vwt_ref,
                   mselt_ref, efullt_ref, o_ref, q2_ref, acct_ref, m_ref, l_ref, oacct_ref, s_ref, sc_ref, sw_ref,
                   sel_ref):
    qb = pl.program_id(2)
    Dh = LANES
    RB = NSA_ROW_BLOCK
    hpb = RB // Q_BLOCK
    n_rb = NSA_HG // hpb
    n_cmp_pad = kc_ref.shape[2]
    n_slc = mselt_ref.shape[0]
    win_keys = WINDOW + Q_BLOCK
    s0 = qb * Q_BLOCK
    t_q = s0 + lax.broadcasted_iota(I32, (1, Q_BLOCK), 1)

    gates_t = jax.nn.sigmoid(glog_ref[0]).T

    for hg in range(NSA_HG):
        q2_ref[hg * Q_BLOCK:(hg + 1) * Q_BLOCK, :Dh] = q_ref[0, :, hg * Dh:(hg + 1) * Dh]
    q2_ref[:, Dh:] = qaug_ref[0]

    cmp_end = lax.broadcasted_iota(I32, (n_cmp_pad, 1), 0) * CMP_STRIDE + (CMP_BLOCK - 1)
    cmp_mask = _tile_lanes(jnp.where(cmp_end <= t_q, 0.0, NEG_INF), hpb)

    ws = pl.multiple_of(jnp.maximum(s0 - WINDOW, 0), Q_BLOCK)
    wdist = t_q - (ws + lax.broadcasted_iota(I32, (win_keys, 1), 0))
    win_mask = _tile_lanes(jnp.where((wdist >= 0) & (wdist < WINDOW), 0.0, NEG_INF), hpb)

    kc = kc_ref[0, 0]
    vct = jnp.concatenate([vct_ref[0, 0, j] for j in range(n_cmp_pad // Dh)], axis=1)
    kwin = kw_ref[0, 0, pl.ds(ws, win_keys), :]
    wblk = ws // Dh
    vwt = jnp.concatenate([vwt_ref[0, 0, wblk + j] for j in range(win_keys // Dh)], axis=1)

    imp_t = jnp.zeros((n_cmp_pad, Q_BLOCK), F32)
    for rb in range(n_rb):
        q_rb = q2_ref[rb * RB:(rb + 1) * RB, :]
        sc_ref[rb] = _dot_nt(kc, q_rb) + cmp_mask
        sw_ref[rb] = _dot_nt(kwin, q_rb) + win_mask
    for rb in range(n_rb):
        e, inv = _col_softmax_terms(sc_ref[rb])
        p = e * inv
        for j in range(hpb):
            imp_t = imp_t + p[:, j * Q_BLOCK:(j + 1) * Q_BLOCK]
        o_c = jnp.dot(vct, p.astype(BF16), preferred_element_type=F32)
        e, inv = _col_softmax_terms(sw_ref[rb])
        o_w = jnp.dot(vwt, e.astype(BF16), preferred_element_type=F32) * inv
        for j in range(hpb):
            hg = rb * hpb + j
            sub = slice(j * Q_BLOCK, (j + 1) * Q_BLOCK)
            acct_ref[:, hg * Q_BLOCK:(hg + 1) * Q_BLOCK] = (
                gates_t[hg:hg + 1, :] * o_c[:, sub] + gates_t[2 * NSA_HG + hg:2 * NSA_HG + hg + 1, :] * o_w[:, sub])

    hi, mid, lo = _split3(imp_t)
    mselt = mselt_ref[...]
    slc_imp = (jnp.dot(mselt, hi, preferred_element_type=F32) + jnp.dot(mselt, mid, preferred_element_type=F32)
               + jnp.dot(mselt, lo, preferred_element_type=F32))

    blk = lax.broadcasted_iota(I32, (n_slc, 1), 0)
    cur = lax.shift_right_logical(t_q, 6)
    forced = (blk == 0) | (blk == cur) | (blk == cur - 1)
    score = jnp.where(forced, jnp.inf, jnp.where(blk <= cur, slc_imp, NEG_INF))
    sel = jnp.zeros((n_slc, Q_BLOCK), F32)
    for _ in range(SLC_TOPK):
        m = jnp.max(score, axis=0, keepdims=True)
        cand = (score == m) & (m > NEG_INF)
        first = jnp.min(jnp.where(cand, blk, n_slc), axis=0, keepdims=True)
        one = blk == first
        sel = jnp.where(one, 1.0, sel)
        score = jnp.where(one, NEG_INF, score)
    sel_b = sel.astype(BF16)
    sel_ref[...] = sel

    m_ref[...] = jnp.full(m_ref.shape, NEG_INF, F32)
    l_ref[...] = jnp.zeros(l_ref.shape, F32)
    oacct_ref[...] = jnp.zeros(oacct_ref.shape, F32)
    KC = SLC_KEY_CHUNK
    key_iota = lax.broadcasted_iota(I32, (KC, 1), 0)

    def chunk_step(c, carry):
        k0 = pl.multiple_of(c * KC, KC)
        bpc = KC // SLC_BLOCK
        picked = sel_ref[pl.ds(pl.multiple_of(c * bpc, bpc), bpc), :]
        any_picked = jnp.max(jnp.max(picked, axis=0, keepdims=True), axis=1, keepdims=True)[0, 0]

        @pl.when(any_picked > 0.5)
        def _():
            kch = ks_ref[0, 0, pl.ds(k0, KC), :]
            vt = jnp.concatenate([vst_ref[0, 0, c * (KC // Dh) + j] for j in range(KC // Dh)], axis=1)
            selk = jnp.dot(efullt_ref[pl.ds(k0, KC), :], sel_b, preferred_element_type=F32)
            mask = _tile_lanes(jnp.where((selk > 0.5) & (k0 + key_iota <= t_q), 0.0, NEG_INF), hpb)
            for rb in range(n_rb):
                s_ref[rb] = _dot_nt(kch, q2_ref[rb * RB:(rb + 1) * RB, :]) + mask
            for rb in range(n_rb):
                cols = slice(rb * RB, (rb + 1) * RB)
                s = s_ref[rb]
                m_old = m_ref[rb]
                m_new = jnp.maximum(m_old, jnp.max(s, axis=0, keepdims=True))
                m_safe = jnp.where(m_new == NEG_INF, 0.0, m_new)
                alpha = jnp.exp2(m_old - m_safe)
                p = jnp.exp2(s - m_safe)
                l_ref[rb] = alpha * l_ref[rb] + jnp.sum(p, axis=0, keepdims=True)
                oacct_ref[:, cols] = (alpha * oacct_ref[:, cols]
                                      + jnp.dot(vt, p.astype(BF16), preferred_element_type=F32))
                m_ref[rb] = m_new
        return carry

    n_chunks = (s0 + Q_BLOCK + KC - 1) // KC
    lax.fori_loop(0, n_chunks, chunk_step, 0)
    for hg in range(NSA_HG):
        rb, j = divmod(hg, hpb)
        cols = slice(hg * Q_BLOCK, (hg + 1) * Q_BLOCK)
        inv = 1.0 / jnp.maximum(l_ref[rb][:, j * Q_BLOCK:(j + 1) * Q_BLOCK], 1e-30)
        o_t = acct_ref[:, cols] + gates_t[NSA_HG + hg:NSA_HG + hg + 1, :] * (oacct_ref[:, cols] * inv)
        o_ref[0, :, hg * Dh:(hg + 1) * Dh] = o_t.T.astype(o_ref.dtype)


def _pos_pieces(pos):
    pos = np.asarray(pos)
    out = np.zeros((pos.shape[0], LANES), np.float32)
    for i in range(3):
        out[:, 2 * i] = 64 * (pos // 64)
        out[:, 2 * i + 1] = pos % 64
    return jnp.asarray(out, BF16)


def _slope_pieces():
    H, G, HG = NSA_HEADS, NSA_GROUPS, NSA_HG
    slopes = jnp.asarray(LOG2E * 2.0 ** (-8.0 * np.arange(1, H + 1, dtype=np.float64) / H), F32)
    pieces = jnp.stack(_split3(slopes), axis=-1)
    cols = jnp.repeat(pieces, 2, axis=-1)
    cols = jnp.pad(cols, ((0, 0), (0, LANES - cols.shape[-1])))
    return jnp.repeat(cols.reshape(G, HG, 1, LANES), Q_BLOCK, axis=2).reshape(G, HG * Q_BLOCK, LANES)


def _blocked_transpose(v):
    lead, (t, dh) = v.shape[:-2], v.shape[-2:]
    return jnp.swapaxes(v.reshape(*lead, t // LANES, LANES, dh), -1, -2)


def nsa_attention(q, glog, kvc, zz, B, T):
    G, HG, Dh = NSA_GROUPS, NSA_HG, LANES
    H = NSA_HEADS
    n_cmp = T // CMP_STRIDE - CMP_BLOCK // CMP_STRIDE + 1
    n_cmp_pad = -(-T // CMP_STRIDE // LANES) * LANES
    n_slc = T // SLC_BLOCK
    per = SLC_BLOCK // CMP_STRIDE
    n = np.arange(n_cmp_pad)[None, :]
    j = np.arange(n_slc)[:, None]
    mselt = ((n // per == j).astype(np.float32) + ((n + 1) // per == j).astype(np.float32))
    mselt[:, n_cmp:] = 0.0
    kvc = jnp.pad(kvc, ((0, 0), (0, 0), (0, 0), (0, n_cmp_pad - kvc.shape[3]), (0, 0)))
    efullt = (np.arange(T)[:, None] // SLC_BLOCK == np.arange(n_slc)[None, :]).astype(np.float32)

    key_aug = jnp.broadcast_to(_pos_pieces(np.arange(T)), (G, B, T, LANES))
    cmp_aug = jnp.broadcast_to(_pos_pieces(np.arange(n_cmp_pad) * CMP_STRIDE + CMP_BLOCK - 1),
                               (B, G, n_cmp_pad, LANES))
    kc_aug = jnp.concatenate([kvc[0], cmp_aug], axis=-1)
    vct = _blocked_transpose(kvc[1])
    ks_aug = jnp.concatenate([zz[2 * G:3 * G], key_aug], axis=-1)
    kw_aug = jnp.concatenate([zz[4 * G:5 * G], key_aug], axis=-1)
    vst = _blocked_transpose(zz[3 * G:4 * G])
    vwt = _blocked_transpose(zz[5 * G:6 * G])

    def k_spec():
        return pl.BlockSpec((1, 1, T, 2 * Dh), lambda b, g, i: (g, b, 0, 0))

    def vt_spec():
        return pl.BlockSpec((1, 1, T // LANES, Dh, LANES), lambda b, g, i: (g, b, 0, 0, 0))

    n_rb = HG * Q_BLOCK // NSA_ROW_BLOCK
    return pl.pallas_call(
        _nsa_attn_body,
        grid=(B, G, T // Q_BLOCK),
        in_specs=[pl.BlockSpec((1, Q_BLOCK, HG * Dh), lambda b, g, i: (b, i, g)),
                  pl.BlockSpec((1, HG * Q_BLOCK, LANES), lambda b, g, i: (g, 0, 0)),
                  pl.BlockSpec((1, Q_BLOCK, LANES), lambda b, g, i: (b, i, g)),
                  pl.BlockSpec((1, 1, n_cmp_pad, 2 * Dh), lambda b, g, i: (b, g, 0, 0)),
                  pl.BlockSpec((1, 1, n_cmp_pad // LANES, Dh, LANES), lambda b, g, i: (b, g, 0, 0, 0)),
                  k_spec(), vt_spec(), k_spec(), vt_spec(),
                  pl.BlockSpec((n_slc, n_cmp_pad), lambda b, g, i: (0, 0)),
                  pl.BlockSpec((T, n_slc), lambda b, g, i: (0, 0))],
        out_specs=pl.BlockSpec((1, Q_BLOCK, HG * Dh), lambda b, g, i: (b, i, g)),
        out_shape=jax.ShapeDtypeStruct((B, T, H * Dh), BF16),
        scratch_shapes=[pltpu.VMEM((HG * Q_BLOCK, 2 * Dh), BF16),
                        pltpu.VMEM((Dh, HG * Q_BLOCK), F32),
                        pltpu.VMEM((n_rb, 1, NSA_ROW_BLOCK), F32),
                        pltpu.VMEM((n_rb, 1, NSA_ROW_BLOCK), F32),
                        pltpu.VMEM((Dh, HG * Q_BLOCK), F32),
                        pltpu.VMEM((n_rb, SLC_KEY_CHUNK, NSA_ROW_BLOCK), F32),
                        pltpu.VMEM((n_rb, n_cmp_pad, NSA_ROW_BLOCK), F32),
                        pltpu.VMEM((n_rb, WINDOW + Q_BLOCK, NSA_ROW_BLOCK), F32),
                        pltpu.VMEM((n_slc, Q_BLOCK), F32)],
        compiler_params=_params("arbitrary", "arbitrary", "arbitrary"),
        name="nsa_attention",
    )(q, _slope_pieces(), glog, kc_aug, vct, ks_aug, vst, kw_aug, vwt,
      jnp.asarray(mselt, BF16), jnp.asarray(efullt, BF16))


def _pack_rows(y, o_ref):
    m, d = y.shape
    bits = lax.bitcast_convert_type(y.astype(BF16).astype(F32), U32)
    for c in range(d // (2 * LANES)):
        lo = lax.shift_right_logical(bits[:, c * LANES:(c + 1) * LANES], jnp.uint32(16))
        hi = bits[:, d // 2 + c * LANES:d // 2 + (c + 1) * LANES]
        o_ref[pl.ds(c, m, stride=d // (2 * LANES)), :] = lo | hi


def _unpack_chunk(words):
    lo = lax.bitcast_convert_type(lax.shift_left(words, jnp.uint32(16)), F32)
    hi = lax.bitcast_convert_type(words & jnp.uint32(0xFFFF0000), F32)
    return lo, hi


def _moe_router_body(h_ref, g_ref, w_ref, b_ref, tri_ref, xp_ref, idx_ref, wgt_ref, rank_ref, cnt_ref):
    @pl.when(pl.program_id(0) == 0)
    def _():
        cnt_ref[...] = jnp.zeros(cnt_ref.shape, F32)

    x = h_ref[...]
    ms = jnp.mean(x * x, axis=-1, keepdims=True)
    xn = x * lax.rsqrt(ms + NORM_EPS) * g_ref[...]
    _pack_rows(xn, xp_ref)
    logits = jnp.dot(xn.astype(BF16), w_ref[...], preferred_element_type=F32) + b_ref[...]
    lane = lax.broadcasted_iota(I32, logits.shape, 1)
    logits = jnp.where(lane < N_EXPERTS, logits, NEG_INF)
    idx_out = jnp.zeros(logits.shape, I32)
    val_out = jnp.full(logits.shape, NEG_INF, F32)
    picked = []
    for k in range(TOP_K):
        m = jnp.max(logits, axis=-1, keepdims=True)
        first = jnp.min(jnp.where(logits == m, lane, LANES), axis=-1, keepdims=True)
        idx_out = jnp.where(lane == k, first, idx_out)
        val_out = jnp.where(lane == k, m, val_out)
        picked.append(jnp.where(lane == first, 1.0, 0.0))
        logits = jnp.where(lane == first, NEG_INF, logits)
    e = jnp.exp(val_out - jnp.max(val_out, axis=-1, keepdims=True))
    idx_ref[...] = idx_out
    wgt_ref[...] = e * (1.0 / jnp.sum(e, axis=-1, keepdims=True))

    base = cnt_ref[0:1, :]
    rank_out = jnp.zeros(logits.shape, F32)
    for k in range(TOP_K):
        earlier = jnp.dot(tri_ref[...], picked[k].astype(BF16), preferred_element_type=F32)
        r = jnp.sum(picked[k] * (base + earlier), axis=-1, keepdims=True)
        rank_out = jnp.where(lane == k, r, rank_out)
        base = base + jnp.sum(picked[k], axis=0, keepdims=True)
    rank_ref[...] = rank_out.astype(I32)
    cnt_ref[...] = jnp.broadcast_to(base, cnt_ref.shape)


def moe_router(h, gain, w_router, b_router, tm=256):
    n, d = h.shape
    tm = min(tm, n)
    pr = d // (2 * LANES)
    w = _pad_cols(w_router).astype(BF16)
    b = _pad_cols(b_router.reshape(1, -1)).astype(F32)
    tri = jnp.asarray(np.tril(np.ones((tm, tm), np.float32), -1), BF16)
    return pl.pallas_call(
        _moe_router_body,
        grid=(n // tm,),
        in_specs=[pl.BlockSpec((tm, d), lambda i: (i, 0)),
                  pl.BlockSpec((1, d), lambda i: (0, 0)),
                  pl.BlockSpec((d, LANES), lambda i: (0, 0)),
                  pl.BlockSpec((1, LANES), lambda i: (0, 0)),
                  pl.BlockSpec((tm, tm), lambda i: (0, 0))],
        out_specs=[pl.BlockSpec((tm * pr, LANES), lambda i: (i, 0)),
                   pl.BlockSpec((tm, LANES), lambda i: (i, 0)),
                   pl.BlockSpec((tm, LANES), lambda i: (i, 0)),
                   pl.BlockSpec((tm, LANES), lambda i: (i, 0)),
                   pl.BlockSpec((8, LANES), lambda i: (0, 0))],
        out_shape=[jax.ShapeDtypeStruct((n * pr, LANES), U32),
                   jax.ShapeDtypeStruct((n, LANES), I32), jax.ShapeDtypeStruct((n, LANES), F32),
                   jax.ShapeDtypeStruct((n, LANES), I32), jax.ShapeDtypeStruct((8, LANES), F32)],
        compiler_params=_params("arbitrary"),
        name="moe_router",
    )(h, gain.reshape(1, d).astype(F32), w, b, tri)


def _row_gather_copy(src_hbm, src_row, dst_buf, slot, r, sem, pr):
    return pltpu.make_async_copy(src_hbm.at[pl.ds(pl.multiple_of(src_row, pr), pr)],
                                 dst_buf.at[slot, pl.ds(pl.multiple_of(r * pr, pr), pr)], sem.at[slot])


def _start_row_gather(ids_ref, src_hbm, dst_buf, slot, sem, n_rows, pr):
    def body(r, carry):
        _row_gather_copy(src_hbm, ids_ref[0, 0, r], dst_buf, slot, r, sem, pr).start()
        return carry
    lax.fori_loop(0, n_rows, body, 0, unroll=8)


def _wait_row_gather(src_hbm, dst_buf, slot, sem):
    pltpu.make_async_copy(src_hbm.at[pl.ds(0, dst_buf.shape[1])], dst_buf.at[slot], sem.at[slot]).wait()


def _moe_expert_body(blk_e_ref, n_used_ref, ids0_ref, idsn_ref, x_hbm, wgu_ref, bgu_ref, wd_ref, bd_ref,
                     y_ref, xbuf, xs, sem):
    i = pl.program_id(0)
    n_used = n_used_ref[0]
    rows, d = xs.shape
    pr = d // (2 * LANES)
    ff = wd_ref.shape[2]

    @pl.when(i == 0)
    def _():
        _start_row_gather(ids0_ref, x_hbm, xbuf, 0, sem, rows, pr)

    @pl.when(i + 1 < n_used)
    def _():
        _start_row_gather(idsn_ref, x_hbm, xbuf, (i + 1) % 2, sem, rows, pr)

    @pl.when(i < n_used)
    def _():
        slot = i % 2
        _wait_row_gather(x_hbm, xbuf, slot, sem)
        sub = rows // MOE_SUB_BLOCKS
        for sb in range(MOE_SUB_BLOCKS):
            r0 = sb * sub
            for c in range(pr):
                lo, hi = _unpack_chunk(xbuf[slot, pl.ds(r0 * pr + c, sub, stride=pr), :])
                xs[r0:r0 + sub, c * LANES:(c + 1) * LANES] = lo.astype(BF16)
                xs[r0:r0 + sub, d // 2 + c * LANES:d // 2 + (c + 1) * LANES] = hi.astype(BF16)
            gu = jnp.dot(xs[r0:r0 + sub, :], wgu_ref[0, 0], preferred_element_type=F32) + bgu_ref[0]
            gate = jnp.minimum(gu[:, :ff], SWIGLU_LIMIT)
            up = jnp.clip(gu[:, ff:], -SWIGLU_LIMIT, SWIGLU_LIMIT)
            act = (up + 1.0) * (gate * jax.nn.sigmoid(SWIGLU_ALPHA * gate))
            y = jnp.dot(act.astype(BF16), wd_ref[0, 0], preferred_element_type=F32) + bd_ref[0]
            _pack_rows(y, y_ref.at[pl.ds(r0 * pr, sub * pr), :])

    @pl.when(i >= n_used)
    def _():
        y_ref[...] = jnp.zeros(y_ref.shape, y_ref.dtype)


def moe_experts(xp, tok_rows, blk_e, n_used, w_gate_up, b_gate_up, w_down, b_down, l):
    _, E, d, ff2 = w_gate_up.shape
    ff = ff2 // 2
    pr = d // (2 * LANES)
    rows = MOE_BLOCK_ROWS
    n_blocks = tok_rows.shape[0] // rows
    ids = tok_rows.reshape(n_blocks, 1, rows)

    def used(i, nu):
        return jnp.minimum(i, nu[0] - 1)

    grid_spec = pltpu.PrefetchScalarGridSpec(
        num_scalar_prefetch=2,
        grid=(n_blocks,),
        in_specs=[pl.BlockSpec((1, 1, rows), lambda i, be, nu: (0, 0, 0), memory_space=pltpu.SMEM),
                  pl.BlockSpec((1, 1, rows), lambda i, be, nu: (jnp.minimum(i + 1, n_blocks - 1), 0, 0),
                               memory_space=pltpu.SMEM),
                  pl.BlockSpec(memory_space=pl.ANY),
                  pl.BlockSpec((1, 1, d, ff2), lambda i, be, nu: (l, be[used(i, nu)], 0, 0)),
                  pl.BlockSpec((1, 1, ff2), lambda i, be, nu: (be[used(i, nu)], 0, 0)),
                  pl.BlockSpec((1, 1, ff, d), lambda i, be, nu: (l, be[used(i, nu)], 0, 0)),
                  pl.BlockSpec((1, 1, d), lambda i, be, nu: (be[used(i, nu)], 0, 0))],
        out_specs=pl.BlockSpec((rows * pr, LANES), lambda i, be, nu: (i, 0)),
        scratch_shapes=[pltpu.VMEM((2, rows * pr, LANES), U32), pltpu.VMEM((rows, d), BF16),
                        pltpu.SemaphoreType.DMA((2,))],
    )
    return pl.pallas_call(
        _moe_expert_body,
        grid_spec=grid_spec,
        out_shape=jax.ShapeDtypeStruct((n_blocks * rows * pr, LANES), U32),
        compiler_params=_params("arbitrary"),
        name="moe_experts",
    )(blk_e, n_used, ids, ids, xp, w_gate_up, b_gate_up.reshape(E, 1, ff2).astype(F32),
      w_down, b_down.reshape(E, 1, d).astype(F32))


def _moe_combine_body(ids0_ref, idsn_ref, y_hbm, h_ref, w_ref, g_ref, o_ref, xn_ref, ybuf, sem):
    i = pl.program_id(0)
    n_steps = pl.num_programs(0)
    tt, d = h_ref.shape
    pr = d // (2 * LANES)
    rows = TOP_K * tt

    @pl.when(i == 0)
    def _():
        _start_row_gather(ids0_ref, y_hbm, ybuf, 0, sem, rows, pr)

    @pl.when(i + 1 < n_steps)
    def _():
        _start_row_gather(idsn_ref, y_hbm, ybuf, (i + 1) % 2, sem, rows, pr)

    slot = i % 2
    _wait_row_gather(y_hbm, ybuf, slot, sem)
    w = w_ref[...]
    wk = [jnp.broadcast_to(w[:, k:k + 1], (tt, LANES)) for k in range(TOP_K)]
    ssq = jnp.zeros((tt, LANES), F32)
    for c in range(pr):
        lo_cols = slice(c * LANES, (c + 1) * LANES)
        hi_cols = slice(d // 2 + c * LANES, d // 2 + (c + 1) * LANES)
        acc_lo = h_ref[:, lo_cols]
        acc_hi = h_ref[:, hi_cols]
        for k in range(TOP_K):
            lo, hi = _unpack_chunk(ybuf[slot, pl.ds(k * tt * pr + c, tt, stride=pr), :])
            acc_lo = acc_lo + wk[k] * lo
            acc_hi = acc_hi + wk[k] * hi
        o_ref[:, lo_cols] = acc_lo
        o_ref[:, hi_cols] = acc_hi
        ssq = ssq + acc_lo * acc_lo + acc_hi * acc_hi
    inv = lax.rsqrt(jnp.sum(ssq, axis=-1, keepdims=True) * (1.0 / d) + NORM_EPS)
    xn_ref[...] = (o_ref[...] * inv * g_ref[...]).astype(xn_ref.dtype)


def moe_combine(yp, slot_rows, weights, h, next_gain):
    n, d = h.shape
    pr = d // (2 * LANES)
    tt = min(MOE_COMBINE_TOKENS, n)
    n_steps = n // tt
    ids = slot_rows.reshape(n_steps, tt, TOP_K).transpose(0, 2, 1).reshape(n_steps, 1, TOP_K * tt)
    return pl.pallas_call(
        _moe_combine_body,
        grid=(n_steps,),
        in_specs=[pl.BlockSpec((1, 1, TOP_K * tt), lambda i: (0, 0, 0), memory_space=pltpu.SMEM),
                  pl.BlockSpec((1, 1, TOP_K * tt), lambda i: (jnp.minimum(i + 1, n_steps - 1), 0, 0),
                               memory_space=pltpu.SMEM),
                  pl.BlockSpec(memory_space=pl.ANY),
                  pl.BlockSpec((tt, d), lambda i: (i, 0)),
                  pl.BlockSpec((tt, LANES), lambda i: (i, 0)),
                  pl.BlockSpec((1, d), lambda i: (0, 0))],
        out_specs=[pl.BlockSpec((tt, d), lambda i: (i, 0)),
                   pl.BlockSpec((tt, d), lambda i: (i, 0))],
        out_shape=[jax.ShapeDtypeStruct((n, d), F32), jax.ShapeDtypeStruct((n, d), BF16)],
        scratch_shapes=[pltpu.VMEM((2, TOP_K * tt * pr, LANES), U32), pltpu.SemaphoreType.DMA((2,))],
        compiler_params=_params("arbitrary"),
        name="moe_combine",
    )(ids, ids, yp, h, weights, next_gain.reshape(1, d).astype(F32))


def _moe_plan(top_idx, rank, sizes, n_rows):
    E, rows = N_EXPERTS, MOE_BLOCK_ROWS
    flat_e = top_idx.reshape(-1)
    rank = rank.reshape(-1)
    nk = flat_e.shape[0]
    nblk = (sizes + rows - 1) // rows
    blk_end = jnp.cumsum(nblk)
    pad_start = (blk_end - nblk) * rows
    slot_dest = pad_start[flat_e] + rank
    tok_buf = jnp.zeros((n_rows,), I32).at[slot_dest].set(jnp.arange(nk, dtype=I32) // TOP_K)
    n_blocks = n_rows // rows
    blk_e = jnp.minimum(jnp.searchsorted(blk_end, jnp.arange(n_blocks, dtype=I32), side='right'), E - 1).astype(I32)
    n_used = blk_end[-1:].astype(I32)
    return slot_dest, tok_buf, blk_e, n_used


def moe_layer(h, gain, w_router, b_router, w_gate_up, b_gate_up, w_down, b_down, l, next_gain):
    n, d = h.shape
    pr = d // (2 * LANES)
    rows = MOE_BLOCK_ROWS
    xp, top_idx, weights, rank, counts = moe_router(h, gain, w_router, b_router)
    n_rows = n * TOP_K + N_EXPERTS * rows
    slot_dest, tok_buf, blk_e, n_used = _moe_plan(top_idx[:, :TOP_K], rank[:, :TOP_K],
                                                  counts[0, :N_EXPERTS].astype(I32), n_rows)
    yp = moe_experts(xp, tok_buf * pr, blk_e, n_used, w_gate_up, b_gate_up, w_down, b_down, l)
    return moe_combine(yp, slot_dest.reshape(n, TOP_K) * pr, weights, h, next_gain)


def _gla_mixer(h, xn, w_in, w_gate2, b_gate2, out_gain, w_out, l, B, T):
    n_main = w_in.shape[2] - GLA_GATE_RANK
    w_in_t = jnp.swapaxes(w_in, 1, 2)
    proj = matmul(xn, w_in_t, layer=l, n=n_main, w_transposed=True, out_dtype=BF16)
    wa_t = jnp.pad(w_in_t[l, n_main:], ((0, LANES - GLA_GATE_RANK), (0, 0)))
    a_pad = matmul(xn, wa_t[None], w_transposed=True)
    o = gla_core(proj, a_pad, w_gate2, b_gate2, out_gain, B, T)
    return matmul(o, w_out, layer=l, residual=h)


def _nsa_shared_kv(h, kv_gain, w_kv, k_gain, cmp_pe, cmp_w1, cmp_w2, B, T):
    G = NSA_GROUPS
    Dh = LANES
    xn = rmsnorm(h, kv_gain)
    ones = jnp.ones((G * Dh,), F32)
    gain_cols = jnp.concatenate([ones, ones, jnp.tile(k_gain[1], G), ones, jnp.tile(k_gain[2], G), ones])
    norm_groups = [False] * (2 * G) + [True] * G + [False] * G + [True] * G + [False] * G
    zz = matmul_groupnorm(xn, w_kv.astype(BF16)[None], gain_cols, norm_groups, split_out=True, tn=6 * G * Dh)
    kvc = nsa_compress(zz, cmp_pe, cmp_w1, cmp_w2, k_gain[0], B, T)
    return kvc, zz.reshape(6 * G, B, T, Dh)


def _nsa_mixer(h, xn, w_in, q_gain, w_out, kvc, zz, j, B, T):
    H, G, HG, Dh = NSA_HEADS, NSA_GROUPS, NSA_HG, LANES
    w_in_t = jnp.swapaxes(w_in, 1, 2)
    q = matmul_groupnorm(xn, w_in_t, jnp.tile(q_gain, H), [True] * 4, layer=j, n=H * Dh, w_transposed=True,
                         scale=Dh ** -0.5 * LOG2E, tm=1024, tn=512)
    wg = w_in_t[j, H * Dh:].reshape(G, HG, 3, -1).transpose(0, 2, 1, 3).reshape(G, 3 * HG, -1)
    wg = jnp.pad(wg, ((0, 0), (0, LANES - 3 * HG), (0, 0))).reshape(1, G * LANES, -1)
    glog = matmul(xn, wg, w_transposed=True)
    o = nsa_attention(q.reshape(B, T, H * Dh), glog.reshape(B, T, G * LANES), kvc, zz, B, T)
    return matmul(o.reshape(B * T, H * Dh), w_out, layer=j, residual=h)


def kernel(x, ln_mix, ln_ffn, a_w_in, a_w_gate2, a_b_gate2, a_out_gain, a_w_out, kv_gain, w_kv, k_gain,
           cmp_pe, cmp_w1, cmp_w2, b_w_in, b_q_gain, b_w_out, w_router, b_router, w_gate_up, b_gate_up,
           w_down, b_down):
    B, T, D = x.shape
    depth = ln_mix.shape[0]
    n_a = a_w_in.shape[0]
    h = x.reshape(B * T, D)
    shared = None
    wgu_bf = w_gate_up.astype(BF16)
    wd_bf = w_down.astype(BF16)
    xn = rmsnorm(h, ln_mix[0])
    for l in range(depth):
        if l < n_a:
            h = _gla_mixer(h, xn, a_w_in, a_w_gate2[l], a_b_gate2[l], a_out_gain[l], a_w_out, l, B, T)
        else:
            if l == n_a:
                shared = _nsa_shared_kv(h, kv_gain, w_kv, k_gain, cmp_pe, cmp_w1, cmp_w2, B, T)
            j = l - n_a
            h = _nsa_mixer(h, xn, b_w_in, b_q_gain[j], b_w_out, *shared, j, B, T)
        h, xn = moe_layer(h, ln_ffn[l], w_router[l], b_router[l], wgu_bf, b_gate_up[l], wd_bf, b_down[l], l,
                          ln_mix[min(l + 1, depth - 1)])
    return h.reshape(B, T, D)
```

```python
import functools
import math

import jax
import jax.numpy as jnp
from jax import lax
import numpy as np
from jax.experimental import pallas as pl
from jax.experimental.pallas import tpu as pltpu

F32 = jnp.float32
BF16 = jnp.bfloat16
I32 = jnp.int32
U32 = jnp.uint32

NORM_EPS = 1e-5
GLA_HEADS = 8
GLA_GATE_RANK = 16
GLA_GATE_TAU = 16.0
NSA_HEADS = 32
NSA_GROUPS = 2
NSA_HG = NSA_HEADS // NSA_GROUPS
CMP_BLOCK = 32
CMP_STRIDE = 16
SLC_BLOCK = 64
SLC_TOPK = 8
WINDOW = 512
Q_BLOCK = 128
N_EXPERTS = 32
TOP_K = 4
SWIGLU_ALPHA = 1.702
SWIGLU_LIMIT = 7.0

LANES = 128
MXU_DIM = 256
VMEM_LIMIT_BYTES = 56 * 1024 * 1024
MOE_BLOCK_ROWS = 512
MOE_SUB_BLOCKS = 4
MOE_COMBINE_TOKENS = 128
SLC_KEY_CHUNK = 512
NSA_ROW_BLOCK = 256
GLA_BLOCK = 128
GLA_STEP_TOKENS = 512
GLA_HEADS_PER_STEP = 2
NEG_INF = float("-inf")
LOG2E = math.log2(math.e)


def _params(*sem):
    return pltpu.CompilerParams(dimension_semantics=sem, vmem_limit_bytes=VMEM_LIMIT_BYTES)


def _split3(x):
    hi = x.astype(BF16)
    r1 = x - hi.astype(F32)
    mid = r1.astype(BF16)
    lo = (r1 - mid.astype(F32)).astype(BF16)
    return hi, mid, lo


def _rmsnorm_body(x_ref, g_ref, o_ref):
    x = x_ref[...].astype(F32)
    ms = jnp.mean(x * x, axis=-1, keepdims=True)
    o_ref[...] = (x * lax.rsqrt(ms + NORM_EPS) * g_ref[...].astype(F32)).astype(o_ref.dtype)


def rmsnorm(x, gain, out_dtype=BF16, tm=512):
    m, d = x.shape
    tm = min(tm, m)
    return pl.pallas_call(
        _rmsnorm_body,
        grid=(m // tm,),
        in_specs=[pl.BlockSpec((tm, d), lambda i: (i, 0)),
                  pl.BlockSpec((1, d), lambda i: (0, 0))],
        out_specs=pl.BlockSpec((tm, d), lambda i: (i, 0)),
        out_shape=jax.ShapeDtypeStruct((m, d), out_dtype),
        compiler_params=_params("arbitrary"),
        name="rmsnorm",
    )(x, gain.reshape(1, d))


def _tile_dot(a_ref, w_ref, w_transposed):
    w = w_ref[0].astype(BF16)
    dims = (((1,), (1,)), ((), ())) if w_transposed else (((1,), (0,)), ((), ()))
    return lax.dot_general(a_ref[...], w, dims, preferred_element_type=F32)


def _w_spec(k, tn, layer, w_transposed):
    if w_transposed:
        return pl.BlockSpec((1, tn, k), lambda i, j: (layer, j, 0))
    return pl.BlockSpec((1, k, tn), lambda i, j: (layer, 0, j))


def _matmul_body(a_ref, w_ref, o_ref, *, w_transposed):
    o_ref[...] = _tile_dot(a_ref, w_ref, w_transposed).astype(o_ref.dtype)


def _matmul_res_body(a_ref, w_ref, r_ref, o_ref, *, w_transposed):
    o_ref[...] = (r_ref[...].astype(F32) + _tile_dot(a_ref, w_ref, w_transposed)).astype(o_ref.dtype)


def matmul(a, w, layer=0, n=None, residual=None, w_transposed=False, out_dtype=F32, tm=1024, tn=512):
    m, k = a.shape
    n = w.shape[1 if w_transposed else 2] if n is None else n
    tm = min(tm, m)
    tn = min(tn, n)
    assert m % tm == 0 and n % tn == 0, (m, n, tm, tn)
    in_specs = [pl.BlockSpec((tm, k), lambda i, j: (i, 0)), _w_spec(k, tn, layer, w_transposed)]
    args = [a, w]
    body = functools.partial(_matmul_body, w_transposed=w_transposed)
    if residual is not None:
        in_specs.append(pl.BlockSpec((tm, tn), lambda i, j: (i, j)))
        args.append(residual)
        body = functools.partial(_matmul_res_body, w_transposed=w_transposed)
    return pl.pallas_call(
        body,
        grid=(m // tm, n // tn),
        in_specs=in_specs,
        out_specs=pl.BlockSpec((tm, tn), lambda i, j: (i, j)),
        out_shape=jax.ShapeDtypeStruct((m, n), out_dtype),
        compiler_params=_params("arbitrary", "arbitrary"),
        name="matmul",
    )(*args)


def _matmul_groupnorm_body(a_ref, w_ref, g_ref, o_ref, *, norm_groups, scale, split_out, w_transposed):
    acc = _tile_dot(a_ref, w_ref, w_transposed)
    for c, do_norm in enumerate(norm_groups):
        seg = acc[:, c * LANES:(c + 1) * LANES]
        if do_norm:
            ms = jnp.mean(seg * seg, axis=-1, keepdims=True)
            seg = seg * lax.rsqrt(ms + NORM_EPS) * g_ref[:, c * LANES:(c + 1) * LANES] * scale
        if split_out:
            o_ref[c] = seg.astype(o_ref.dtype)
        else:
            o_ref[:, c * LANES:(c + 1) * LANES] = seg.astype(o_ref.dtype)


def matmul_groupnorm(a, w, gain_cols, norm_groups, layer=0, n=None, w_transposed=False, scale=1.0, split_out=False,
                     out_dtype=BF16, tm=512, tn=512):
    m, k = a.shape
    n = w.shape[1 if w_transposed else 2] if n is None else n
    tm = min(tm, m)
    tn = min(tn, n)
    assert m % tm == 0 and n % tn == 0 and len(norm_groups) == tn // LANES
    if split_out:
        out_shape = jax.ShapeDtypeStruct((n // LANES, m, LANES), out_dtype)
        out_spec = pl.BlockSpec((tn // LANES, tm, LANES), lambda i, j: (j, i, 0))
    else:
        out_shape = jax.ShapeDtypeStruct((m, n), out_dtype)
        out_spec = pl.BlockSpec((tm, tn), lambda i, j: (i, j))
    body = functools.partial(_matmul_groupnorm_body, norm_groups=tuple(norm_groups), scale=scale,
                             split_out=split_out, w_transposed=w_transposed)
    return pl.pallas_call(
        body,
        grid=(m // tm, n // tn),
        in_specs=[pl.BlockSpec((tm, k), lambda i, j: (i, 0)),
                  _w_spec(k, tn, layer, w_transposed),
                  pl.BlockSpec((1, tn), lambda i, j: (0, j))],
        out_specs=out_spec,
        out_shape=out_shape,
        compiler_params=_params("arbitrary", "arbitrary"),
        name="matmul_groupnorm",
    )(a, w, gain_cols.reshape(1, n).astype(F32))


def _pad_cols(w, mult=LANES):
    pad = (-w.shape[-1]) % mult
    return jnp.pad(w, ((0, 0), (0, pad))) if pad else w


def _gla_body(q_ref, k_ref, v_ref, g_ref, a_ref, wg_ref, bg_ref, og_ref, tri_ref, o_ref, s_ref):
    dk, dv = s_ref.shape[1], s_ref.shape[2]
    C = GLA_BLOCK

    @pl.when(pl.program_id(2) == 0)
    def _():
        s_ref[...] = jnp.zeros(s_ref.shape, F32)

    tri = tri_ref[...]
    row = lax.broadcasted_iota(I32, (C, C), 0)
    col = lax.broadcasted_iota(I32, (C, C), 1)
    for c in range(q_ref.shape[0] // C):
        rows = slice(c * C, (c + 1) * C)
        a = a_ref[rows, :].astype(BF16)
        for hh in range(s_ref.shape[0]):
            kc = slice(hh * dk, (hh + 1) * dk)
            vc = slice(hh * dv, (hh + 1) * dv)
            gate_in = jnp.dot(a, wg_ref[:, kc], preferred_element_type=F32) + bg_ref[:, kc]
            log_a = jax.nn.log_sigmoid(gate_in) * (1.0 / GLA_GATE_TAU)
            hi, mid, lo = _split3(log_a)
            bcum = (jnp.dot(tri, hi, preferred_element_type=F32) + jnp.dot(tri, mid, preferred_element_type=F32)
                    + jnp.dot(tri, lo, preferred_element_type=F32))
            b_mid = bcum[C // 2 - 1:C // 2, :]
            q = q_ref[rows, kc].astype(F32) * dk ** -0.5
            k = k_ref[rows, kc].astype(F32)
            v = v_ref[rows, vc]
            att = lax.dot_general((q * jnp.exp(bcum - b_mid)).astype(BF16),
                                  (k * jnp.exp(b_mid - bcum)).astype(BF16),
                                  (((1,), (1,)), ((), ())), preferred_element_type=F32)
            att = jnp.where(col <= row, att, 0.0)
            o = jnp.dot(att.astype(BF16), v, preferred_element_type=F32)
            o = o + jnp.dot((q * jnp.exp(bcum)).astype(BF16), s_ref[hh].astype(BF16), preferred_element_type=F32)
            bcum_t = bcum.T
            b_last = bcum_t[:, C - 1:C]
            k_t = (k.T * jnp.exp(b_last - bcum_t)).astype(BF16)
            s_ref[hh] = s_ref[hh] * jnp.exp(b_last) + jnp.dot(k_t, v, preferred_element_type=F32)
            ms = jnp.mean(o * o, axis=-1, keepdims=True)
            o = o * lax.rsqrt(ms + NORM_EPS) * og_ref[...]
            o_ref[rows, vc] = (o * jax.nn.silu(g_ref[rows, vc].astype(F32))).astype(o_ref.dtype)


def gla_core(proj, a_pad, w_gate2, b_gate2, out_gain, B, T):
    H = GLA_HEADS
    n = proj.shape[0]
    dv = out_gain.shape[0]
    dk = w_gate2.shape[1] // H
    tb = min(GLA_STEP_TOKENS, T)
    nt = T // tb
    wg = jnp.pad(w_gate2, ((0, LANES - w_gate2.shape[0]), (0, 0))).astype(BF16)
    tri = jnp.asarray(np.tril(np.ones((GLA_BLOCK, GLA_BLOCK), np.float32)), BF16)
    hp = GLA_HEADS_PER_STEP
    hs = H // hp
    v0 = 2 * H * dk // (hp * dv)
    return pl.pallas_call(
        _gla_body,
        grid=(B, hs, nt),
        in_specs=[pl.BlockSpec((tb, hp * dk), lambda b, h, i: (b * nt + i, h)),
                  pl.BlockSpec((tb, hp * dk), lambda b, h, i: (b * nt + i, hs + h)),
                  pl.BlockSpec((tb, hp * dv), lambda b, h, i: (b * nt + i, v0 + h)),
                  pl.BlockSpec((tb, hp * dv), lambda b, h, i: (b * nt + i, v0 + hs + h)),
                  pl.BlockSpec((tb, LANES), lambda b, h, i: (b * nt + i, 0)),
                  pl.BlockSpec((LANES, hp * dk), lambda b, h, i: (0, h)),
                  pl.BlockSpec((1, hp * dk), lambda b, h, i: (0, h)),
                  pl.BlockSpec((1, dv), lambda b, h, i: (0, 0)),
                  pl.BlockSpec((GLA_BLOCK, GLA_BLOCK), lambda b, h, i: (0, 0))],
        out_specs=pl.BlockSpec((tb, hp * dv), lambda b, h, i: (b * nt + i, h)),
        out_shape=jax.ShapeDtypeStruct((n, H * dv), BF16),
        scratch_shapes=[pltpu.VMEM((hp, dk, dv), F32)],
        compiler_params=_params("arbitrary", "arbitrary", "arbitrary"),
        name="gla_core",
    )(proj, proj, proj, proj, a_pad, wg, b_gate2.reshape(1, -1).astype(F32), out_gain.reshape(1, dv).astype(F32), tri)


def _nsa_compress_body(u_ref, w1_ref, w2_ref, pe_ref, kg_ref, o_ref):
    kv = pl.program_id(0)
    u = u_ref[0, 0]
    half = u.shape[1]
    w1 = w1_ref[0]
    a = jnp.dot(u, w1[:half], preferred_element_type=F32)
    b = jnp.dot(u, w1[half:], preferred_element_type=F32)
    pe_term = jnp.dot(pe_ref[0], w1, preferred_element_type=F32)[0:1]
    n_chunks = u.shape[0]
    hid = jax.nn.gelu(a + pltpu.roll(b, n_chunks - 1, 0) + pe_term, approximate=True)
    out = jnp.dot(hid.astype(BF16), w2_ref[0], preferred_element_type=F32)
    ms = jnp.mean(out * out, axis=-1, keepdims=True)
    normed = out * lax.rsqrt(ms + NORM_EPS) * kg_ref[...]
    res = jnp.where(kv == 0, normed, out)
    row = lax.broadcasted_iota(I32, res.shape, 0)
    o_ref[0, 0, 0] = jnp.where(row < n_chunks - 1, res, 0.0).astype(o_ref.dtype)


def nsa_compress(zz, cmp_pe, cmp_w1, cmp_w2, kc_gain, B, T):
    G = NSA_GROUPS
    Dh = zz.shape[-1]
    n_chunks = T // CMP_STRIDE
    u = zz.reshape(zz.shape[0], B, n_chunks, CMP_STRIDE * Dh)
    w1 = cmp_w1.reshape(2, CMP_BLOCK * Dh, Dh).astype(BF16)
    pe = jnp.broadcast_to(cmp_pe.reshape(2, 1, CMP_BLOCK * Dh), (2, 8, CMP_BLOCK * Dh)).astype(BF16)
    return pl.pallas_call(
        _nsa_compress_body,
        grid=(2, B, G),
        in_specs=[pl.BlockSpec((1, 1, n_chunks, CMP_STRIDE * Dh), lambda kv, b, g: (kv * G + g, b, 0, 0)),
                  pl.BlockSpec((1, CMP_BLOCK * Dh, Dh), lambda kv, b, g: (kv, 0, 0)),
                  pl.BlockSpec((1, Dh, Dh), lambda kv, b, g: (kv, 0, 0)),
                  pl.BlockSpec((1, 8, CMP_BLOCK * Dh), lambda kv, b, g: (kv, 0, 0)),
                  pl.BlockSpec((1, Dh), lambda kv, b, g: (0, 0))],
        out_specs=pl.BlockSpec((1, 1, 1, n_chunks, Dh), lambda kv, b, g: (kv, b, g, 0, 0)),
        out_shape=jax.ShapeDtypeStruct((2, B, G, n_chunks, Dh), BF16),
        compiler_params=_params("arbitrary", "arbitrary", "arbitrary"),
        name="nsa_compress",
    )(u, w1, cmp_w2.astype(BF16), pe, kc_gain.reshape(1, Dh).astype(F32))


def _dot_nt(a, b):
    return lax.dot_general(a, b, (((1,), (1,)), ((), ())), preferred_element_type=F32)


def _col_softmax_terms(s):
    m = jnp.max(s, axis=0, keepdims=True)
    m = jnp.where(m == NEG_INF, 0.0, m)
    e = jnp.exp2(s - m)
    return e, 1.0 / jnp.maximum(jnp.sum(e, axis=0, keepdims=True), 1e-30)


def _tile_lanes(x, n):
    return jnp.concatenate([x] * n, axis=1)


def _nsa_attn_body(q_ref, qaug_ref, glog_ref, kc_ref, vct_ref, ks_ref, vst_ref, kw_ref, vwt_ref,
                   mselt_ref, efullt_ref, o_ref, q2_ref, acct_ref, m_ref, l_ref, oacct_ref, s_ref, sc_ref, sw_ref,
                   sel_ref):
    qb = pl.program_id(2)
    Dh = LANES
    RB = NSA_ROW_BLOCK
    hpb = RB // Q_BLOCK
    n_rb = NSA_HG // hpb
    n_cmp_pad = kc_ref.shape[2]
    n_slc = mselt_ref.shape[0]
    win_keys = WINDOW + Q_BLOCK
    s0 = qb * Q_BLOCK
    t_q = s0 + lax.broadcasted_iota(I32, (1, Q_BLOCK), 1)

    gates_t = jax.nn.sigmoid(glog_ref[0]).T

    for hg in range(NSA_HG):
        q2_ref[hg * Q_BLOCK:(hg + 1) * Q_BLOCK, :Dh] = q_ref[0, :, hg * Dh:(hg + 1) * Dh]
    q2_ref[:, Dh:] = qaug_ref[0]

    cmp_end = lax.broadcasted_iota(I32, (n_cmp_pad, 1), 0) * CMP_STRIDE + (CMP_BLOCK - 1)
    cmp_mask = _tile_lanes(jnp.where(cmp_end <= t_q, 0.0, NEG_INF), hpb)

    ws = pl.multiple_of(jnp.maximum(s0 - WINDOW, 0), Q_BLOCK)
    wdist = t_q - (ws + lax.broadcasted_iota(I32, (win_keys, 1), 0))
    win_mask = _tile_lanes(jnp.where((wdist >= 0) & (wdist < WINDOW), 0.0, NEG_INF), hpb)

    kc = kc_ref[0, 0]
    vct = jnp.concatenate([vct_ref[0, 0, j] for j in range(n_cmp_pad // Dh)], axis=1)
    kwin = kw_ref[0, 0, pl.ds(ws, win_keys), :]
    wblk = ws // Dh
    vwt = jnp.concatenate([vwt_ref[0, 0, wblk + j] for j in range(win_keys // Dh)], axis=1)

    imp_t = jnp.zeros((n_cmp_pad, Q_BLOCK), F32)
    for rb in range(n_rb):
        q_rb = q2_ref[rb * RB:(rb + 1) * RB, :]
        sc_ref[rb] = _dot_nt(kc, q_rb) + cmp_mask
        sw_ref[rb] = _dot_nt(kwin, q_rb) + win_mask
    for rb in range(n_rb):
        e, inv = _col_softmax_terms(sc_ref[rb])
        p = e * inv
        for j in range(hpb):
            imp_t = imp_t + p[:, j * Q_BLOCK:(j + 1) * Q_BLOCK]
        o_c = jnp.dot(vct, p.astype(BF16), preferred_element_type=F32)
        e, inv = _col_softmax_terms(sw_ref[rb])
        o_w = jnp.dot(vwt, e.astype(BF16), preferred_element_type=F32) * inv
        for j in range(hpb):
            hg = rb * hpb + j
            sub = slice(j * Q_BLOCK, (j + 1) * Q_BLOCK)
            acct_ref[:, hg * Q_BLOCK:(hg + 1) * Q_BLOCK] = (
                gates_t[hg:hg + 1, :] * o_c[:, sub] + gates_t[2 * NSA_HG + hg:2 * NSA_HG + hg + 1, :] * o_w[:, sub])

    hi, mid, lo = _split3(imp_t)
    mselt = mselt_ref[...]
    slc_imp = (jnp.dot(mselt, hi, preferred_element_type=F32) + jnp.dot(mselt, mid, preferred_element_type=F32)
               + jnp.dot(mselt, lo, preferred_element_type=F32))

    blk = lax.broadcasted_iota(I32, (n_slc, 1), 0)
    cur = lax.shift_right_logical(t_q, 6)
    forced = (blk == 0) | (blk == cur) | (blk == cur - 1)
    score = jnp.where(forced, jnp.inf, jnp.where(blk <= cur, slc_imp, NEG_INF))
    sel = jnp.zeros((n_slc, Q_BLOCK), F32)
    for _ in range(SLC_TOPK):
        m = jnp.max(score, axis=0, keepdims=True)
        cand = (score == m) & (m > NEG_INF)
        first = jnp.min(jnp.where(cand, blk, n_slc), axis=0, keepdims=True)
        one = blk == first
        sel = jnp.where(one, 1.0, sel)
        score = jnp.where(one, NEG_INF, score)
    sel_b = sel.astype(BF16)
    sel_ref[...] = sel

    m_ref[...] = jnp.full(m_ref.shape, NEG_INF, F32)
    l_ref[...] = jnp.zeros(l_ref.shape, F32)
    oacct_ref[...] = jnp.zeros(oacct_ref.shape, F32)
    KC = SLC_KEY_CHUNK
    key_iota = lax.broadcasted_iota(I32, (KC, 1), 0)

    def chunk_step(c, carry):
        k0 = pl.multiple_of(c * KC, KC)
        bpc = KC // SLC_BLOCK
        picked = sel_ref[pl.ds(pl.multiple_of(c * bpc, bpc), bpc), :]
        any_picked = jnp.max(jnp.max(picked, axis=0, keepdims=True), axis=1, keepdims=True)[0, 0]

        @pl.when(any_picked > 0.5)
        def _():
            kch = ks_ref[0, 0, pl.ds(k0, KC), :]
            vt = jnp.concatenate([vst_ref[0, 0, c * (KC // Dh) + j] for j in range(KC // Dh)], axis=1)
            selk = jnp.dot(efullt_ref[pl.ds(k0, KC), :], sel_b, preferred_element_type=F32)
            mask = _tile_lanes(jnp.where((selk > 0.5) & (k0 + key_iota <= t_q), 0.0, NEG_INF), hpb)
            for rb in range(n_rb):
                s_ref[rb] = _dot_nt(kch, q2_ref[rb * RB:(rb + 1) * RB, :]) + mask
            for rb in range(n_rb):
                cols = slice(rb * RB, (rb + 1) * RB)
                s = s_ref[rb]
                m_old = m_ref[rb]
                m_new = jnp.maximum(m_old, jnp.max(s, axis=0, keepdims=True))
                m_safe = jnp.where(m_new == NEG_INF, 0.0, m_new)
                alpha = jnp.exp2(m_old - m_safe)
                p = jnp.exp2(s - m_safe)
                l_ref[rb] = alpha * l_ref[rb] + jnp.sum(p, axis=0, keepdims=True)
                oacct_ref[:, cols] = (alpha * oacct_ref[:, cols]
                                      + jnp.dot(vt, p.astype(BF16), preferred_element_type=F32))
                m_ref[rb] = m_new
        return carry

    n_chunks = (s0 + Q_BLOCK + KC - 1) // KC
    lax.fori_loop(0, n_chunks, chunk_step, 0)
    for hg in range(NSA_HG):
        rb, j = divmod(hg, hpb)
        cols = slice(hg * Q_BLOCK, (hg + 1) * Q_BLOCK)
        inv = 1.0 / jnp.maximum(l_ref[rb][:, j * Q_BLOCK:(j + 1) * Q_BLOCK], 1e-30)
        o_t = acct_ref[:, cols] + gates_t[NSA_HG + hg:NSA_HG + hg + 1, :] * (oacct_ref[:, cols] * inv)
        o_ref[0, :, hg * Dh:(hg + 1) * Dh] = o_t.T.astype(o_ref.dtype)


def _pos_pieces(pos):
    pos = np.asarray(pos)
    out = np.zeros((pos.shape[0], LANES), np.float32)
    for i in range(3):
        out[:, 2 * i] = 64 * (pos // 64)
        out[:, 2 * i + 1] = pos % 64
    return jnp.asarray(out, BF16)


def _slope_pieces():
    H, G, HG = NSA_HEADS, NSA_GROUPS, NSA_HG
    slopes = jnp.asarray(LOG2E * 2.0 ** (-8.0 * np.arange(1, H + 1, dtype=np.float64) / H), F32)
    pieces = jnp.stack(_split3(slopes), axis=-1)
    cols = jnp.repeat(pieces, 2, axis=-1)
    cols = jnp.pad(cols, ((0, 0), (0, LANES - cols.shape[-1])))
    return jnp.repeat(cols.reshape(G, HG, 1, LANES), Q_BLOCK, axis=2).reshape(G, HG * Q_BLOCK, LANES)


def _blocked_transpose(v):
    lead, (t, dh) = v.shape[:-2], v.shape[-2:]
    return jnp.swapaxes(v.reshape(*lead, t // LANES, LANES, dh), -1, -2)


def nsa_attention(q, glog, kvc, zz, B, T):
    G, HG, Dh = NSA_GROUPS, NSA_HG, LANES
    H = NSA_HEADS
    n_cmp = T // CMP_STRIDE - CMP_BLOCK // CMP_STRIDE + 1
    n_cmp_pad = -(-T // CMP_STRIDE // LANES) * LANES
    n_slc = T // SLC_BLOCK
    per = SLC_BLOCK // CMP_STRIDE
    n = np.arange(n_cmp_pad)[None, :]
    j = np.arange(n_slc)[:, None]
    mselt = ((n // per == j).astype(np.float32) + ((n + 1) // per == j).astype(np.float32))
    mselt[:, n_cmp:] = 0.0
    kvc = jnp.pad(kvc, ((0, 0), (0, 0), (0, 0), (0, n_cmp_pad - kvc.shape[3]), (0, 0)))
    efullt = (np.arange(T)[:, None] // SLC_BLOCK == np.arange(n_slc)[None, :]).astype(np.float32)

    key_aug = jnp.broadcast_to(_pos_pieces(np.arange(T)), (G, B, T, LANES))
    cmp_aug = jnp.broadcast_to(_pos_pieces(np.arange(n_cmp_pad) * CMP_STRIDE + CMP_BLOCK - 1),
                               (B, G, n_cmp_pad, LANES))
    kc_aug = jnp.concatenate([kvc[0], cmp_aug], axis=-1)
    vct = _blocked_transpose(kvc[1])
    ks_aug = jnp.concatenate([zz[2 * G:3 * G], key_aug], axis=-1)
    kw_aug = jnp.concatenate([zz[4 * G:5 * G], key_aug], axis=-1)
    vst = _blocked_transpose(zz[3 * G:4 * G])
    vwt = _blocked_transpose(zz[5 * G:6 * G])

    def k_spec():
        return pl.BlockSpec((1, 1, T, 2 * Dh), lambda b, g, i: (g, b, 0, 0))

    def vt_spec():
        return pl.BlockSpec((1, 1, T // LANES, Dh, LANES), lambda b, g, i: (g, b, 0, 0, 0))

    n_rb = HG * Q_BLOCK // NSA_ROW_BLOCK
    return pl.pallas_call(
        _nsa_attn_body,
        grid=(B, G, T // Q_BLOCK),
        in_specs=[pl.BlockSpec((1, Q_BLOCK, HG * Dh), lambda b, g, i: (b, i, g)),
                  pl.BlockSpec((1, HG * Q_BLOCK, LANES), lambda b, g, i: (g, 0, 0)),
                  pl.BlockSpec((1, Q_BLOCK, LANES), lambda b, g, i: (b, i, g)),
                  pl.BlockSpec((1, 1, n_cmp_pad, 2 * Dh), lambda b, g, i: (b, g, 0, 0)),
                  pl.BlockSpec((1, 1, n_cmp_pad // LANES, Dh, LANES), lambda b, g, i: (b, g, 0, 0, 0)),
                  k_spec(), vt_spec(), k_spec(), vt_spec(),
                  pl.BlockSpec((n_slc, n_cmp_pad), lambda b, g, i: (0, 0)),
                  pl.BlockSpec((T, n_slc), lambda b, g, i: (0, 0))],
        out_specs=pl.BlockSpec((1, Q_BLOCK, HG * Dh), lambda b, g, i: (b, i, g)),
        out_shape=jax.ShapeDtypeStruct((B, T, H * Dh), BF16),
        scratch_shapes=[pltpu.VMEM((HG * Q_BLOCK, 2 * Dh), BF16),
                        pltpu.VMEM((Dh, HG * Q_BLOCK), F32),
                        pltpu.VMEM((n_rb, 1, NSA_ROW_BLOCK), F32),
                        pltpu.VMEM((n_rb, 1, NSA_ROW_BLOCK), F32),
                        pltpu.VMEM((Dh, HG * Q_BLOCK), F32),
                        pltpu.VMEM((n_rb, SLC_KEY_CHUNK, NSA_ROW_BLOCK), F32),
                        pltpu.VMEM((n_rb, n_cmp_pad, NSA_ROW_BLOCK), F32),
                        pltpu.VMEM((n_rb, WINDOW + Q_BLOCK, NSA_ROW_BLOCK), F32),
                        pltpu.VMEM((n_slc, Q_BLOCK), F32)],
        compiler_params=_params("arbitrary", "arbitrary", "arbitrary"),
        name="nsa_attention",
    )(q, _slope_pieces(), glog, kc_aug, vct, ks_aug, vst, kw_aug, vwt,
      jnp.asarray(mselt, BF16), jnp.asarray(efullt, BF16))


def _pack_rows(y, o_ref):
    m, d = y.shape
    bits = lax.bitcast_convert_type(y.astype(BF16).astype(F32), U32)
    for c in range(d // (2 * LANES)):
        lo = lax.shift_right_logical(bits[:, c * LANES:(c + 1) * LANES], jnp.uint32(16))
        hi = bits[:, d // 2 + c * LANES:d // 2 + (c + 1) * LANES]
        o_ref[pl.ds(c, m, stride=d // (2 * LANES)), :] = lo | hi


def _unpack_chunk(words):
    lo = lax.bitcast_convert_type(lax.shift_left(words, jnp.uint32(16)), F32)
    hi = lax.bitcast_convert_type(words & jnp.uint32(0xFFFF0000), F32)
    return lo, hi


def _moe_router_body(h_ref, g_ref, w_ref, b_ref, tri_ref, xp_ref, idx_ref, wgt_ref, rank_ref, cnt_ref):
    @pl.when(pl.program_id(0) == 0)
    def _():
        cnt_ref[...] = jnp.zeros(cnt_ref.shape, F32)

    x = h_ref[...]
    ms = jnp.mean(x * x, axis=-1, keepdims=True)
    xn = x * lax.rsqrt(ms + NORM_EPS) * g_ref[...]
    _pack_rows(xn, xp_ref)
    logits = jnp.dot(xn.astype(BF16), w_ref[...], preferred_element_type=F32) + b_ref[...]
    lane = lax.broadcasted_iota(I32, logits.shape, 1)
    logits = jnp.where(lane < N_EXPERTS, logits, NEG_INF)
    idx_out = jnp.zeros(logits.shape, I32)
    val_out = jnp.full(logits.shape, NEG_INF, F32)
    picked = []
    for k in range(TOP_K):
        m = jnp.max(logits, axis=-1, keepdims=True)
        first = jnp.min(jnp.where(logits == m, lane, LANES), axis=-1, keepdims=True)
        idx_out = jnp.where(lane == k, first, idx_out)
        val_out = jnp.where(lane == k, m, val_out)
        picked.append(jnp.where(lane == first, 1.0, 0.0))
        logits = jnp.where(lane == first, NEG_INF, logits)
    e = jnp.exp(val_out - jnp.max(val_out, axis=-1, keepdims=True))
    idx_ref[...] = idx_out
    wgt_ref[...] = e * (1.0 / jnp.sum(e, axis=-1, keepdims=True))

    base = cnt_ref[0:1, :]
    rank_out = jnp.zeros(logits.shape, F32)
    for k in range(TOP_K):
        earlier = jnp.dot(tri_ref[...], picked[k].astype(BF16), preferred_element_type=F32)
        r = jnp.sum(picked[k] * (base + earlier), axis=-1, keepdims=True)
        rank_out = jnp.where(lane == k, r, rank_out)
        base = base + jnp.sum(picked[k], axis=0, keepdims=True)
    rank_ref[...] = rank_out.astype(I32)
    cnt_ref[...] = jnp.broadcast_to(base, cnt_ref.shape)


def moe_router(h, gain, w_router, b_router, tm=256):
    n, d = h.shape
    tm = min(tm, n)
    pr = d // (2 * LANES)
    w = _pad_cols(w_router).astype(BF16)
    b = _pad_cols(b_router.reshape(1, -1)).astype(F32)
    tri = jnp.asarray(np.tril(np.ones((tm, tm), np.float32), -1), BF16)
    return pl.pallas_call(
        _moe_router_body,
        grid=(n // tm,),
        in_specs=[pl.BlockSpec((tm, d), lambda i: (i, 0)),
                  pl.BlockSpec((1, d), lambda i: (0, 0)),
                  pl.BlockSpec((d, LANES), lambda i: (0, 0)),
                  pl.BlockSpec((1, LANES), lambda i: (0, 0)),
                  pl.BlockSpec((tm, tm), lambda i: (0, 0))],
        out_specs=[pl.BlockSpec((tm * pr, LANES), lambda i: (i, 0)),
                   pl.BlockSpec((tm, LANES), lambda i: (i, 0)),
                   pl.BlockSpec((tm, LANES), lambda i: (i, 0)),
                   pl.BlockSpec((tm, LANES), lambda i: (i, 0)),
                   pl.BlockSpec((8, LANES), lambda i: (0, 0))],
        out_shape=[jax.ShapeDtypeStruct((n * pr, LANES), U32),
                   jax.ShapeDtypeStruct((n, LANES), I32), jax.ShapeDtypeStruct((n, LANES), F32),
                   jax.ShapeDtypeStruct((n, LANES), I32), jax.ShapeDtypeStruct((8, LANES), F32)],
        compiler_params=_params("arbitrary"),
        name="moe_router",
    )(h, gain.reshape(1, d).astype(F32), w, b, tri)


def _row_gather_copy(src_hbm, src_row, dst_buf, slot, r, sem, pr):
    return pltpu.make_async_copy(src_hbm.at[pl.ds(pl.multiple_of(src_row, pr), pr)],
                                 dst_buf.at[slot, pl.ds(pl.multiple_of(r * pr, pr), pr)], sem.at[slot])


def _start_row_gather(ids_ref, src_hbm, dst_buf, slot, sem, n_rows, pr):
    def body(r, carry):
        _row_gather_copy(src_hbm, ids_ref[0, 0, r], dst_buf, slot, r, sem, pr).start()
        return carry
    lax.fori_loop(0, n_rows, body, 0, unroll=8)


def _wait_row_gather(src_hbm, dst_buf, slot, sem):
    pltpu.make_async_copy(src_hbm.at[pl.ds(0, dst_buf.shape[1])], dst_buf.at[slot], sem.at[slot]).wait()


def _moe_expert_body(blk_e_ref, n_used_ref, ids0_ref, idsn_ref, x_hbm, wgu_ref, bgu_ref, wd_ref, bd_ref,
                     y_ref, xbuf, xs, sem):
    i = pl.program_id(0)
    n_used = n_used_ref[0]
    rows, d = xs.shape
    pr = d // (2 * LANES)
    ff = wd_ref.shape[2]

    @pl.when(i == 0)
    def _():
        _start_row_gather(ids0_ref, x_hbm, xbuf, 0, sem, rows, pr)

    @pl.when(i + 1 < n_used)
    def _():
        _start_row_gather(idsn_ref, x_hbm, xbuf, (i + 1) % 2, sem, rows, pr)

    @pl.when(i < n_used)
    def _():
        slot = i % 2
        _wait_row_gather(x_hbm, xbuf, slot, sem)
        sub = rows // MOE_SUB_BLOCKS
        for sb in range(MOE_SUB_BLOCKS):
            r0 = sb * sub
            for c in range(pr):
                lo, hi = _unpack_chunk(xbuf[slot, pl.ds(r0 * pr + c, sub, stride=pr), :])
                xs[r0:r0 + sub, c * LANES:(c + 1) * LANES] = lo.astype(BF16)
                xs[r0:r0 + sub, d // 2 + c * LANES:d // 2 + (c + 1) * LANES] = hi.astype(BF16)
            gu = jnp.dot(xs[r0:r0 + sub, :], wgu_ref[0, 0], preferred_element_type=F32) + bgu_ref[0]
            gate = jnp.minimum(gu[:, :ff], SWIGLU_LIMIT)
            up = jnp.clip(gu[:, ff:], -SWIGLU_LIMIT, SWIGLU_LIMIT)
            act = (up + 1.0) * (gate * jax.nn.sigmoid(SWIGLU_ALPHA * gate))
            y = jnp.dot(act.astype(BF16), wd_ref[0, 0], preferred_element_type=F32) + bd_ref[0]
            _pack_rows(y, y_ref.at[pl.ds(r0 * pr, sub * pr), :])

    @pl.when(i >= n_used)
    def _():
        y_ref[...] = jnp.zeros(y_ref.shape, y_ref.dtype)


def moe_experts(xp, tok_rows, blk_e, n_used, w_gate_up, b_gate_up, w_down, b_down, l):
    _, E, d, ff2 = w_gate_up.shape
    ff = ff2 // 2
    pr = d // (2 * LANES)
    rows = MOE_BLOCK_ROWS
    n_blocks = tok_rows.shape[0] // rows
    ids = tok_rows.reshape(n_blocks, 1, rows)

    def used(i, nu):
        return jnp.minimum(i, nu[0] - 1)

    grid_spec = pltpu.PrefetchScalarGridSpec(
        num_scalar_prefetch=2,
        grid=(n_blocks,),
        in_specs=[pl.BlockSpec((1, 1, rows), lambda i, be, nu: (0, 0, 0), memory_space=pltpu.SMEM),
                  pl.BlockSpec((1, 1, rows), lambda i, be, nu: (jnp.minimum(i + 1, n_blocks - 1), 0, 0),
                               memory_space=pltpu.SMEM),
                  pl.BlockSpec(memory_space=pl.ANY),
                  pl.BlockSpec((1, 1, d, ff2), lambda i, be, nu: (l, be[used(i, nu)], 0, 0)),
                  pl.BlockSpec((1, 1, ff2), lambda i, be, nu: (be[used(i, nu)], 0, 0)),
                  pl.BlockSpec((1, 1, ff, d), lambda i, be, nu: (l, be[used(i, nu)], 0, 0)),
                  pl.BlockSpec((1, 1, d), lambda i, be, nu: (be[used(i, nu)], 0, 0))],
        out_specs=pl.BlockSpec((rows * pr, LANES), lambda i, be, nu: (i, 0)),
        scratch_shapes=[pltpu.VMEM((2, rows * pr, LANES), U32), pltpu.VMEM((rows, d), BF16),
                        pltpu.SemaphoreType.DMA((2,))],
    )
    return pl.pallas_call(
        _moe_expert_body,
        grid_spec=grid_spec,
        out_shape=jax.ShapeDtypeStruct((n_blocks * rows * pr, LANES), U32),
        compiler_params=_params("arbitrary"),
        name="moe_experts",
    )(blk_e, n_used, ids, ids, xp, w_gate_up, b_gate_up.reshape(E, 1, ff2).astype(F32),
      w_down, b_down.reshape(E, 1, d).astype(F32))


def _moe_combine_body(ids0_ref, idsn_ref, y_hbm, h_ref, w_ref, g_ref, o_ref, xn_ref, ybuf, sem):
    i = pl.program_id(0)
    n_steps = pl.num_programs(0)
    tt, d = h_ref.shape
    pr = d // (2 * LANES)
    rows = TOP_K * tt

    @pl.when(i == 0)
    def _():
        _start_row_gather(ids0_ref, y_hbm, ybuf, 0, sem, rows, pr)

    @pl.when(i + 1 < n_steps)
    def _():
        _start_row_gather(idsn_ref, y_hbm, ybuf, (i + 1) % 2, sem, rows, pr)

    slot = i % 2
    _wait_row_gather(y_hbm, ybuf, slot, sem)
    w = w_ref[...]
    wk = [jnp.broadcast_to(w[:, k:k + 1], (tt, LANES)) for k in range(TOP_K)]
    ssq = jnp.zeros((tt, LANES), F32)
    for c in range(pr):
        lo_cols = slice(c * LANES, (c + 1) * LANES)
        hi_cols = slice(d // 2 + c * LANES, d // 2 + (c + 1) * LANES)
        acc_lo = h_ref[:, lo_cols]
        acc_hi = h_ref[:, hi_cols]
        for k in range(TOP_K):
            lo, hi = _unpack_chunk(ybuf[slot, pl.ds(k * tt * pr + c, tt, stride=pr), :])
            acc_lo = acc_lo + wk[k] * lo
            acc_hi = acc_hi + wk[k] * hi
        o_ref[:, lo_cols] = acc_lo
        o_ref[:, hi_cols] = acc_hi
        ssq = ssq + acc_lo * acc_lo + acc_hi * acc_hi
    inv = lax.rsqrt(jnp.sum(ssq, axis=-1, keepdims=True) * (1.0 / d) + NORM_EPS)
    xn_ref[...] = (o_ref[...] * inv * g_ref[...]).astype(xn_ref.dtype)


def moe_combine(yp, slot_rows, weights, h, next_gain):
    n, d = h.shape
    pr = d // (2 * LANES)
    tt = min(MOE_COMBINE_TOKENS, n)
    n_steps = n // tt
    ids = slot_rows.reshape(n_steps, tt, TOP_K).transpose(0, 2, 1).reshape(n_steps, 1, TOP_K * tt)
    return pl.pallas_call(
        _moe_combine_body,
        grid=(n_steps,),
        in_specs=[pl.BlockSpec((1, 1, TOP_K * tt), lambda i: (0, 0, 0), memory_space=pltpu.SMEM),
                  pl.BlockSpec((1, 1, TOP_K * tt), lambda i: (jnp.minimum(i + 1, n_steps - 1), 0, 0),
                               memory_space=pltpu.SMEM),
                  pl.BlockSpec(memory_space=pl.ANY),
                  pl.BlockSpec((tt, d), lambda i: (i, 0)),
                  pl.BlockSpec((tt, LANES), lambda i: (i, 0)),
                  pl.BlockSpec((1, d), lambda i: (0, 0))],
        out_specs=[pl.BlockSpec((tt, d), lambda i: (i, 0)),
                   pl.BlockSpec((tt, d), lambda i: (i, 0))],
        out_shape=[jax.ShapeDtypeStruct((n, d), F32), jax.ShapeDtypeStruct((n, d), BF16)],
        scratch_shapes=[pltpu.VMEM((2, TOP_K * tt * pr, LANES), U32), pltpu.SemaphoreType.DMA((2,))],
        compiler_params=_params("arbitrary"),
        name="moe_combine",
    )(ids, ids, yp, h, weights, next_gain.reshape(1, d).astype(F32))


def _moe_plan(top_idx, rank, sizes, n_rows):
    E, rows = N_EXPERTS, MOE_BLOCK_ROWS
    flat_e = top_idx.reshape(-1)
    rank = rank.reshape(-1)
    nk = flat_e.shape[0]
    nblk = (sizes + rows - 1) // rows
    blk_end = jnp.cumsum(nblk)
    pad_start = (blk_end - nblk) * rows
    slot_dest = pad_start[flat_e] + rank
    tok_buf = jnp.zeros((n_rows,), I32).at[slot_dest].set(jnp.arange(nk, dtype=I32) // TOP_K)
    n_blocks = n_rows // rows
    blk_e = jnp.sum((blk_end[None, :] <= jnp.arange(n_blocks, dtype=I32)[:, None]).astype(I32), axis=1)
    blk_e = jnp.minimum(blk_e, E - 1)
    n_used = blk_end[-1:].astype(I32)
    return slot_dest, tok_buf, blk_e, n_used


def moe_layer(h, gain, w_router, b_router, w_gate_up, b_gate_up, w_down, b_down, l, next_gain):
    n, d = h.shape
    pr = d // (2 * LANES)
    rows = MOE_BLOCK_ROWS
    xp, top_idx, weights, rank, counts = moe_router(h, gain, w_router, b_router)
    n_rows = n * TOP_K + N_EXPERTS * rows
    slot_dest, tok_buf, blk_e, n_used = _moe_plan(top_idx[:, :TOP_K], rank[:, :TOP_K],
                                                  counts[0, :N_EXPERTS].astype(I32), n_rows)
    yp = moe_experts(xp, tok_buf * pr, blk_e, n_used, w_gate_up, b_gate_up, w_down, b_down, l)
    return moe_combine(yp, slot_dest.reshape(n, TOP_K) * pr, weights, h, next_gain)


def _gla_mixer(h, xn, w_in, w_gate2, b_gate2, out_gain, w_out, l, B, T):
    n_main = w_in.shape[2] - GLA_GATE_RANK
    w_in_t = jnp.swapaxes(w_in, 1, 2)
    proj = matmul(xn, w_in_t, layer=l, n=n_main, w_transposed=True, out_dtype=BF16)
    wa_t = jnp.pad(w_in_t[l, n_main:], ((0, LANES - GLA_GATE_RANK), (0, 0)))
    a_pad = matmul(xn, wa_t[None], w_transposed=True)
    o = gla_core(proj, a_pad, w_gate2, b_gate2, out_gain, B, T)
    return matmul(o, w_out, layer=l, residual=h)


def _nsa_shared_kv(h, kv_gain, w_kv, k_gain, cmp_pe, cmp_w1, cmp_w2, B, T):
    G = NSA_GROUPS
    Dh = LANES
    xn = rmsnorm(h, kv_gain)
    ones = jnp.ones((G * Dh,), F32)
    gain_cols = jnp.concatenate([ones, ones, jnp.tile(k_gain[1], G), ones, jnp.tile(k_gain[2], G), ones])
    norm_groups = [False] * (2 * G) + [True] * G + [False] * G + [True] * G + [False] * G
    zz = matmul_groupnorm(xn, w_kv.astype(BF16)[None], gain_cols, norm_groups, split_out=True, tn=6 * G * Dh)
    kvc = nsa_compress(zz, cmp_pe, cmp_w1, cmp_w2, k_gain[0], B, T)
    return kvc, zz.reshape(6 * G, B, T, Dh)


def _nsa_mixer(h, xn, w_in, q_gain, w_out, kvc, zz, j, B, T):
    H, G, HG, Dh = NSA_HEADS, NSA_GROUPS, NSA_HG, LANES
    w_in_t = jnp.swapaxes(w_in, 1, 2)
    q = matmul_groupnorm(xn, w_in_t, jnp.tile(q_gain, H), [True] * 4, layer=j, n=H * Dh, w_transposed=True,
                         scale=Dh ** -0.5 * LOG2E, tm=1024, tn=512)
    wg = w_in_t[j, H * Dh:].reshape(G, HG, 3, -1).transpose(0, 2, 1, 3).reshape(G, 3 * HG, -1)
    wg = jnp.pad(wg, ((0, 0), (0, LANES - 3 * HG), (0, 0))).reshape(1, G * LANES, -1)
    glog = matmul(xn, wg, w_transposed=True)
    o = nsa_attention(q.reshape(B, T, H * Dh), glog.reshape(B, T, G * LANES), kvc, zz, B, T)
    return matmul(o.reshape(B * T, H * Dh), w_out, layer=j, residual=h)


def kernel(x, ln_mix, ln_ffn, a_w_in, a_w_gate2, a_b_gate2, a_out_gain, a_w_out, kv_gain, w_kv, k_gain,
           cmp_pe, cmp_w1, cmp_w2, b_w_in, b_q_gain, b_w_out, w_router, b_router, w_gate_up, b_gate_up,
           w_down, b_down):
    B, T, D = x.shape
    depth = ln_mix.shape[0]
    n_a = a_w_in.shape[0]
    h = x.reshape(B * T, D)
    shared = None
    wgu_bf = w_gate_up.astype(BF16)
    wd_bf = w_down.astype(BF16)
    xn = rmsnorm(h, ln_mix[0])
    for l in range(depth):
        if l < n_a:
            h = _gla_mixer(h, xn, a_w_in, a_w_gate2[l], a_b_gate2[l], a_out_gain[l], a_w_out, l, B, T)
        else:
            if l == n_a:
                shared = _nsa_shared_kv(h, kv_gain, w_kv, k_gain, cmp_pe, cmp_w1, cmp_w2, B, T)
            j = l - n_a
            h = _nsa_mixer(h, xn, b_w_in, b_q_gain[j], b_w_out, *shared, j, B, T)
        h, xn = moe_layer(h, ln_ffn[l], w_router[l], b_router[l], wgu_bf, b_gate_up[l], wd_bf, b_down[l], l,
                          ln_mix[min(l + 1, depth - 1)])
    return h.reshape(B, T, D)
```

```python
import functools
import math

import jax
import jax.numpy as jnp
from jax import lax
import numpy as np
from jax.experimental import pallas as pl
from jax.experimental.pallas import tpu as pltpu

F32 = jnp.float32
BF16 = jnp.bfloat16
I32 = jnp.int32
U32 = jnp.uint32

NORM_EPS = 1e-5
GLA_HEADS = 8
GLA_GATE_RANK = 16
GLA_GATE_TAU = 16.0
NSA_HEADS = 32
NSA_GROUPS = 2
NSA_HG = NSA_HEADS // NSA_GROUPS
CMP_BLOCK = 32
CMP_STRIDE = 16
SLC_BLOCK = 64
SLC_TOPK = 8
WINDOW = 512
Q_BLOCK = 128
N_EXPERTS = 32
TOP_K = 4
SWIGLU_ALPHA = 1.702
SWIGLU_LIMIT = 7.0

LANES = 128
MXU_DIM = 256
VMEM_LIMIT_BYTES = 56 * 1024 * 1024
MOE_BLOCK_ROWS = 512
MOE_SUB_BLOCKS = 4
MOE_COMBINE_TOKENS = 128
SLC_KEY_CHUNK = 512
NSA_ROW_BLOCK = 256
GLA_BLOCK = 128
GLA_STEP_TOKENS = 512
GLA_HEADS_PER_STEP = 2
NEG_INF = float("-inf")
LOG2E = math.log2(math.e)


def _params(*sem):
    return pltpu.CompilerParams(dimension_semantics=sem, vmem_limit_bytes=VMEM_LIMIT_BYTES)


def _split3(x):
    hi = x.astype(BF16)
    r1 = x - hi.astype(F32)
    mid = r1.astype(BF16)
    lo = (r1 - mid.astype(F32)).astype(BF16)
    return hi, mid, lo


def _rmsnorm_body(x_ref, g_ref, o_ref):
    x = x_ref[...].astype(F32)
    ms = jnp.mean(x * x, axis=-1, keepdims=True)
    o_ref[...] = (x * lax.rsqrt(ms + NORM_EPS) * g_ref[...].astype(F32)).astype(o_ref.dtype)


def rmsnorm(x, gain, out_dtype=BF16, tm=512):
    m, d = x.shape
    tm = min(tm, m)
    return pl.pallas_call(
        _rmsnorm_body,
        grid=(m // tm,),
        in_specs=[pl.BlockSpec((tm, d), lambda i: (i, 0)),
                  pl.BlockSpec((1, d), lambda i: (0, 0))],
        out_specs=pl.BlockSpec((tm, d), lambda i: (i, 0)),
        out_shape=jax.ShapeDtypeStruct((m, d), out_dtype),
        compiler_params=_params("arbitrary"),
        name="rmsnorm",
    )(x, gain.reshape(1, d))


def _tile_dot(a_ref, w_ref, w_transposed):
    w = w_ref[0].astype(BF16)
    dims = (((1,), (1,)), ((), ())) if w_transposed else (((1,), (0,)), ((), ()))
    return lax.dot_general(a_ref[...], w, dims, preferred_element_type=F32)


def _w_spec(k, tn, layer, w_transposed):
    if w_transposed:
        return pl.BlockSpec((1, tn, k), lambda i, j: (layer, j, 0))
    return pl.BlockSpec((1, k, tn), lambda i, j: (layer, 0, j))


def _matmul_body(a_ref, w_ref, o_ref, *, w_transposed):
    o_ref[...] = _tile_dot(a_ref, w_ref, w_transposed).astype(o_ref.dtype)


def _matmul_res_body(a_ref, w_ref, r_ref, o_ref, *, w_transposed):
    o_ref[...] = (r_ref[...].astype(F32) + _tile_dot(a_ref, w_ref, w_transposed)).astype(o_ref.dtype)


def matmul(a, w, layer=0, n=None, residual=None, w_transposed=False, out_dtype=F32, tm=1024, tn=512):
    m, k = a.shape
    n = w.shape[1 if w_transposed else 2] if n is None else n
    tm = min(tm, m)
    tn = min(tn, n)
    assert m % tm == 0 and n % tn == 0, (m, n, tm, tn)
    in_specs = [pl.BlockSpec((tm, k), lambda i, j: (i, 0)), _w_spec(k, tn, layer, w_transposed)]
    args = [a, w]
    body = functools.partial(_matmul_body, w_transposed=w_transposed)
    if residual is not None:
        in_specs.append(pl.BlockSpec((tm, tn), lambda i, j: (i, j)))
        args.append(residual)
        body = functools.partial(_matmul_res_body, w_transposed=w_transposed)
    return pl.pallas_call(
        body,
        grid=(m // tm, n // tn),
        in_specs=in_specs,
        out_specs=pl.BlockSpec((tm, tn), lambda i, j: (i, j)),
        out_shape=jax.ShapeDtypeStruct((m, n), out_dtype),
        compiler_params=_params("arbitrary", "arbitrary"),
        name="matmul",
    )(*args)


def _matmul_groupnorm_body(a_ref, w_ref, g_ref, o_ref, *, norm_groups, scale, split_out, w_transposed):
    acc = _tile_dot(a_ref, w_ref, w_transposed)
    for c, do_norm in enumerate(norm_groups):
        seg = acc[:, c * LANES:(c + 1) * LANES]
        if do_norm:
            ms = jnp.mean(seg * seg, axis=-1, keepdims=True)
            seg = seg * lax.rsqrt(ms + NORM_EPS) * g_ref[:, c * LANES:(c + 1) * LANES] * scale
        if split_out:
            o_ref[c] = seg.astype(o_ref.dtype)
        else:
            o_ref[:, c * LANES:(c + 1) * LANES] = seg.astype(o_ref.dtype)


def matmul_groupnorm(a, w, gain_cols, norm_groups, layer=0, n=None, w_transposed=False, scale=1.0, split_out=False,
                     out_dtype=BF16, tm=512, tn=512):
    m, k = a.shape
    n = w.shape[1 if w_transposed else 2] if n is None else n
    tm = min(tm, m)
    tn = min(tn, n)
    assert m % tm == 0 and n % tn == 0 and len(norm_groups) == tn // LANES
    if split_out:
        out_shape = jax.ShapeDtypeStruct((n // LANES, m, LANES), out_dtype)
        out_spec = pl.BlockSpec((tn // LANES, tm, LANES), lambda i, j: (j, i, 0))
    else:
        out_shape = jax.ShapeDtypeStruct((m, n), out_dtype)
        out_spec = pl.BlockSpec((tm, tn), lambda i, j: (i, j))
    body = functools.partial(_matmul_groupnorm_body, norm_groups=tuple(norm_groups), scale=scale,
                             split_out=split_out, w_transposed=w_transposed)
    return pl.pallas_call(
        body,
        grid=(m // tm, n // tn),
        in_specs=[pl.BlockSpec((tm, k), lambda i, j: (i, 0)),
                  _w_spec(k, tn, layer, w_transposed),
                  pl.BlockSpec((1, tn), lambda i, j: (0, j))],
        out_specs=out_spec,
        out_shape=out_shape,
        compiler_params=_params("arbitrary", "arbitrary"),
        name="matmul_groupnorm",
    )(a, w, gain_cols.reshape(1, n).astype(F32))


def _pad_cols(w, mult=LANES):
    pad = (-w.shape[-1]) % mult
    return jnp.pad(w, ((0, 0), (0, pad))) if pad else w


def _gla_body(q_ref, k_ref, v_ref, g_ref, a_ref, wg_ref, bg_ref, og_ref, tri_ref, o_ref, s_ref):
    dk, dv = s_ref.shape[1], s_ref.shape[2]
    C = GLA_BLOCK

    @pl.when(pl.program_id(2) == 0)
    def _():
        s_ref[...] = jnp.zeros(s_ref.shape, F32)

    tri = tri_ref[...]
    row = lax.broadcasted_iota(I32, (C, C), 0)
    col = lax.broadcasted_iota(I32, (C, C), 1)
    for c in range(q_ref.shape[0] // C):
        rows = slice(c * C, (c + 1) * C)
        a = a_ref[rows, :].astype(BF16)
        for hh in range(s_ref.shape[0]):
            kc = slice(hh * dk, (hh + 1) * dk)
            vc = slice(hh * dv, (hh + 1) * dv)
            gate_in = jnp.dot(a, wg_ref[:, kc], preferred_element_type=F32) + bg_ref[:, kc]
            log_a = jax.nn.log_sigmoid(gate_in) * (1.0 / GLA_GATE_TAU)
            hi, mid, lo = _split3(log_a)
            bcum = (jnp.dot(tri, hi, preferred_element_type=F32) + jnp.dot(tri, mid, preferred_element_type=F32)
                    + jnp.dot(tri, lo, preferred_element_type=F32))
            b_mid = bcum[C // 2 - 1:C // 2, :]
            q = q_ref[rows, kc].astype(F32) * dk ** -0.5
            k = k_ref[rows, kc].astype(F32)
            v = v_ref[rows, vc]
            att = lax.dot_general((q * jnp.exp(bcum - b_mid)).astype(BF16),
                                  (k * jnp.exp(b_mid - bcum)).astype(BF16),
                                  (((1,), (1,)), ((), ())), preferred_element_type=F32)
            att = jnp.where(col <= row, att, 0.0)
            o = jnp.dot(att.astype(BF16), v, preferred_element_type=F32)
            o = o + jnp.dot((q * jnp.exp(bcum)).astype(BF16), s_ref[hh].astype(BF16), preferred_element_type=F32)
            bcum_t = bcum.T
            b_last = bcum_t[:, C - 1:C]
            k_t = (k.T * jnp.exp(b_last - bcum_t)).astype(BF16)
            s_ref[hh] = s_ref[hh] * jnp.exp(b_last) + jnp.dot(k_t, v, preferred_element_type=F32)
            ms = jnp.mean(o * o, axis=-1, keepdims=True)
            o = o * lax.rsqrt(ms + NORM_EPS) * og_ref[...]
            o_ref[rows, vc] = (o * jax.nn.silu(g_ref[rows, vc].astype(F32))).astype(o_ref.dtype)


def gla_core(proj, a_pad, w_gate2, b_gate2, out_gain, B, T):
    H = GLA_HEADS
    n = proj.shape[0]
    dv = out_gain.shape[0]
    dk = w_gate2.shape[1] // H
    tb = min(GLA_STEP_TOKENS, T)
    nt = T // tb
    wg = jnp.pad(w_gate2, ((0, LANES - w_gate2.shape[0]), (0, 0))).astype(BF16)
    tri = jnp.asarray(np.tril(np.ones((GLA_BLOCK, GLA_BLOCK), np.float32)), BF16)
    hp = GLA_HEADS_PER_STEP
    hs = H // hp
    v0 = 2 * H * dk // (hp * dv)
    return pl.pallas_call(
        _gla_body,
        grid=(B, hs, nt),
        in_specs=[pl.BlockSpec((tb, hp * dk), lambda b, h, i: (b * nt + i, h)),
                  pl.BlockSpec((tb, hp * dk), lambda b, h, i: (b * nt + i, hs + h)),
                  pl.BlockSpec((tb, hp * dv), lambda b, h, i: (b * nt + i, v0 + h)),
                  pl.BlockSpec((tb, hp * dv), lambda b, h, i: (b * nt + i, v0 + hs + h)),
                  pl.BlockSpec((tb, LANES), lambda b, h, i: (b * nt + i, 0)),
                  pl.BlockSpec((LANES, hp * dk), lambda b, h, i: (0, h)),
                  pl.BlockSpec((1, hp * dk), lambda b, h, i: (0, h)),
                  pl.BlockSpec((1, dv), lambda b, h, i: (0, 0)),
                  pl.BlockSpec((GLA_BLOCK, GLA_BLOCK), lambda b, h, i: (0, 0))],
        out_specs=pl.BlockSpec((tb, hp * dv), lambda b, h, i: (b * nt + i, h)),
        out_shape=jax.ShapeDtypeStruct((n, H * dv), BF16),
        scratch_shapes=[pltpu.VMEM((hp, dk, dv), F32)],
        compiler_params=_params("arbitrary", "arbitrary", "arbitrary"),
        name="gla_core",
    )(proj, proj, proj, proj, a_pad, wg, b_gate2.reshape(1, -1).astype(F32), out_gain.reshape(1, dv).astype(F32), tri)


def _nsa_compress_body(u_ref, w1_ref, w2_ref, pe_ref, kg_ref, o_ref):
    kv = pl.program_id(0)
    u = u_ref[0, 0]
    half = u.shape[1]
    w1 = w1_ref[0]
    a = jnp.dot(u, w1[:half], preferred_element_type=F32)
    b = jnp.dot(u, w1[half:], preferred_element_type=F32)
    pe_term = jnp.dot(pe_ref[0], w1, preferred_element_type=F32)[0:1]
    n_chunks = u.shape[0]
    hid = jax.nn.gelu(a + pltpu.roll(b, n_chunks - 1, 0) + pe_term, approximate=True)
    out = jnp.dot(hid.astype(BF16), w2_ref[0], preferred_element_type=F32)
    ms = jnp.mean(out * out, axis=-1, keepdims=True)
    normed = out * lax.rsqrt(ms + NORM_EPS) * kg_ref[...]
    res = jnp.where(kv == 0, normed, out)
    row = lax.broadcasted_iota(I32, res.shape, 0)
    o_ref[0, 0, 0] = jnp.where(row < n_chunks - 1, res, 0.0).astype(o_ref.dtype)


def nsa_compress(zz, cmp_pe, cmp_w1, cmp_w2, kc_gain, B, T):
    G = NSA_GROUPS
    Dh = zz.shape[-1]
    n_chunks = T // CMP_STRIDE
    u = zz.reshape(zz.shape[0], B, n_chunks, CMP_STRIDE * Dh)
    w1 = cmp_w1.reshape(2, CMP_BLOCK * Dh, Dh).astype(BF16)
    pe = jnp.broadcast_to(cmp_pe.reshape(2, 1, CMP_BLOCK * Dh), (2, 8, CMP_BLOCK * Dh)).astype(BF16)
    return pl.pallas_call(
        _nsa_compress_body,
        grid=(2, B, G),
        in_specs=[pl.BlockSpec((1, 1, n_chunks, CMP_STRIDE * Dh), lambda kv, b, g: (kv * G + g, b, 0, 0)),
                  pl.BlockSpec((1, CMP_BLOCK * Dh, Dh), lambda kv, b, g: (kv, 0, 0)),
                  pl.BlockSpec((1, Dh, Dh), lambda kv, b, g: (kv, 0, 0)),
                  pl.BlockSpec((1, 8, CMP_BLOCK * Dh), lambda kv, b, g: (kv, 0, 0)),
                  pl.BlockSpec((1, Dh), lambda kv, b, g: (0, 0))],
        out_specs=pl.BlockSpec((1, 1, 1, n_chunks, Dh), lambda kv, b, g: (kv, b, g, 0, 0)),
        out_shape=jax.ShapeDtypeStruct((2, B, G, n_chunks, Dh), BF16),
        compiler_params=_params("arbitrary", "arbitrary", "arbitrary"),
        name="nsa_compress",
    )(u, w1, cmp_w2.astype(BF16), pe, kc_gain.reshape(1, Dh).astype(F32))


def _dot_nt(a, b):
    return lax.dot_general(a, b, (((1,), (1,)), ((), ())), preferred_element_type=F32)


def _col_softmax_terms(s):
    m = jnp.max(s, axis=0, keepdims=True)
    m = jnp.where(m == NEG_INF, 0.0, m)
    e = jnp.exp2(s - m)
    return e, 1.0 / jnp.maximum(jnp.sum(e, axis=0, keepdims=True), 1e-30)


def _tile_lanes(x, n):
    return jnp.concatenate([x] * n, axis=1)


def _nsa_attn_body(q_ref, qaug_ref, glog_ref, kc_ref, vct_ref, ks_ref, vst_ref, kw_ref, vwt_ref,
                   mselt_ref, efullt_ref, o_ref, q2_ref, acct_ref, m_ref, l_ref, oacct_ref, s_ref, sc_ref, sw_ref,
                   sel_ref):
    qb = pl.program_id(2)
    Dh = LANES
    RB = NSA_ROW_BLOCK
    hpb = RB // Q_BLOCK
    n_rb = NSA_HG // hpb
    n_cmp_pad = kc_ref.shape[2]
    n_slc = mselt_ref.shape[0]
    win_keys = WINDOW + Q_BLOCK
    s0 = qb * Q_BLOCK
    t_q = s0 + lax.broadcasted_iota(I32, (1, Q_BLOCK), 1)

    gates_t = jax.nn.sigmoid(glog_ref[0]).T

    for hg in range(NSA_HG):
        q2_ref[hg * Q_BLOCK:(hg + 1) * Q_BLOCK, :Dh] = q_ref[0, :, hg * Dh:(hg + 1) * Dh]
    q2_ref[:, Dh:] = qaug_ref[0]

    cmp_end = lax.broadcasted_iota(I32, (n_cmp_pad, 1), 0) * CMP_STRIDE + (CMP_BLOCK - 1)
    cmp_mask = _tile_lanes(jnp.where(cmp_end <= t_q, 0.0, NEG_INF), hpb)

    ws = pl.multiple_of(jnp.maximum(s0 - WINDOW, 0), Q_BLOCK)
    wdist = t_q - (ws + lax.broadcasted_iota(I32, (win_keys, 1), 0))
    win_mask = _tile_lanes(jnp.where((wdist >= 0) & (wdist < WINDOW), 0.0, NEG_INF), hpb)

    kc = kc_ref[0, 0]
    vct = jnp.concatenate([vct_ref[0, 0, j] for j in range(n_cmp_pad // Dh)], axis=1)
    kwin = kw_ref[0, 0, pl.ds(ws, win_keys), :]
    wblk = ws // Dh
    vwt = jnp.concatenate([vwt_ref[0, 0, wblk + j] for j in range(win_keys // Dh)], axis=1)

    imp_t = jnp.zeros((n_cmp_pad, Q_BLOCK), F32)
    for rb in range(n_rb):
        q_rb = q2_ref[rb * RB:(rb + 1) * RB, :]
        sc_ref[rb] = _dot_nt(kc, q_rb) + cmp_mask
        sw_ref[rb] = _dot_nt(kwin, q_rb) + win_mask
    for rb in range(n_rb):
        e, inv = _col_softmax_terms(sc_ref[rb])
        p = e * inv
        for j in range(hpb):
            imp_t = imp_t + p[:, j * Q_BLOCK:(j + 1) * Q_BLOCK]
        o_c = jnp.dot(vct, p.astype(BF16), preferred_element_type=F32)
        e, inv = _col_softmax_terms(sw_ref[rb])
        o_w = jnp.dot(vwt, e.astype(BF16), preferred_element_type=F32) * inv
        for j in range(hpb):
            hg = rb * hpb + j
            sub = slice(j * Q_BLOCK, (j + 1) * Q_BLOCK)
            acct_ref[:, hg * Q_BLOCK:(hg + 1) * Q_BLOCK] = (
                gates_t[hg:hg + 1, :] * o_c[:, sub] + gates_t[2 * NSA_HG + hg:2 * NSA_HG + hg + 1, :] * o_w[:, sub])

    hi, mid, lo = _split3(imp_t)
    mselt = mselt_ref[...]
    slc_imp = (jnp.dot(mselt, hi, preferred_element_type=F32) + jnp.dot(mselt, mid, preferred_element_type=F32)
               + jnp.dot(mselt, lo, preferred_element_type=F32))

    blk = lax.broadcasted_iota(I32, (n_slc, 1), 0)
    cur = lax.shift_right_logical(t_q, 6)
    forced = (blk == 0) | (blk == cur) | (blk == cur - 1)
    score = jnp.where(forced, jnp.inf, jnp.where(blk <= cur, slc_imp, NEG_INF))
    sel = jnp.zeros((n_slc, Q_BLOCK), F32)
    for _ in range(SLC_TOPK):
        m = jnp.max(score, axis=0, keepdims=True)
        cand = (score == m) & (m > NEG_INF)
        first = jnp.min(jnp.where(cand, blk, n_slc), axis=0, keepdims=True)
        one = blk == first
        sel = jnp.where(one, 1.0, sel)
        score = jnp.where(one, NEG_INF, score)
    sel_b = sel.astype(BF16)
    sel_ref[...] = sel

    m_ref[...] = jnp.full(m_ref.shape, NEG_INF, F32)
    l_ref[...] = jnp.zeros(l_ref.shape, F32)
    oacct_ref[...] = jnp.zeros(oacct_ref.shape, F32)
    KC = SLC_KEY_CHUNK
    key_iota = lax.broadcasted_iota(I32, (KC, 1), 0)

    def chunk_step(c, carry):
        k0 = pl.multiple_of(c * KC, KC)
        bpc = KC // SLC_BLOCK
        picked = sel_ref[pl.ds(pl.multiple_of(c * bpc, bpc), bpc), :]
        any_picked = jnp.max(jnp.max(picked, axis=0, keepdims=True), axis=1, keepdims=True)[0, 0]

        @pl.when(any_picked > 0.5)
        def _():
            kch = ks_ref[0, 0, pl.ds(k0, KC), :]
            vt = jnp.concatenate([vst_ref[0, 0, c * (KC // Dh) + j] for j in range(KC // Dh)], axis=1)
            selk = jnp.dot(efullt_ref[pl.ds(k0, KC), :], sel_b, preferred_element_type=F32)
            mask = _tile_lanes(jnp.where((selk > 0.5) & (k0 + key_iota <= t_q), 0.0, NEG_INF), hpb)
            for rb in range(n_rb):
                s_ref[rb] = _dot_nt(kch, q2_ref[rb * RB:(rb + 1) * RB, :]) + mask
            for rb in range(n_rb):
                cols = slice(rb * RB, (rb + 1) * RB)
                s = s_ref[rb]
                m_old = m_ref[rb]
                m_new = jnp.maximum(m_old, jnp.max(s, axis=0, keepdims=True))
                m_safe = jnp.where(m_new == NEG_INF, 0.0, m_new)
                alpha = jnp.exp2(m_old - m_safe)
                p = jnp.exp2(s - m_safe)
                l_ref[rb] = alpha * l_ref[rb] + jnp.sum(p, axis=0, keepdims=True)
                oacct_ref[:, cols] = (alpha * oacct_ref[:, cols]
                                      + jnp.dot(vt, p.astype(BF16), preferred_element_type=F32))
                m_ref[rb] = m_new
        return carry

    n_chunks = (s0 + Q_BLOCK + KC - 1) // KC
    lax.fori_loop(0, n_chunks, chunk_step, 0)
    for hg in range(NSA_HG):
        rb, j = divmod(hg, hpb)
        cols = slice(hg * Q_BLOCK, (hg + 1) * Q_BLOCK)
        inv = 1.0 / jnp.maximum(l_ref[rb][:, j * Q_BLOCK:(j + 1) * Q_BLOCK], 1e-30)
        o_t = acct_ref[:, cols] + gates_t[NSA_HG + hg:NSA_HG + hg + 1, :] * (oacct_ref[:, cols] * inv)
        o_ref[0, :, hg * Dh:(hg + 1) * Dh] = o_t.T.astype(o_ref.dtype)


def _pos_pieces(pos):
    pos = np.asarray(pos)
    out = np.zeros((pos.shape[0], LANES), np.float32)
    for i in range(3):
        out[:, 2 * i] = 64 * (pos // 64)
        out[:, 2 * i + 1] = pos % 64
    return jnp.asarray(out, BF16)


def _slope_pieces():
    H, G, HG = NSA_HEADS, NSA_GROUPS, NSA_HG
    slopes = jnp.asarray(LOG2E * 2.0 ** (-8.0 * np.arange(1, H + 1, dtype=np.float64) / H), F32)
    pieces = jnp.stack(_split3(slopes), axis=-1)
    cols = jnp.repeat(pieces, 2, axis=-1)
    cols = jnp.pad(cols, ((0, 0), (0, LANES - cols.shape[-1])))
    return jnp.repeat(cols.reshape(G, HG, 1, LANES), Q_BLOCK, axis=2).reshape(G, HG * Q_BLOCK, LANES)


def _blocked_transpose(v):
    lead, (t, dh) = v.shape[:-2], v.shape[-2:]
    return jnp.swapaxes(v.reshape(*lead, t // LANES, LANES, dh), -1, -2)


def nsa_attention(q, glog, kvc, zz, B, T):
    G, HG, Dh = NSA_GROUPS, NSA_HG, LANES
    H = NSA_HEADS
    n_cmp = T // CMP_STRIDE - CMP_BLOCK // CMP_STRIDE + 1
    n_cmp_pad = -(-T // CMP_STRIDE // LANES) * LANES
    n_slc = T // SLC_BLOCK
    per = SLC_BLOCK // CMP_STRIDE
    n = np.arange(n_cmp_pad)[None, :]
    j = np.arange(n_slc)[:, None]
    mselt = ((n // per == j).astype(np.float32) + ((n + 1) // per == j).astype(np.float32))
    mselt[:, n_cmp:] = 0.0
    kvc = jnp.pad(kvc, ((0, 0), (0, 0), (0, 0), (0, n_cmp_pad - kvc.shape[3]), (0, 0)))
    efullt = (np.arange(T)[:, None] // SLC_BLOCK == np.arange(n_slc)[None, :]).astype(np.float32)

    key_aug = jnp.broadcast_to(_pos_pieces(np.arange(T)), (G, B, T, LANES))
    cmp_aug = jnp.broadcast_to(_pos_pieces(np.arange(n_cmp_pad) * CMP_STRIDE + CMP_BLOCK - 1),
                               (B, G, n_cmp_pad, LANES))
    kc_aug = jnp.concatenate([kvc[0], cmp_aug], axis=-1)
    vct = _blocked_transpose(kvc[1])
    ks_aug = jnp.concatenate([zz[2 * G:3 * G], key_aug], axis=-1)
    kw_aug = jnp.concatenate([zz[4 * G:5 * G], key_aug], axis=-1)
    vst = _blocked_transpose(zz[3 * G:4 * G])
    vwt = _blocked_transpose(zz[5 * G:6 * G])

    def k_spec():
        return pl.BlockSpec((1, 1, T, 2 * Dh), lambda b, g, i: (g, b, 0, 0))

    def vt_spec():
        return pl.BlockSpec((1, 1, T // LANES, Dh, LANES), lambda b, g, i: (g, b, 0, 0, 0))

    n_rb = HG * Q_BLOCK // NSA_ROW_BLOCK
    return pl.pallas_call(
        _nsa_attn_body,
        grid=(B, G, T // Q_BLOCK),
        in_specs=[pl.BlockSpec((1, Q_BLOCK, HG * Dh), lambda b, g, i: (b, i, g)),
                  pl.BlockSpec((1, HG * Q_BLOCK, LANES), lambda b, g, i: (g, 0, 0)),
                  pl.BlockSpec((1, Q_BLOCK, LANES), lambda b, g, i: (b, i, g)),
                  pl.BlockSpec((1, 1, n_cmp_pad, 2 * Dh), lambda b, g, i: (b, g, 0, 0)),
                  pl.BlockSpec((1, 1, n_cmp_pad // LANES, Dh, LANES), lambda b, g, i: (b, g, 0, 0, 0)),
                  k_spec(), vt_spec(), k_spec(), vt_spec(),
                  pl.BlockSpec((n_slc, n_cmp_pad), lambda b, g, i: (0, 0)),
                  pl.BlockSpec((T, n_slc), lambda b, g, i: (0, 0))],
        out_specs=pl.BlockSpec((1, Q_BLOCK, HG * Dh), lambda b, g, i: (b, i, g)),
        out_shape=jax.ShapeDtypeStruct((B, T, H * Dh), BF16),
        scratch_shapes=[pltpu.VMEM((HG * Q_BLOCK, 2 * Dh), BF16),
                        pltpu.VMEM((Dh, HG * Q_BLOCK), F32),
                        pltpu.VMEM((n_rb, 1, NSA_ROW_BLOCK), F32),
                        pltpu.VMEM((n_rb, 1, NSA_ROW_BLOCK), F32),
                        pltpu.VMEM((Dh, HG * Q_BLOCK), F32),
                        pltpu.VMEM((n_rb, SLC_KEY_CHUNK, NSA_ROW_BLOCK), F32),
                        pltpu.VMEM((n_rb, n_cmp_pad, NSA_ROW_BLOCK), F32),
                        pltpu.VMEM((n_rb, WINDOW + Q_BLOCK, NSA_ROW_BLOCK), F32),
                        pltpu.VMEM((n_slc, Q_BLOCK), F32)],
        compiler_params=_params("arbitrary", "arbitrary", "arbitrary"),
        name="nsa_attention",
    )(q, _slope_pieces(), glog, kc_aug, vct, ks_aug, vst, kw_aug, vwt,
      jnp.asarray(mselt, BF16), jnp.asarray(efullt, BF16))


def _pack_rows(y, o_ref):
    m, d = y.shape
    bits = lax.bitcast_convert_type(y.astype(BF16).astype(F32), U32)
    for c in range(d // (2 * LANES)):
        lo = lax.shift_right_logical(bits[:, c * LANES:(c + 1) * LANES], jnp.uint32(16))
        hi = bits[:, d // 2 + c * LANES:d // 2 + (c + 1) * LANES]
        o_ref[pl.ds(c, m, stride=d // (2 * LANES)), :] = lo | hi


def _unpack_chunk(words):
    lo = lax.bitcast_convert_type(lax.shift_left(words, jnp.uint32(16)), F32)
    hi = lax.bitcast_convert_type(words & jnp.uint32(0xFFFF0000), F32)
    return lo, hi


def _moe_router_body(h_ref, g_ref, w_ref, b_ref, tri_ref, xp_ref, idx_ref, wgt_ref, rank_ref, cnt_ref):
    @pl.when(pl.program_id(0) == 0)
    def _():
        cnt_ref[...] = jnp.zeros(cnt_ref.shape, F32)

    x = h_ref[...]
    ms = jnp.mean(x * x, axis=-1, keepdims=True)
    xn = x * lax.rsqrt(ms + NORM_EPS) * g_ref[...]
    _pack_rows(xn, xp_ref)
    logits = jnp.dot(xn.astype(BF16), w_ref[...], preferred_element_type=F32) + b_ref[...]
    lane = lax.broadcasted_iota(I32, logits.shape, 1)
    logits = jnp.where(lane < N_EXPERTS, logits, NEG_INF)
    idx_out = jnp.zeros(logits.shape, I32)
    val_out = jnp.full(logits.shape, NEG_INF, F32)
    picked = []
    for k in range(TOP_K):
        m = jnp.max(logits, axis=-1, keepdims=True)
        first = jnp.min(jnp.where(logits == m, lane, LANES), axis=-1, keepdims=True)
        idx_out = jnp.where(lane == k, first, idx_out)
        val_out = jnp.where(lane == k, m, val_out)
        picked.append(jnp.where(lane == first, 1.0, 0.0))
        logits = jnp.where(lane == first, NEG_INF, logits)
    e = jnp.exp(val_out - jnp.max(val_out, axis=-1, keepdims=True))
    idx_ref[...] = idx_out
    wgt_ref[...] = e * (1.0 / jnp.sum(e, axis=-1, keepdims=True))

    base = cnt_ref[0:1, :]
    rank_out = jnp.zeros(logits.shape, F32)
    for k in range(TOP_K):
        earlier = jnp.dot(tri_ref[...], picked[k].astype(BF16), preferred_element_type=F32)
        r = jnp.sum(picked[k] * (base + earlier), axis=-1, keepdims=True)
        rank_out = jnp.where(lane == k, r, rank_out)
        base = base + jnp.sum(picked[k], axis=0, keepdims=True)
    rank_ref[...] = rank_out.astype(I32)
    cnt_ref[...] = jnp.broadcast_to(base, cnt_ref.shape)


def moe_router(h, gain, w_router, b_router, tm=256):
    n, d = h.shape
    tm = min(tm, n)
    pr = d // (2 * LANES)
    w = _pad_cols(w_router).astype(BF16)
    b = _pad_cols(b_router.reshape(1, -1)).astype(F32)
    tri = jnp.asarray(np.tril(np.ones((tm, tm), np.float32), -1), BF16)
    return pl.pallas_call(
        _moe_router_body,
        grid=(n // tm,),
        in_specs=[pl.BlockSpec((tm, d), lambda i: (i, 0)),
                  pl.BlockSpec((1, d), lambda i: (0, 0)),
                  pl.BlockSpec((d, LANES), lambda i: (0, 0)),
                  pl.BlockSpec((1, LANES), lambda i: (0, 0)),
                  pl.BlockSpec((tm, tm), lambda i: (0, 0))],
        out_specs=[pl.BlockSpec((tm * pr, LANES), lambda i: (i, 0)),
                   pl.BlockSpec((tm, LANES), lambda i: (i, 0)),
                   pl.BlockSpec((tm, LANES), lambda i: (i, 0)),
                   pl.BlockSpec((tm, LANES), lambda i: (i, 0)),
                   pl.BlockSpec((8, LANES), lambda i: (0, 0))],
        out_shape=[jax.ShapeDtypeStruct((n * pr, LANES), U32),
                   jax.ShapeDtypeStruct((n, LANES), I32), jax.ShapeDtypeStruct((n, LANES), F32),
                   jax.ShapeDtypeStruct((n, LANES), I32), jax.ShapeDtypeStruct((8, LANES), F32)],
        compiler_params=_params("arbitrary"),
        name="moe_router",
    )(h, gain.reshape(1, d).astype(F32), w, b, tri)


def _row_gather_copy(src_hbm, src_row, dst_buf, slot, r, sem, pr):
    return pltpu.make_async_copy(src_hbm.at[pl.ds(pl.multiple_of(src_row, pr), pr)],
                                 dst_buf.at[slot, pl.ds(pl.multiple_of(r * pr, pr), pr)], sem.at[slot])


def _start_row_gather(ids_ref, src_hbm, dst_buf, slot, sem, n_rows, pr):
    def body(r, carry):
        _row_gather_copy(src_hbm, ids_ref[0, 0, r], dst_buf, slot, r, sem, pr).start()
        return carry
    lax.fori_loop(0, n_rows, body, 0, unroll=8)


def _wait_row_gather(src_hbm, dst_buf, slot, sem):
    pltpu.make_async_copy(src_hbm.at[pl.ds(0, dst_buf.shape[1])], dst_buf.at[slot], sem.at[slot]).wait()


def _moe_expert_body(blk_e_ref, n_used_ref, ids0_ref, idsn_ref, x_hbm, wgu_ref, bgu_ref, wd_ref, bd_ref,
                     y_ref, xbuf, xs, sem):
    i = pl.program_id(0)
    n_used = n_used_ref[0]
    rows, d = xs.shape
    pr = d // (2 * LANES)
    ff = wd_ref.shape[2]

    @pl.when(i == 0)
    def _():
        _start_row_gather(ids0_ref, x_hbm, xbuf, 0, sem, rows, pr)

    @pl.when(i + 1 < n_used)
    def _():
        _start_row_gather(idsn_ref, x_hbm, xbuf, (i + 1) % 2, sem, rows, pr)

    @pl.when(i < n_used)
    def _():
        slot = i % 2
        _wait_row_gather(x_hbm, xbuf, slot, sem)
        sub = rows // MOE_SUB_BLOCKS
        for sb in range(MOE_SUB_BLOCKS):
            r0 = sb * sub
            for c in range(pr):
                lo, hi = _unpack_chunk(xbuf[slot, pl.ds(r0 * pr + c, sub, stride=pr), :])
                xs[r0:r0 + sub, c * LANES:(c + 1) * LANES] = lo.astype(BF16)
                xs[r0:r0 + sub, d // 2 + c * LANES:d // 2 + (c + 1) * LANES] = hi.astype(BF16)
            gu = jnp.dot(xs[r0:r0 + sub, :], wgu_ref[0, 0], preferred_element_type=F32) + bgu_ref[0]
            gate = jnp.minimum(gu[:, :ff], SWIGLU_LIMIT)
            up = jnp.clip(gu[:, ff:], -SWIGLU_LIMIT, SWIGLU_LIMIT)
            act = (up + 1.0) * (gate * jax.nn.sigmoid(SWIGLU_ALPHA * gate))
            y = jnp.dot(act.astype(BF16), wd_ref[0, 0], preferred_element_type=F32) + bd_ref[0]
            _pack_rows(y, y_ref.at[pl.ds(r0 * pr, sub * pr), :])

    @pl.when(i >= n_used)
    def _():
        y_ref[...] = jnp.zeros(y_ref.shape, y_ref.dtype)


def moe_experts(xp, tok_rows, blk_e, n_used, w_gate_up, b_gate_up, w_down, b_down, l):
    _, E, d, ff2 = w_gate_up.shape
    ff = ff2 // 2
    pr = d // (2 * LANES)
    rows = MOE_BLOCK_ROWS
    n_blocks = tok_rows.shape[0] // rows
    ids = tok_rows.reshape(n_blocks, 1, rows)

    def used(i, nu):
        return jnp.minimum(i, nu[0] - 1)

    grid_spec = pltpu.PrefetchScalarGridSpec(
        num_scalar_prefetch=2,
        grid=(n_blocks,),
        in_specs=[pl.BlockSpec((1, 1, rows), lambda i, be, nu: (0, 0, 0), memory_space=pltpu.SMEM),
                  pl.BlockSpec((1, 1, rows), lambda i, be, nu: (jnp.minimum(i + 1, n_blocks - 1), 0, 0),
                               memory_space=pltpu.SMEM),
                  pl.BlockSpec(memory_space=pl.ANY),
                  pl.BlockSpec((1, 1, d, ff2), lambda i, be, nu: (l, be[used(i, nu)], 0, 0)),
                  pl.BlockSpec((1, 1, ff2), lambda i, be, nu: (be[used(i, nu)], 0, 0)),
                  pl.BlockSpec((1, 1, ff, d), lambda i, be, nu: (l, be[used(i, nu)], 0, 0)),
                  pl.BlockSpec((1, 1, d), lambda i, be, nu: (be[used(i, nu)], 0, 0))],
        out_specs=pl.BlockSpec((rows * pr, LANES), lambda i, be, nu: (i, 0)),
        scratch_shapes=[pltpu.VMEM((2, rows * pr, LANES), U32), pltpu.VMEM((rows, d), BF16),
                        pltpu.SemaphoreType.DMA((2,))],
    )
    return pl.pallas_call(
        _moe_expert_body,
        grid_spec=grid_spec,
        out_shape=jax.ShapeDtypeStruct((n_blocks * rows * pr, LANES), U32),
        compiler_params=_params("arbitrary"),
        name="moe_experts",
    )(blk_e, n_used, ids, ids, xp, w_gate_up, b_gate_up.reshape(E, 1, ff2).astype(F32),
      w_down, b_down.reshape(E, 1, d).astype(F32))


def _moe_combine_body(ids0_ref, idsn_ref, y_hbm, h_ref, w_ref, g_ref, o_ref, *rest):
    xn_refs, (ybuf, sem) = rest[:-2], rest[-2:]
    i = pl.program_id(0)
    n_steps = pl.num_programs(0)
    tt, d = h_ref.shape
    pr = d // (2 * LANES)
    rows = TOP_K * tt

    @pl.when(i == 0)
    def _():
        _start_row_gather(ids0_ref, y_hbm, ybuf, 0, sem, rows, pr)

    @pl.when(i + 1 < n_steps)
    def _():
        _start_row_gather(idsn_ref, y_hbm, ybuf, (i + 1) % 2, sem, rows, pr)

    slot = i % 2
    _wait_row_gather(y_hbm, ybuf, slot, sem)
    w = w_ref[...]
    wk = [jnp.broadcast_to(w[:, k:k + 1], (tt, LANES)) for k in range(TOP_K)]
    ssq = jnp.zeros((tt, LANES), F32)
    for c in range(pr):
        lo_cols = slice(c * LANES, (c + 1) * LANES)
        hi_cols = slice(d // 2 + c * LANES, d // 2 + (c + 1) * LANES)
        acc_lo = h_ref[:, lo_cols]
        acc_hi = h_ref[:, hi_cols]
        for k in range(TOP_K):
            lo, hi = _unpack_chunk(ybuf[slot, pl.ds(k * tt * pr + c, tt, stride=pr), :])
            acc_lo = acc_lo + wk[k] * lo
            acc_hi = acc_hi + wk[k] * hi
        o_ref[:, lo_cols] = acc_lo
        o_ref[:, hi_cols] = acc_hi
        ssq = ssq + acc_lo * acc_lo + acc_hi * acc_hi
    if xn_refs:
        inv = lax.rsqrt(jnp.sum(ssq, axis=-1, keepdims=True) * (1.0 / d) + NORM_EPS)
        for j, xn_ref in enumerate(xn_refs):
            xn_ref[...] = (o_ref[...] * inv * g_ref[j:j + 1, :]).astype(xn_ref.dtype)


def moe_combine(yp, slot_rows, weights, h, next_gains):
    n, d = h.shape
    n_norm = len(next_gains)
    gains = (jnp.stack(next_gains) if next_gains else jnp.zeros((1, d))).astype(F32)
    pr = d // (2 * LANES)
    tt = min(MOE_COMBINE_TOKENS, n)
    n_steps = n // tt
    ids = slot_rows.reshape(n_steps, tt, TOP_K).transpose(0, 2, 1).reshape(n_steps, 1, TOP_K * tt)
    return pl.pallas_call(
        _moe_combine_body,
        grid=(n_steps,),
        in_specs=[pl.BlockSpec((1, 1, TOP_K * tt), lambda i: (0, 0, 0), memory_space=pltpu.SMEM),
                  pl.BlockSpec((1, 1, TOP_K * tt), lambda i: (jnp.minimum(i + 1, n_steps - 1), 0, 0),
                               memory_space=pltpu.SMEM),
                  pl.BlockSpec(memory_space=pl.ANY),
                  pl.BlockSpec((tt, d), lambda i: (i, 0)),
                  pl.BlockSpec((tt, LANES), lambda i: (i, 0)),
                  pl.BlockSpec(gains.shape, lambda i: (0, 0))],
        out_specs=[pl.BlockSpec((tt, d), lambda i: (i, 0))] * (1 + n_norm),
        out_shape=[jax.ShapeDtypeStruct((n, d), F32)] + [jax.ShapeDtypeStruct((n, d), BF16)] * n_norm,
        scratch_shapes=[pltpu.VMEM((2, TOP_K * tt * pr, LANES), U32), pltpu.SemaphoreType.DMA((2,))],
        compiler_params=_params("arbitrary"),
        name="moe_combine",
    )(ids, ids, yp, h, weights, gains)


def _moe_plan(top_idx, rank, sizes, n_rows):
    E, rows = N_EXPERTS, MOE_BLOCK_ROWS
    flat_e = top_idx.reshape(-1)
    rank = rank.reshape(-1)
    nk = flat_e.shape[0]
    nblk = (sizes + rows - 1) // rows
    blk_end = jnp.cumsum(nblk)
    pad_start = (blk_end - nblk) * rows
    slot_dest = pad_start[flat_e] + rank
    tok_buf = jnp.zeros((n_rows,), I32).at[slot_dest].set(jnp.arange(nk, dtype=I32) // TOP_K)
    n_blocks = n_rows // rows
    blk_e = jnp.sum((blk_end[None, :] <= jnp.arange(n_blocks, dtype=I32)[:, None]).astype(I32), axis=1)
    blk_e = jnp.minimum(blk_e, E - 1)
    n_used = blk_end[-1:].astype(I32)
    return slot_dest, tok_buf, blk_e, n_used


def moe_layer(h, gain, w_router, b_router, w_gate_up, b_gate_up, w_down, b_down, l, next_gains):
    n, d = h.shape
    pr = d // (2 * LANES)
    rows = MOE_BLOCK_ROWS
    xp, top_idx, weights, rank, counts = moe_router(h, gain, w_router, b_router)
    n_rows = n * TOP_K + N_EXPERTS * rows
    slot_dest, tok_buf, blk_e, n_used = _moe_plan(top_idx[:, :TOP_K], rank[:, :TOP_K],
                                                  counts[0, :N_EXPERTS].astype(I32), n_rows)
    yp = moe_experts(xp, tok_buf * pr, blk_e, n_used, w_gate_up, b_gate_up, w_down, b_down, l)
    return moe_combine(yp, slot_dest.reshape(n, TOP_K) * pr, weights, h, next_gains)


def _gla_mixer(h, xn, w_in, w_gate2, b_gate2, out_gain, w_out, l, B, T):
    n_main = w_in.shape[2] - GLA_GATE_RANK
    w_in_t = jnp.swapaxes(w_in, 1, 2)
    proj = matmul(xn, w_in_t, layer=l, n=n_main, w_transposed=True, out_dtype=BF16)
    wa_t = jnp.pad(w_in_t[l, n_main:], ((0, LANES - GLA_GATE_RANK), (0, 0)))
    a_pad = matmul(xn, wa_t[None], w_transposed=True)
    o = gla_core(proj, a_pad, w_gate2, b_gate2, out_gain, B, T)
    return matmul(o, w_out, layer=l, residual=h)


def _nsa_shared_kv(xn, w_kv, k_gain, cmp_pe, cmp_w1, cmp_w2, B, T):
    G = NSA_GROUPS
    Dh = LANES
    ones = jnp.ones((G * Dh,), F32)
    gain_cols = jnp.concatenate([ones, ones, jnp.tile(k_gain[1], G), ones, jnp.tile(k_gain[2], G), ones])
    norm_groups = [False] * (2 * G) + [True] * G + [False] * G + [True] * G + [False] * G
    zz = matmul_groupnorm(xn, w_kv.astype(BF16)[None], gain_cols, norm_groups, split_out=True, tn=6 * G * Dh)
    kvc = nsa_compress(zz, cmp_pe, cmp_w1, cmp_w2, k_gain[0], B, T)
    return kvc, zz.reshape(6 * G, B, T, Dh)


def _nsa_mixer(h, xn, w_in, q_gain, w_out, kvc, zz, j, B, T):
    H, G, HG, Dh = NSA_HEADS, NSA_GROUPS, NSA_HG, LANES
    w_in_t = jnp.swapaxes(w_in, 1, 2)
    q = matmul_groupnorm(xn, w_in_t, jnp.tile(q_gain, H), [True] * 4, layer=j, n=H * Dh, w_transposed=True,
                         scale=Dh ** -0.5 * LOG2E, tm=1024, tn=512)
    wg = w_in_t[j, H * Dh:].reshape(G, HG, 3, -1).transpose(0, 2, 1, 3).reshape(G, 3 * HG, -1)
    wg = jnp.pad(wg, ((0, 0), (0, LANES - 3 * HG), (0, 0))).reshape(1, G * LANES, -1)
    glog = matmul(xn, wg, w_transposed=True)
    o = nsa_attention(q.reshape(B, T, H * Dh), glog.reshape(B, T, G * LANES), kvc, zz, B, T)
    return matmul(o.reshape(B * T, H * Dh), w_out, layer=j, residual=h)


def kernel(x, ln_mix, ln_ffn, a_w_in, a_w_gate2, a_b_gate2, a_out_gain, a_w_out, kv_gain, w_kv, k_gain,
           cmp_pe, cmp_w1, cmp_w2, b_w_in, b_q_gain, b_w_out, w_router, b_router, w_gate_up, b_gate_up,
           w_down, b_down):
    B, T, D = x.shape
    depth = ln_mix.shape[0]
    n_a = a_w_in.shape[0]
    h = x.reshape(B * T, D)
    shared = None
    wgu_bf = w_gate_up.astype(BF16)
    wd_bf = w_down.astype(BF16)
    xn = rmsnorm(h, ln_mix[0])
    xn_kv = rmsnorm(h, kv_gain) if n_a == 0 else None
    for l in range(depth):
        if l < n_a:
            h = _gla_mixer(h, xn, a_w_in, a_w_gate2[l], a_b_gate2[l], a_out_gain[l], a_w_out, l, B, T)
        else:
            if l == n_a:
                shared = _nsa_shared_kv(xn_kv, w_kv, k_gain, cmp_pe, cmp_w1, cmp_w2, B, T)
            j = l - n_a
            h = _nsa_mixer(h, xn, b_w_in, b_q_gain[j], b_w_out, *shared, j, B, T)
        next_gains = [ln_mix[l + 1]] if l + 1 < depth else []
        if l + 1 == n_a:
            next_gains.append(kv_gain)
        h, *normed = moe_layer(h, ln_ffn[l], w_router[l], b_router[l], wgu_bf, b_gate_up[l], wd_bf, b_down[l], l,
                               next_gains)
        if normed:
            xn = normed[0]
        if l + 1 == n_a:
            xn_kv = normed[1]
    return h.reshape(B, T, D)
```

```python
import functools
import math

import jax
import jax.numpy as jnp
from jax import lax
import numpy as np
from jax.experimental import pallas as pl
from jax.experimental.pallas import tpu as pltpu

F32 = jnp.float32
BF16 = jnp.bfloat16
I32 = jnp.int32
U32 = jnp.uint32

NORM_EPS = 1e-5
GLA_HEADS = 8
GLA_GATE_RANK = 16
GLA_GATE_TAU = 16.0
NSA_HEADS = 32
NSA_GROUPS = 2
NSA_HG = NSA_HEADS // NSA_GROUPS
CMP_BLOCK = 32
CMP_STRIDE = 16
SLC_BLOCK = 64
SLC_TOPK = 8
WINDOW = 512
Q_BLOCK = 128
N_EXPERTS = 32
TOP_K = 4
SWIGLU_ALPHA = 1.702
SWIGLU_LIMIT = 7.0

LANES = 128
MXU_DIM = 256
VMEM_LIMIT_BYTES = 56 * 1024 * 1024
MOE_BLOCK_ROWS = 512
MOE_SUB_BLOCKS = 4
MOE_COMBINE_TOKENS = 128
SLC_KEY_CHUNK = 512
NSA_ROW_BLOCK = 256
GLA_BLOCK = 128
GLA_STEP_TOKENS = 512
GLA_HEADS_PER_STEP = 2
NEG_INF = float("-inf")
LOG2E = math.log2(math.e)


def _params(*sem):
    return pltpu.CompilerParams(dimension_semantics=sem, vmem_limit_bytes=VMEM_LIMIT_BYTES)


def _split3(x):
    hi = x.astype(BF16)
    r1 = x - hi.astype(F32)
    mid = r1.astype(BF16)
    lo = (r1 - mid.astype(F32)).astype(BF16)
    return hi, mid, lo


def _rmsnorm_body(x_ref, g_ref, o_ref):
    x = x_ref[...].astype(F32)
    ms = jnp.mean(x * x, axis=-1, keepdims=True)
    o_ref[...] = (x * lax.rsqrt(ms + NORM_EPS) * g_ref[...].astype(F32)).astype(o_ref.dtype)


def rmsnorm(x, gain, out_dtype=BF16, tm=512):
    m, d = x.shape
    tm = min(tm, m)
    return pl.pallas_call(
        _rmsnorm_body,
        grid=(m // tm,),
        in_specs=[pl.BlockSpec((tm, d), lambda i: (i, 0)),
                  pl.BlockSpec((1, d), lambda i: (0, 0))],
        out_specs=pl.BlockSpec((tm, d), lambda i: (i, 0)),
        out_shape=jax.ShapeDtypeStruct((m, d), out_dtype),
        compiler_params=_params("arbitrary"),
        name="rmsnorm",
    )(x, gain.reshape(1, d))


def _tile_dot(a_ref, w_ref, w_transposed):
    w = w_ref[0].astype(BF16)
    dims = (((1,), (1,)), ((), ())) if w_transposed else (((1,), (0,)), ((), ()))
    return lax.dot_general(a_ref[...], w, dims, preferred_element_type=F32)


def _w_spec(k, tn, layer, w_transposed):
    if w_transposed:
        return pl.BlockSpec((1, tn, k), lambda i, j: (layer, j, 0))
    return pl.BlockSpec((1, k, tn), lambda i, j: (layer, 0, j))


def _matmul_body(a_ref, w_ref, o_ref, *, w_transposed):
    o_ref[...] = _tile_dot(a_ref, w_ref, w_transposed).astype(o_ref.dtype)


def _matmul_res_body(a_ref, w_ref, r_ref, o_ref, *, w_transposed):
    o_ref[...] = (r_ref[...].astype(F32) + _tile_dot(a_ref, w_ref, w_transposed)).astype(o_ref.dtype)


def matmul(a, w, layer=0, n=None, residual=None, w_transposed=False, out_dtype=F32, tm=1024, tn=512):
    m, k = a.shape
    n = w.shape[1 if w_transposed else 2] if n is None else n
    tm = min(tm, m)
    tn = min(tn, n)
    assert m % tm == 0 and n % tn == 0, (m, n, tm, tn)
    in_specs = [pl.BlockSpec((tm, k), lambda i, j: (i, 0)), _w_spec(k, tn, layer, w_transposed)]
    args = [a, w]
    body = functools.partial(_matmul_body, w_transposed=w_transposed)
    if residual is not None:
        in_specs.append(pl.BlockSpec((tm, tn), lambda i, j: (i, j)))
        args.append(residual)
        body = functools.partial(_matmul_res_body, w_transposed=w_transposed)
    return pl.pallas_call(
        body,
        grid=(m // tm, n // tn),
        in_specs=in_specs,
        out_specs=pl.BlockSpec((tm, tn), lambda i, j: (i, j)),
        out_shape=jax.ShapeDtypeStruct((m, n), out_dtype),
        compiler_params=_params("arbitrary", "arbitrary"),
        name="matmul",
    )(*args)


def _matmul_groupnorm_body(a_ref, w_ref, g_ref, o_ref, *, norm_groups, scale, split_out, w_transposed):
    acc = _tile_dot(a_ref, w_ref, w_transposed)
    for c, do_norm in enumerate(norm_groups):
        seg = acc[:, c * LANES:(c + 1) * LANES]
        if do_norm:
            ms = jnp.mean(seg * seg, axis=-1, keepdims=True)
            seg = seg * lax.rsqrt(ms + NORM_EPS) * g_ref[:, c * LANES:(c + 1) * LANES] * scale
        if split_out:
            o_ref[c] = seg.astype(o_ref.dtype)
        else:
            o_ref[:, c * LANES:(c + 1) * LANES] = seg.astype(o_ref.dtype)


def matmul_groupnorm(a, w, gain_cols, norm_groups, layer=0, n=None, w_transposed=False, scale=1.0, split_out=False,
                     out_dtype=BF16, tm=512, tn=512):
    m, k = a.shape
    n = w.shape[1 if w_transposed else 2] if n is None else n
    tm = min(tm, m)
    tn = min(tn, n)
    assert m % tm == 0 and n % tn == 0 and len(norm_groups) == tn // LANES
    if split_out:
        out_shape = jax.ShapeDtypeStruct((n // LANES, m, LANES), out_dtype)
        out_spec = pl.BlockSpec((tn // LANES, tm, LANES), lambda i, j: (j, i, 0))
    else:
        out_shape = jax.ShapeDtypeStruct((m, n), out_dtype)
        out_spec = pl.BlockSpec((tm, tn), lambda i, j: (i, j))
    body = functools.partial(_matmul_groupnorm_body, norm_groups=tuple(norm_groups), scale=scale,
                             split_out=split_out, w_transposed=w_transposed)
    return pl.pallas_call(
        body,
        grid=(m // tm, n // tn),
        in_specs=[pl.BlockSpec((tm, k), lambda i, j: (i, 0)),
                  _w_spec(k, tn, layer, w_transposed),
                  pl.BlockSpec((1, tn), lambda i, j: (0, j))],
        out_specs=out_spec,
        out_shape=out_shape,
        compiler_params=_params("arbitrary", "arbitrary"),
        name="matmul_groupnorm",
    )(a, w, gain_cols.reshape(1, n).astype(F32))


def _pad_cols(w, mult=LANES):
    pad = (-w.shape[-1]) % mult
    return jnp.pad(w, ((0, 0), (0, pad))) if pad else w


def _gla_body(q_ref, k_ref, v_ref, g_ref, a_ref, wg_ref, bg_ref, og_ref, tri_ref, o_ref, s_ref):
    dk, dv = s_ref.shape[1], s_ref.shape[2]
    C = GLA_BLOCK

    @pl.when(pl.program_id(2) == 0)
    def _():
        s_ref[...] = jnp.zeros(s_ref.shape, F32)

    tri = tri_ref[...]
    row = lax.broadcasted_iota(I32, (C, C), 0)
    col = lax.broadcasted_iota(I32, (C, C), 1)
    for c in range(q_ref.shape[0] // C):
        rows = slice(c * C, (c + 1) * C)
        a = a_ref[rows, :].astype(BF16)
        for hh in range(s_ref.shape[0]):
            kc = slice(hh * dk, (hh + 1) * dk)
            vc = slice(hh * dv, (hh + 1) * dv)
            gate_in = jnp.dot(a, wg_ref[:, kc], preferred_element_type=F32) + bg_ref[:, kc]
            log_a = jax.nn.log_sigmoid(gate_in) * (1.0 / GLA_GATE_TAU)
            hi, mid, lo = _split3(log_a)
            bcum = (jnp.dot(tri, hi, preferred_element_type=F32) + jnp.dot(tri, mid, preferred_element_type=F32)
                    + jnp.dot(tri, lo, preferred_element_type=F32))
            b_mid = bcum[C // 2 - 1:C // 2, :]
            q = q_ref[rows, kc].astype(F32) * dk ** -0.5
            k = k_ref[rows, kc].astype(F32)
            v = v_ref[rows, vc]
            att = lax.dot_general((q * jnp.exp(bcum - b_mid)).astype(BF16),
                                  (k * jnp.exp(b_mid - bcum)).astype(BF16),
                                  (((1,), (1,)), ((), ())), preferred_element_type=F32)
            att = jnp.where(col <= row, att, 0.0)
            o = jnp.dot(att.astype(BF16), v, preferred_element_type=F32)
            o = o + jnp.dot((q * jnp.exp(bcum)).astype(BF16), s_ref[hh].astype(BF16), preferred_element_type=F32)
            bcum_t = bcum.T
            b_last = bcum_t[:, C - 1:C]
            k_t = (k.T * jnp.exp(b_last - bcum_t)).astype(BF16)
            s_ref[hh] = s_ref[hh] * jnp.exp(b_last) + jnp.dot(k_t, v, preferred_element_type=F32)
            ms = jnp.mean(o * o, axis=-1, keepdims=True)
            o = o * lax.rsqrt(ms + NORM_EPS) * og_ref[...]
            o_ref[rows, vc] = (o * jax.nn.silu(g_ref[rows, vc].astype(F32))).astype(o_ref.dtype)


def gla_core(proj, a_pad, w_gate2, b_gate2, out_gain, B, T):
    H = GLA_HEADS
    n = proj.shape[0]
    dv = out_gain.shape[0]
    dk = w_gate2.shape[1] // H
    tb = min(GLA_STEP_TOKENS, T)
    nt = T // tb
    wg = jnp.pad(w_gate2, ((0, LANES - w_gate2.shape[0]), (0, 0))).astype(BF16)
    tri = jnp.asarray(np.tril(np.ones((GLA_BLOCK, GLA_BLOCK), np.float32)), BF16)
    hp = GLA_HEADS_PER_STEP
    hs = H // hp
    v0 = 2 * H * dk // (hp * dv)
    return pl.pallas_call(
        _gla_body,
        grid=(B, hs, nt),
        in_specs=[pl.BlockSpec((tb, hp * dk), lambda b, h, i: (b * nt + i, h)),
                  pl.BlockSpec((tb, hp * dk), lambda b, h, i: (b * nt + i, hs + h)),
                  pl.BlockSpec((tb, hp * dv), lambda b, h, i: (b * nt + i, v0 + h)),
                  pl.BlockSpec((tb, hp * dv), lambda b, h, i: (b * nt + i, v0 + hs + h)),
                  pl.BlockSpec((tb, LANES), lambda b, h, i: (b * nt + i, 0)),
                  pl.BlockSpec((LANES, hp * dk), lambda b, h, i: (0, h)),
                  pl.BlockSpec((1, hp * dk), lambda b, h, i: (0, h)),
                  pl.BlockSpec((1, dv), lambda b, h, i: (0, 0)),
                  pl.BlockSpec((GLA_BLOCK, GLA_BLOCK), lambda b, h, i: (0, 0))],
        out_specs=pl.BlockSpec((tb, hp * dv), lambda b, h, i: (b * nt + i, h)),
        out_shape=jax.ShapeDtypeStruct((n, H * dv), BF16),
        scratch_shapes=[pltpu.VMEM((hp, dk, dv), F32)],
        compiler_params=_params("arbitrary", "arbitrary", "arbitrary"),
        name="gla_core",
    )(proj, proj, proj, proj, a_pad, wg, b_gate2.reshape(1, -1).astype(F32), out_gain.reshape(1, dv).astype(F32), tri)


def _nsa_compress_body(u_ref, w1_ref, w2_ref, pe_ref, kg_ref, o_ref):
    kv = pl.program_id(0)
    u = u_ref[0, 0]
    half = u.shape[1]
    w1 = w1_ref[0]
    a = jnp.dot(u, w1[:half], preferred_element_type=F32)
    b = jnp.dot(u, w1[half:], preferred_element_type=F32)
    pe_term = jnp.dot(pe_ref[0], w1, preferred_element_type=F32)[0:1]
    n_chunks = u.shape[0]
    hid = jax.nn.gelu(a + pltpu.roll(b, n_chunks - 1, 0) + pe_term, approximate=True)
    out = jnp.dot(hid.astype(BF16), w2_ref[0], preferred_element_type=F32)
    ms = jnp.mean(out * out, axis=-1, keepdims=True)
    normed = out * lax.rsqrt(ms + NORM_EPS) * kg_ref[...]
    res = jnp.where(kv == 0, normed, out)
    row = lax.broadcasted_iota(I32, res.shape, 0)
    o_ref[0, 0, 0] = jnp.where(row < n_chunks - 1, res, 0.0).astype(o_ref.dtype)


def nsa_compress(zz, cmp_pe, cmp_w1, cmp_w2, kc_gain, B, T):
    G = NSA_GROUPS
    Dh = zz.shape[-1]
    n_chunks = T // CMP_STRIDE
    u = zz.reshape(zz.shape[0], B, n_chunks, CMP_STRIDE * Dh)
    w1 = cmp_w1.reshape(2, CMP_BLOCK * Dh, Dh).astype(BF16)
    pe = jnp.broadcast_to(cmp_pe.reshape(2, 1, CMP_BLOCK * Dh), (2, 8, CMP_BLOCK * Dh)).astype(BF16)
    return pl.pallas_call(
        _nsa_compress_body,
        grid=(2, B, G),
        in_specs=[pl.BlockSpec((1, 1, n_chunks, CMP_STRIDE * Dh), lambda kv, b, g: (kv * G + g, b, 0, 0)),
                  pl.BlockSpec((1, CMP_BLOCK * Dh, Dh), lambda kv, b, g: (kv, 0, 0)),
                  pl.BlockSpec((1, Dh, Dh), lambda kv, b, g: (kv, 0, 0)),
                  pl.BlockSpec((1, 8, CMP_BLOCK * Dh), lambda kv, b, g: (kv, 0, 0)),
                  pl.BlockSpec((1, Dh), lambda kv, b, g: (0, 0))],
        out_specs=pl.BlockSpec((1, 1, 1, n_chunks, Dh), lambda kv, b, g: (kv, b, g, 0, 0)),
        out_shape=jax.ShapeDtypeStruct((2, B, G, n_chunks, Dh), BF16),
        compiler_params=_params("arbitrary", "arbitrary", "arbitrary"),
        name="nsa_compress",
    )(u, w1, cmp_w2.astype(BF16), pe, kc_gain.reshape(1, Dh).astype(F32))


def _dot_nt(a, b):
    return lax.dot_general(a, b, (((1,), (1,)), ((), ())), preferred_element_type=F32)


def _col_softmax_terms(s):
    m = jnp.max(s, axis=0, keepdims=True)
    m = jnp.where(m == NEG_INF, 0.0, m)
    e = jnp.exp2(s - m)
    return e, 1.0 / jnp.maximum(jnp.sum(e, axis=0, keepdims=True), 1e-30)


def _tile_lanes(x, n):
    return jnp.concatenate([x] * n, axis=1)


def _nsa_attn_body(q_ref, qaug_ref, glog_ref, kc_ref, vct_ref, ks_ref, vst_ref, kw_ref, vwt_ref,
                   mselt_ref, efullt_ref, o_ref, q2_ref, acct_ref, m_ref, l_ref, oacct_ref, s_ref, sc_ref, sw_ref,
                   sel_ref):
    qb = pl.program_id(2)
    Dh = LANES
    RB = NSA_ROW_BLOCK
    hpb = RB // Q_BLOCK
    n_rb = NSA_HG // hpb
    n_cmp_pad = kc_ref.shape[2]
    n_slc = mselt_ref.shape[0]
    win_keys = WINDOW + Q_BLOCK
    s0 = qb * Q_BLOCK
    t_q = s0 + lax.broadcasted_iota(I32, (1, Q_BLOCK), 1)

    gates_t = jax.nn.sigmoid(glog_ref[0]).T

    for hg in range(NSA_HG):
        q2_ref[hg * Q_BLOCK:(hg + 1) * Q_BLOCK, :Dh] = q_ref[0, :, hg * Dh:(hg + 1) * Dh]
    q2_ref[:, Dh:] = qaug_ref[0]

    cmp_end = lax.broadcasted_iota(I32, (n_cmp_pad, 1), 0) * CMP_STRIDE + (CMP_BLOCK - 1)
    cmp_mask = _tile_lanes(jnp.where(cmp_end <= t_q, 0.0, NEG_INF), hpb)

    ws = pl.multiple_of(jnp.maximum(s0 - WINDOW, 0), Q_BLOCK)
    wdist = t_q - (ws + lax.broadcasted_iota(I32, (win_keys, 1), 0))
    win_mask = _tile_lanes(jnp.where((wdist >= 0) & (wdist < WINDOW), 0.0, NEG_INF), hpb)

    kc = kc_ref[0, 0]
    vct = jnp.concatenate([vct_ref[0, 0, j] for j in range(n_cmp_pad // Dh)], axis=1)
    kwin = kw_ref[0, 0, pl.ds(ws, win_keys), :]
    wblk = ws // Dh
    vwt = jnp.concatenate([vwt_ref[0, 0, wblk + j] for j in range(win_keys // Dh)], axis=1)

    imp_t = jnp.zeros((n_cmp_pad, Q_BLOCK), F32)
    for rb in range(n_rb):
        q_rb = q2_ref[rb * RB:(rb + 1) * RB, :]
        sc_ref[rb] = _dot_nt(kc, q_rb) + cmp_mask
        sw_ref[rb] = _dot_nt(kwin, q_rb) + win_mask
    for rb in range(n_rb):
        e, inv = _col_softmax_terms(sc_ref[rb])
        p = e * inv
        for j in range(hpb):
            imp_t = imp_t + p[:, j * Q_BLOCK:(j + 1) * Q_BLOCK]
        o_c = jnp.dot(vct, p.astype(BF16), preferred_element_type=F32)
        e, inv = _col_softmax_terms(sw_ref[rb])
        o_w = jnp.dot(vwt, e.astype(BF16), preferred_element_type=F32) * inv
        for j in range(hpb):
            hg = rb * hpb + j
            sub = slice(j * Q_BLOCK, (j + 1) * Q_BLOCK)
            acct_ref[:, hg * Q_BLOCK:(hg + 1) * Q_BLOCK] = (
                gates_t[hg:hg + 1, :] * o_c[:, sub] + gates_t[2 * NSA_HG + hg:2 * NSA_HG + hg + 1, :] * o_w[:, sub])

    hi, mid, lo = _split3(imp_t)
    mselt = mselt_ref[...]
    slc_imp = (jnp.dot(mselt, hi, preferred_element_type=F32) + jnp.dot(mselt, mid, preferred_element_type=F32)
               + jnp.dot(mselt, lo, preferred_element_type=F32))

    blk = lax.broadcasted_iota(I32, (n_slc, 1), 0)
    cur = lax.shift_right_logical(t_q, 6)
    forced = (blk == 0) | (blk == cur) | (blk == cur - 1)
    score = jnp.where(forced, jnp.inf, jnp.where(blk <= cur, slc_imp, NEG_INF))
    sel = jnp.zeros((n_slc, Q_BLOCK), F32)
    for _ in range(SLC_TOPK):
        m = jnp.max(score, axis=0, keepdims=True)
        cand = (score == m) & (m > NEG_INF)
        first = jnp.min(jnp.where(cand, blk, n_slc), axis=0, keepdims=True)
        one = blk == first
        sel = jnp.where(one, 1.0, sel)
        score = jnp.where(one, NEG_INF, score)
    sel_b = sel.astype(BF16)
    sel_ref[...] = sel

    m_ref[...] = jnp.full(m_ref.shape, NEG_INF, F32)
    l_ref[...] = jnp.zeros(l_ref.shape, F32)
    oacct_ref[...] = jnp.zeros(oacct_ref.shape, F32)
    KC = SLC_KEY_CHUNK
    key_iota = lax.broadcasted_iota(I32, (KC, 1), 0)

    def chunk_step(c, carry):
        k0 = pl.multiple_of(c * KC, KC)
        bpc = KC // SLC_BLOCK
        picked = sel_ref[pl.ds(pl.multiple_of(c * bpc, bpc), bpc), :]
        any_picked = jnp.max(jnp.max(picked, axis=0, keepdims=True), axis=1, keepdims=True)[0, 0]

        @pl.when(any_picked > 0.5)
        def _():
            kch = ks_ref[0, 0, pl.ds(k0, KC), :]
            vt = jnp.concatenate([vst_ref[0, 0, c * (KC // Dh) + j] for j in range(KC // Dh)], axis=1)
            selk = jnp.dot(efullt_ref[pl.ds(k0, KC), :], sel_b, preferred_element_type=F32)
            mask = _tile_lanes(jnp.where((selk > 0.5) & (k0 + key_iota <= t_q), 0.0, NEG_INF), hpb)
            for rb in range(n_rb):
                s_ref[rb] = _dot_nt(kch, q2_ref[rb * RB:(rb + 1) * RB, :]) + mask
            for rb in range(n_rb):
                cols = slice(rb * RB, (rb + 1) * RB)
                s = s_ref[rb]
                m_old = m_ref[rb]
                m_new = jnp.maximum(m_old, jnp.max(s, axis=0, keepdims=True))
                m_safe = jnp.where(m_new == NEG_INF, 0.0, m_new)
                alpha = jnp.exp2(m_old - m_safe)
                p = jnp.exp2(s - m_safe)
                l_ref[rb] = alpha * l_ref[rb] + jnp.sum(p, axis=0, keepdims=True)
                oacct_ref[:, cols] = (alpha * oacct_ref[:, cols]
                                      + jnp.dot(vt, p.astype(BF16), preferred_element_type=F32))
                m_ref[rb] = m_new
        return carry

    n_chunks = (s0 + Q_BLOCK + KC - 1) // KC
    lax.fori_loop(0, n_chunks, chunk_step, 0)
    for hg in range(NSA_HG):
        rb, j = divmod(hg, hpb)
        cols = slice(hg * Q_BLOCK, (hg + 1) * Q_BLOCK)
        inv = 1.0 / jnp.maximum(l_ref[rb][:, j * Q_BLOCK:(j + 1) * Q_BLOCK], 1e-30)
        o_t = acct_ref[:, cols] + gates_t[NSA_HG + hg:NSA_HG + hg + 1, :] * (oacct_ref[:, cols] * inv)
        o_ref[0, :, hg * Dh:(hg + 1) * Dh] = o_t.T.astype(o_ref.dtype)


def _pos_pieces(pos):
    pos = np.asarray(pos)
    out = np.zeros((pos.shape[0], LANES), np.float32)
    for i in range(3):
        out[:, 2 * i] = 64 * (pos // 64)
        out[:, 2 * i + 1] = pos % 64
    return jnp.asarray(out, BF16)


def _slope_pieces():
    H, G, HG = NSA_HEADS, NSA_GROUPS, NSA_HG
    slopes = jnp.asarray(LOG2E * 2.0 ** (-8.0 * np.arange(1, H + 1, dtype=np.float64) / H), F32)
    pieces = jnp.stack(_split3(slopes), axis=-1)
    cols = jnp.repeat(pieces, 2, axis=-1)
    cols = jnp.pad(cols, ((0, 0), (0, LANES - cols.shape[-1])))
    return jnp.repeat(cols.reshape(G, HG, 1, LANES), Q_BLOCK, axis=2).reshape(G, HG * Q_BLOCK, LANES)


def _blocked_transpose(v):
    lead, (t, dh) = v.shape[:-2], v.shape[-2:]
    return jnp.swapaxes(v.reshape(*lead, t // LANES, LANES, dh), -1, -2)


def nsa_attention(q, glog, kvc, zz, B, T):
    G, HG, Dh = NSA_GROUPS, NSA_HG, LANES
    H = NSA_HEADS
    n_cmp = T // CMP_STRIDE - CMP_BLOCK // CMP_STRIDE + 1
    n_cmp_pad = -(-T // CMP_STRIDE // LANES) * LANES
    n_slc = T // SLC_BLOCK
    per = SLC_BLOCK // CMP_STRIDE
    n = np.arange(n_cmp_pad)[None, :]
    j = np.arange(n_slc)[:, None]
    mselt = ((n // per == j).astype(np.float32) + ((n + 1) // per == j).astype(np.float32))
    mselt[:, n_cmp:] = 0.0
    kvc = jnp.pad(kvc, ((0, 0), (0, 0), (0, 0), (0, n_cmp_pad - kvc.shape[3]), (0, 0)))
    efullt = (np.arange(T)[:, None] // SLC_BLOCK == np.arange(n_slc)[None, :]).astype(np.float32)

    key_aug = jnp.broadcast_to(_pos_pieces(np.arange(T)), (G, B, T, LANES))
    cmp_aug = jnp.broadcast_to(_pos_pieces(np.arange(n_cmp_pad) * CMP_STRIDE + CMP_BLOCK - 1),
                               (B, G, n_cmp_pad, LANES))
    kc_aug = jnp.concatenate([kvc[0], cmp_aug], axis=-1)
    vct = _blocked_transpose(kvc[1])
    ks_aug = jnp.concatenate([zz[2 * G:3 * G], key_aug], axis=-1)
    kw_aug = jnp.concatenate([zz[4 * G:5 * G], key_aug], axis=-1)
    vst = _blocked_transpose(zz[3 * G:4 * G])
    vwt = _blocked_transpose(zz[5 * G:6 * G])

    def k_spec():
        return pl.BlockSpec((1, 1, T, 2 * Dh), lambda b, g, i: (g, b, 0, 0))

    def vt_spec():
        return pl.BlockSpec((1, 1, T // LANES, Dh, LANES), lambda b, g, i: (g, b, 0, 0, 0))

    n_rb = HG * Q_BLOCK // NSA_ROW_BLOCK
    return pl.pallas_call(
        _nsa_attn_body,
        grid=(B, G, T // Q_BLOCK),
        in_specs=[pl.BlockSpec((1, Q_BLOCK, HG * Dh), lambda b, g, i: (b, i, g)),
                  pl.BlockSpec((1, HG * Q_BLOCK, LANES), lambda b, g, i: (g, 0, 0)),
                  pl.BlockSpec((1, Q_BLOCK, LANES), lambda b, g, i: (b, i, g)),
                  pl.BlockSpec((1, 1, n_cmp_pad, 2 * Dh), lambda b, g, i: (b, g, 0, 0)),
                  pl.BlockSpec((1, 1, n_cmp_pad // LANES, Dh, LANES), lambda b, g, i: (b, g, 0, 0, 0)),
                  k_spec(), vt_spec(), k_spec(), vt_spec(),
                  pl.BlockSpec((n_slc, n_cmp_pad), lambda b, g, i: (0, 0)),
                  pl.BlockSpec((T, n_slc), lambda b, g, i: (0, 0))],
        out_specs=pl.BlockSpec((1, Q_BLOCK, HG * Dh), lambda b, g, i: (b, i, g)),
        out_shape=jax.ShapeDtypeStruct((B, T, H * Dh), BF16),
        scratch_shapes=[pltpu.VMEM((HG * Q_BLOCK, 2 * Dh), BF16),
                        pltpu.VMEM((Dh, HG * Q_BLOCK), F32),
                        pltpu.VMEM((n_rb, 1, NSA_ROW_BLOCK), F32),
                        pltpu.VMEM((n_rb, 1, NSA_ROW_BLOCK), F32),
                        pltpu.VMEM((Dh, HG * Q_BLOCK), F32),
                        pltpu.VMEM((n_rb, SLC_KEY_CHUNK, NSA_ROW_BLOCK), F32),
                        pltpu.VMEM((n_rb, n_cmp_pad, NSA_ROW_BLOCK), F32),
                        pltpu.VMEM((n_rb, WINDOW + Q_BLOCK, NSA_ROW_BLOCK), F32),
                        pltpu.VMEM((n_slc, Q_BLOCK), F32)],
        compiler_params=_params("arbitrary", "arbitrary", "arbitrary"),
        name="nsa_attention",
    )(q, _slope_pieces(), glog, kc_aug, vct, ks_aug, vst, kw_aug, vwt,
      jnp.asarray(mselt, BF16), jnp.asarray(efullt, BF16))


def _pack_rows(y, o_ref):
    m, d = y.shape
    bits = lax.bitcast_convert_type(y.astype(BF16).astype(F32), U32)
    for c in range(d // (2 * LANES)):
        lo = lax.shift_right_logical(bits[:, c * LANES:(c + 1) * LANES], jnp.uint32(16))
        hi = bits[:, d // 2 + c * LANES:d // 2 + (c + 1) * LANES]
        o_ref[pl.ds(c, m, stride=d // (2 * LANES)), :] = lo | hi


def _unpack_chunk(words):
    lo = lax.bitcast_convert_type(lax.shift_left(words, jnp.uint32(16)), F32)
    hi = lax.bitcast_convert_type(words & jnp.uint32(0xFFFF0000), F32)
    return lo, hi


def _moe_router_body(h_ref, g_ref, w_ref, b_ref, tri_ref, xp_ref, idx_ref, wgt_ref, rank_ref, cnt_ref):
    @pl.when(pl.program_id(0) == 0)
    def _():
        cnt_ref[...] = jnp.zeros(cnt_ref.shape, F32)

    x = h_ref[...]
    ms = jnp.mean(x * x, axis=-1, keepdims=True)
    xn = x * lax.rsqrt(ms + NORM_EPS) * g_ref[...]
    _pack_rows(xn, xp_ref)
    logits = jnp.dot(xn.astype(BF16), w_ref[...], preferred_element_type=F32) + b_ref[...]
    lane = lax.broadcasted_iota(I32, logits.shape, 1)
    logits = jnp.where(lane < N_EXPERTS, logits, NEG_INF)
    idx_out = jnp.zeros(logits.shape, I32)
    val_out = jnp.full(logits.shape, NEG_INF, F32)
    picked = []
    for k in range(TOP_K):
        m = jnp.max(logits, axis=-1, keepdims=True)
        first = jnp.min(jnp.where(logits == m, lane, LANES), axis=-1, keepdims=True)
        idx_out = jnp.where(lane == k, first, idx_out)
        val_out = jnp.where(lane == k, m, val_out)
        picked.append(jnp.where(lane == first, 1.0, 0.0))
        logits = jnp.where(lane == first, NEG_INF, logits)
    e = jnp.exp(val_out - jnp.max(val_out, axis=-1, keepdims=True))
    idx_ref[...] = idx_out
    wgt_ref[...] = e * (1.0 / jnp.sum(e, axis=-1, keepdims=True))

    base = cnt_ref[0:1, :]
    rank_out = jnp.zeros(logits.shape, F32)
    for k in range(TOP_K):
        earlier = jnp.dot(tri_ref[...], picked[k].astype(BF16), preferred_element_type=F32)
        r = jnp.sum(picked[k] * (base + earlier), axis=-1, keepdims=True)
        rank_out = jnp.where(lane == k, r, rank_out)
        base = base + jnp.sum(picked[k], axis=0, keepdims=True)
    rank_ref[...] = rank_out.astype(I32)
    cnt_ref[...] = jnp.broadcast_to(base, cnt_ref.shape)


def moe_router(h, gain, w_router, b_router, tm=256):
    n, d = h.shape
    tm = min(tm, n)
    pr = d // (2 * LANES)
    w = _pad_cols(w_router).astype(BF16)
    b = _pad_cols(b_router.reshape(1, -1)).astype(F32)
    tri = jnp.asarray(np.tril(np.ones((tm, tm), np.float32), -1), BF16)
    return pl.pallas_call(
        _moe_router_body,
        grid=(n // tm,),
        in_specs=[pl.BlockSpec((tm, d), lambda i: (i, 0)),
                  pl.BlockSpec((1, d), lambda i: (0, 0)),
                  pl.BlockSpec((d, LANES), lambda i: (0, 0)),
                  pl.BlockSpec((1, LANES), lambda i: (0, 0)),
                  pl.BlockSpec((tm, tm), lambda i: (0, 0))],
        out_specs=[pl.BlockSpec((tm * pr, LANES), lambda i: (i, 0)),
                   pl.BlockSpec((tm, LANES), lambda i: (i, 0)),
                   pl.BlockSpec((tm, LANES), lambda i: (i, 0)),
                   pl.BlockSpec((tm, LANES), lambda i: (i, 0)),
                   pl.BlockSpec((8, LANES), lambda i: (0, 0))],
        out_shape=[jax.ShapeDtypeStruct((n * pr, LANES), U32),
                   jax.ShapeDtypeStruct((n, LANES), I32), jax.ShapeDtypeStruct((n, LANES), F32),
                   jax.ShapeDtypeStruct((n, LANES), I32), jax.ShapeDtypeStruct((8, LANES), F32)],
        compiler_params=_params("arbitrary"),
        name="moe_router",
    )(h, gain.reshape(1, d).astype(F32), w, b, tri)


def _row_gather_copy(src_hbm, src_row, dst_buf, slot, r, sem, pr):
    return pltpu.make_async_copy(src_hbm.at[pl.ds(pl.multiple_of(src_row, pr), pr)],
                                 dst_buf.at[slot, pl.ds(pl.multiple_of(r * pr, pr), pr)], sem.at[slot])


def _start_row_gather(ids_ref, src_hbm, dst_buf, slot, sem, n_rows, pr):
    def body(r, carry):
        _row_gather_copy(src_hbm, ids_ref[0, 0, r], dst_buf, slot, r, sem, pr).start()
        return carry
    lax.fori_loop(0, n_rows, body, 0, unroll=8)


def _wait_row_gather(src_hbm, dst_buf, slot, sem):
    pltpu.make_async_copy(src_hbm.at[pl.ds(0, dst_buf.shape[1])], dst_buf.at[slot], sem.at[slot]).wait()


def _moe_dispatch_body(n_used_ref, ids0_ref, idsn_ref, x_hbm, xs, xbuf, sem):
    i = pl.program_id(0)
    n_used = n_used_ref[0]
    rows, d = xs.shape
    pr = d // (2 * LANES)

    @pl.when(i == 0)
    def _():
        _start_row_gather(ids0_ref, x_hbm, xbuf, 0, sem, rows, pr)

    @pl.when(i + 1 < n_used)
    def _():
        _start_row_gather(idsn_ref, x_hbm, xbuf, (i + 1) % 2, sem, rows, pr)

    @pl.when(i < n_used)
    def _():
        slot = i % 2
        _wait_row_gather(x_hbm, xbuf, slot, sem)
        for c in range(pr):
            lo, hi = _unpack_chunk(xbuf[slot, pl.ds(c, rows, stride=pr), :])
            xs[:, c * LANES:(c + 1) * LANES] = lo.astype(BF16)
            xs[:, d // 2 + c * LANES:d // 2 + (c + 1) * LANES] = hi.astype(BF16)

    @pl.when(i >= n_used)
    def _():
        xs[...] = jnp.zeros(xs.shape, xs.dtype)


def moe_dispatch(xp, tok_rows, n_used, d):
    pr = d // (2 * LANES)
    rows = MOE_BLOCK_ROWS
    n_blocks = tok_rows.shape[0] // rows
    ids = tok_rows.reshape(n_blocks, 1, rows)
    grid_spec = pltpu.PrefetchScalarGridSpec(
        num_scalar_prefetch=1,
        grid=(n_blocks,),
        in_specs=[pl.BlockSpec((1, 1, rows), lambda i, nu: (0, 0, 0), memory_space=pltpu.SMEM),
                  pl.BlockSpec((1, 1, rows), lambda i, nu: (jnp.minimum(i + 1, n_blocks - 1), 0, 0),
                               memory_space=pltpu.SMEM),
                  pl.BlockSpec(memory_space=pl.ANY)],
        out_specs=pl.BlockSpec((rows, d), lambda i, nu: (i, 0)),
        scratch_shapes=[pltpu.VMEM((2, rows * pr, LANES), U32), pltpu.SemaphoreType.DMA((2,))],
    )
    return pl.pallas_call(
        _moe_dispatch_body,
        grid_spec=grid_spec,
        out_shape=jax.ShapeDtypeStruct((n_blocks * rows, d), BF16),
        compiler_params=_params("arbitrary"),
        name="moe_dispatch",
    )(n_used, ids, ids, xp)


def _moe_expert_body(blk_e_ref, n_used_ref, xs, wgu_ref, bgu_ref, wd_ref, bd_ref, y_ref):
    i = pl.program_id(0)
    n_used = n_used_ref[0]
    rows, d = xs.shape
    pr = d // (2 * LANES)
    ff = wd_ref.shape[2]

    @pl.when(i < n_used)
    def _():
        sub = rows // MOE_SUB_BLOCKS
        for sb in range(MOE_SUB_BLOCKS):
            r0 = sb * sub
            gu = jnp.dot(xs[r0:r0 + sub, :], wgu_ref[0, 0], preferred_element_type=F32) + bgu_ref[0]
            gate = jnp.minimum(gu[:, :ff], SWIGLU_LIMIT)
            up = jnp.clip(gu[:, ff:], -SWIGLU_LIMIT, SWIGLU_LIMIT)
            act = (up + 1.0) * (gate * jax.nn.sigmoid(SWIGLU_ALPHA * gate))
            y = jnp.dot(act.astype(BF16), wd_ref[0, 0], preferred_element_type=F32) + bd_ref[0]
            _pack_rows(y, y_ref.at[pl.ds(r0 * pr, sub * pr), :])

    @pl.when(i >= n_used)
    def _():
        y_ref[...] = jnp.zeros(y_ref.shape, y_ref.dtype)


def moe_experts(xp, tok_rows, blk_e, n_used, w_gate_up, b_gate_up, w_down, b_down, l):
    _, E, d, ff2 = w_gate_up.shape
    ff = ff2 // 2
    pr = d // (2 * LANES)
    rows = MOE_BLOCK_ROWS
    n_blocks = tok_rows.shape[0] // rows
    xs = moe_dispatch(xp, tok_rows, n_used, d)

    def used(i, nu):
        return jnp.minimum(i, nu[0] - 1)

    grid_spec = pltpu.PrefetchScalarGridSpec(
        num_scalar_prefetch=2,
        grid=(n_blocks,),
        in_specs=[pl.BlockSpec((rows, d), lambda i, be, nu: (used(i, nu), 0)),
                  pl.BlockSpec((1, 1, d, ff2), lambda i, be, nu: (l, be[used(i, nu)], 0, 0)),
                  pl.BlockSpec((1, 1, ff2), lambda i, be, nu: (be[used(i, nu)], 0, 0)),
                  pl.BlockSpec((1, 1, ff, d), lambda i, be, nu: (l, be[used(i, nu)], 0, 0)),
                  pl.BlockSpec((1, 1, d), lambda i, be, nu: (be[used(i, nu)], 0, 0))],
        out_specs=pl.BlockSpec((rows * pr, LANES), lambda i, be, nu: (i, 0)),
    )
    return pl.pallas_call(
        _moe_expert_body,
        grid_spec=grid_spec,
        out_shape=jax.ShapeDtypeStruct((n_blocks * rows * pr, LANES), U32),
        compiler_params=_params("arbitrary"),
        name="moe_experts",
    )(blk_e, n_used, xs, w_gate_up, b_gate_up.reshape(E, 1, ff2).astype(F32),
      w_down, b_down.reshape(E, 1, d).astype(F32))


def _moe_combine_body(ids0_ref, idsn_ref, y_hbm, h_ref, w_ref, g_ref, o_ref, *rest):
    xn_refs, (ybuf, sem) = rest[:-2], rest[-2:]
    i = pl.program_id(0)
    n_steps = pl.num_programs(0)
    tt, d = h_ref.shape
    pr = d // (2 * LANES)
    rows = TOP_K * tt

    @pl.when(i == 0)
    def _():
        _start_row_gather(ids0_ref, y_hbm, ybuf, 0, sem, rows, pr)

    @pl.when(i + 1 < n_steps)
    def _():
        _start_row_gather(idsn_ref, y_hbm, ybuf, (i + 1) % 2, sem, rows, pr)

    slot = i % 2
    _wait_row_gather(y_hbm, ybuf, slot, sem)
    w = w_ref[...]
    wk = [jnp.broadcast_to(w[:, k:k + 1], (tt, LANES)) for k in range(TOP_K)]
    ssq = jnp.zeros((tt, LANES), F32)
    for c in range(pr):
        lo_cols = slice(c * LANES, (c + 1) * LANES)
        hi_cols = slice(d // 2 + c * LANES, d // 2 + (c + 1) * LANES)
        acc_lo = h_ref[:, lo_cols]
        acc_hi = h_ref[:, hi_cols]
        for k in range(TOP_K):
            lo, hi = _unpack_chunk(ybuf[slot, pl.ds(k * tt * pr + c, tt, stride=pr), :])
            acc_lo = acc_lo + wk[k] * lo
            acc_hi = acc_hi + wk[k] * hi
        o_ref[:, lo_cols] = acc_lo
        o_ref[:, hi_cols] = acc_hi
        ssq = ssq + acc_lo * acc_lo + acc_hi * acc_hi
    if xn_refs:
        inv = lax.rsqrt(jnp.sum(ssq, axis=-1, keepdims=True) * (1.0 / d) + NORM_EPS)
        for j, xn_ref in enumerate(xn_refs):
            xn_ref[...] = (o_ref[...] * inv * g_ref[j:j + 1, :]).astype(xn_ref.dtype)


def moe_combine(yp, slot_rows, weights, h, next_gains):
    n, d = h.shape
    n_norm = len(next_gains)
    gains = (jnp.stack(next_gains) if next_gains else jnp.zeros((1, d))).astype(F32)
    pr = d // (2 * LANES)
    tt = min(MOE_COMBINE_TOKENS, n)
    n_steps = n // tt
    ids = slot_rows.reshape(n_steps, tt, TOP_K).transpose(0, 2, 1).reshape(n_steps, 1, TOP_K * tt)
    return pl.pallas_call(
        _moe_combine_body,
        grid=(n_steps,),
        in_specs=[pl.BlockSpec((1, 1, TOP_K * tt), lambda i: (0, 0, 0), memory_space=pltpu.SMEM),
                  pl.BlockSpec((1, 1, TOP_K * tt), lambda i: (jnp.minimum(i + 1, n_steps - 1), 0, 0),
                               memory_space=pltpu.SMEM),
                  pl.BlockSpec(memory_space=pl.ANY),
                  pl.BlockSpec((tt, d), lambda i: (i, 0)),
                  pl.BlockSpec((tt, LANES), lambda i: (i, 0)),
                  pl.BlockSpec(gains.shape, lambda i: (0, 0))],
        out_specs=[pl.BlockSpec((tt, d), lambda i: (i, 0))] * (1 + n_norm),
        out_shape=[jax.ShapeDtypeStruct((n, d), F32)] + [jax.ShapeDtypeStruct((n, d), BF16)] * n_norm,
        scratch_shapes=[pltpu.VMEM((2, TOP_K * tt * pr, LANES), U32), pltpu.SemaphoreType.DMA((2,))],
        compiler_params=_params("arbitrary"),
        name="moe_combine",
    )(ids, ids, yp, h, weights, gains)


def _moe_plan(top_idx, rank, sizes, n_rows):
    E, rows = N_EXPERTS, MOE_BLOCK_ROWS
    flat_e = top_idx.reshape(-1)
    rank = rank.reshape(-1)
    nk = flat_e.shape[0]
    nblk = (sizes + rows - 1) // rows
    blk_end = jnp.cumsum(nblk)
    pad_start = (blk_end - nblk) * rows
    slot_dest = pad_start[flat_e] + rank
    tok_buf = jnp.zeros((n_rows,), I32).at[slot_dest].set(jnp.arange(nk, dtype=I32) // TOP_K)
    n_blocks = n_rows // rows
    blk_e = jnp.sum((blk_end[None, :] <= jnp.arange(n_blocks, dtype=I32)[:, None]).astype(I32), axis=1)
    blk_e = jnp.minimum(blk_e, E - 1)
    n_used = blk_end[-1:].astype(I32)
    return slot_dest, tok_buf, blk_e, n_used


def moe_layer(h, gain, w_router, b_router, w_gate_up, b_gate_up, w_down, b_down, l, next_gains):
    n, d = h.shape
    pr = d // (2 * LANES)
    rows = MOE_BLOCK_ROWS
    xp, top_idx, weights, rank, counts = moe_router(h, gain, w_router, b_router)
    n_rows = n * TOP_K + N_EXPERTS * rows
    slot_dest, tok_buf, blk_e, n_used = _moe_plan(top_idx[:, :TOP_K], rank[:, :TOP_K],
                                                  counts[0, :N_EXPERTS].astype(I32), n_rows)
    yp = moe_experts(xp, tok_buf * pr, blk_e, n_used, w_gate_up, b_gate_up, w_down, b_down, l)
    return moe_combine(yp, slot_dest.reshape(n, TOP_K) * pr, weights, h, next_gains)


def _gla_mixer(h, xn, w_in, w_gate2, b_gate2, out_gain, w_out, l, B, T):
    n_main = w_in.shape[2] - GLA_GATE_RANK
    w_in_t = jnp.swapaxes(w_in, 1, 2)
    proj = matmul(xn, w_in_t, layer=l, n=n_main, w_transposed=True, out_dtype=BF16)
    wa_t = jnp.pad(w_in_t[l, n_main:], ((0, LANES - GLA_GATE_RANK), (0, 0)))
    a_pad = matmul(xn, wa_t[None], w_transposed=True)
    o = gla_core(proj, a_pad, w_gate2, b_gate2, out_gain, B, T)
    return matmul(o, w_out, layer=l, residual=h)


def _nsa_shared_kv(xn, w_kv, k_gain, cmp_pe, cmp_w1, cmp_w2, B, T):
    G = NSA_GROUPS
    Dh = LANES
    ones = jnp.ones((G * Dh,), F32)
    gain_cols = jnp.concatenate([ones, ones, jnp.tile(k_gain[1], G), ones, jnp.tile(k_gain[2], G), ones])
    norm_groups = [False] * (2 * G) + [True] * G + [False] * G + [True] * G + [False] * G
    zz = matmul_groupnorm(xn, w_kv.astype(BF16)[None], gain_cols, norm_groups, split_out=True, tn=6 * G * Dh)
    kvc = nsa_compress(zz, cmp_pe, cmp_w1, cmp_w2, k_gain[0], B, T)
    return kvc, zz.reshape(6 * G, B, T, Dh)


def _nsa_mixer(h, xn, w_in, q_gain, w_out, kvc, zz, j, B, T):
    H, G, HG, Dh = NSA_HEADS, NSA_GROUPS, NSA_HG, LANES
    w_in_t = jnp.swapaxes(w_in, 1, 2)
    q = matmul_groupnorm(xn, w_in_t, jnp.tile(q_gain, H), [True] * 4, layer=j, n=H * Dh, w_transposed=True,
                         scale=Dh ** -0.5 * LOG2E, tm=1024, tn=512)
    wg = w_in_t[j, H * Dh:].reshape(G, HG, 3, -1).transpose(0, 2, 1, 3).reshape(G, 3 * HG, -1)
    wg = jnp.pad(wg, ((0, 0), (0, LANES - 3 * HG), (0, 0))).reshape(1, G * LANES, -1)
    glog = matmul(xn, wg, w_transposed=True)
    o = nsa_attention(q.reshape(B, T, H * Dh), glog.reshape(B, T, G * LANES), kvc, zz, B, T)
    return matmul(o.reshape(B * T, H * Dh), w_out, layer=j, residual=h)


def kernel(x, ln_mix, ln_ffn, a_w_in, a_w_gate2, a_b_gate2, a_out_gain, a_w_out, kv_gain, w_kv, k_gain,
           cmp_pe, cmp_w1, cmp_w2, b_w_in, b_q_gain, b_w_out, w_router, b_router, w_gate_up, b_gate_up,
           w_down, b_down):
    B, T, D = x.shape
    depth = ln_mix.shape[0]
    n_a = a_w_in.shape[0]
    h = x.reshape(B * T, D)
    shared = None
    wgu_bf = w_gate_up.astype(BF16)
    wd_bf = w_down.astype(BF16)
    xn = rmsnorm(h, ln_mix[0])
    xn_kv = rmsnorm(h, kv_gain) if n_a == 0 else None
    for l in range(depth):
        if l < n_a:
            h = _gla_mixer(h, xn, a_w_in, a_w_gate2[l], a_b_gate2[l], a_out_gain[l], a_w_out, l, B, T)
        else:
            if l == n_a:
                shared = _nsa_shared_kv(xn_kv, w_kv, k_gain, cmp_pe, cmp_w1, cmp_w2, B, T)
            j = l - n_a
            h = _nsa_mixer(h, xn, b_w_in, b_q_gain[j], b_w_out, *shared, j, B, T)
        next_gains = [ln_mix[l + 1]] if l + 1 < depth else []
        if l + 1 == n_a:
            next_gains.append(kv_gain)
        h, *normed = moe_layer(h, ln_ffn[l], w_router[l], b_router[l], wgu_bf, b_gate_up[l], wd_bf, b_down[l], l,
                               next_gains)
        if normed:
            xn = normed[0]
        if l + 1 == n_a:
            xn_kv = normed[1]
    return h.reshape(B, T, D)
```
